```python
import math
import jax, jax.numpy as jnp
from jax import lax
import numpy as np

D_MODEL = 1024
BATCH = 8
SEQ = 4096
DEPTH = 2

N_EVEN = (DEPTH + 1) // 2
N_ODD = DEPTH // 2

MIX_WIDTH = D_MODEL
A_WIDTH = MIX_WIDTH // 2
A_GROUPS = 4
A_GROUP_DIM = A_WIDTH // A_GROUPS
CHUNK = 128
B_WIDTH = MIX_WIDTH - A_WIDTH
POOL_WINDOWS = (2, 4, 8, 16)
B_GROUPS = len(POOL_WINDOWS)
B_GROUP_DIM = B_WIDTH // B_GROUPS
C_HEADS = 4
C_QK_DIM = 64
C_V_DIM = 2 * C_QK_DIM
C_WIDTH = C_HEADS * C_V_DIM
C_QK_WIDTH = C_HEADS * 2 * C_QK_DIM
Q_BLOCK = 128
D_WIDTH = MIX_WIDTH - C_WIDTH
CONV_WIDTH = 3
REL_BUCKETS = 32
REL_MAX_DIST = 128
D_FF = 4 * D_MODEL
EPS = 1e-6

EVEN_IN = 2 * A_WIDTH + B_WIDTH
ODD_IN = 2 * C_QK_WIDTH + C_WIDTH + 3 * D_WIDTH

kernel_name = "hybrid_gmlp_pool_diffattn_shortconv"


def rms_norm(x, g):
    xf = x.astype(jnp.float32)
    y = xf * lax.rsqrt(jnp.mean(xf * xf, axis=-1, keepdims=True) + EPS)
    return (y * g.astype(jnp.float32)).astype(x.dtype)


def layer_norm(x, g, b):
    xf = x.astype(jnp.float32)
    mu = jnp.mean(xf, axis=-1, keepdims=True)
    xc = xf - mu
    y = xc * lax.rsqrt(jnp.mean(xc * xc, axis=-1, keepdims=True) + EPS)
    return (y * g.astype(jnp.float32) + b.astype(jnp.float32)).astype(x.dtype)


def chunked_gmlp(z, ln_g, ln_b, w_s, b_s):
    bsz, s, _ = z.shape
    z = jax.nn.gelu(z)
    u, v = jnp.split(z, 2, axis=-1)
    v = layer_norm(v, ln_g, ln_b)
    v = v.reshape(bsz, s // CHUNK, CHUNK, A_GROUPS, A_GROUP_DIM)
    causal = jnp.tril(jnp.ones((CHUNK, CHUNK), dtype=bool))
    w = jnp.where(causal[None], w_s, 0).astype(v.dtype)
    mixed = jnp.einsum('gts,bnsgc->bntgc', w, v) + b_s.T[None, None, :, :, None].astype(v.dtype)
    return u * mixed.reshape(bsz, s, A_WIDTH)


def multiscale_pool(p, pool_w, pool_scale):
    bsz, s, _ = p.shape
    pos = jnp.arange(s)
    outs = []
    for g_in, w in zip(jnp.split(p, B_GROUPS, axis=-1), POOL_WINDOWS):
        gf = g_in.astype(jnp.float32)
        c = jnp.cumsum(gf, axis=1)
        shifted = jnp.pad(c[:, :s - w], ((0, 0), (w, 0), (0, 0)))
        count = jnp.minimum(pos + 1, w).astype(jnp.float32)[None, :, None]
        outs.append(((c - shifted) / count - gf).astype(p.dtype))
    pooled = jnp.stack(outs, axis=2)
    y = jnp.einsum('bsgc,gcd->bsgd', pooled, pool_w).reshape(bsz, s, B_WIDTH)
    return y * pool_scale


def t5_bucket(dist):
    max_exact = REL_BUCKETS // 2
    nf = jnp.maximum(dist, 1).astype(jnp.float32)
    large = max_exact + (jnp.log(nf / max_exact) / math.log(REL_MAX_DIST / max_exact)
                         * (REL_BUCKETS - max_exact)).astype(jnp.int32)
    large = jnp.minimum(large, REL_BUCKETS - 1)
    return jnp.where(dist < max_exact, dist, large)


def diff_attention(q, k, v, rel_table, lam, lambda_init, subln_g):
    bsz, s = q.shape[0], q.shape[1]
    q = q.transpose(0, 2, 3, 1, 4) * (C_QK_DIM ** -0.5)
    k = k.transpose(0, 2, 3, 1, 4)
    v = v.transpose(0, 2, 1, 3)
    outs = []
    for blk in range(s // Q_BLOCK):
        q0 = blk * Q_BLOCK
        end = q0 + Q_BLOCK
        logits = jnp.einsum('bhmqd,bhmkd->bhmqk', q[:, :, :, q0:end], k[:, :, :, :end]).astype(jnp.float32)
        dist = jnp.arange(q0, end)[:, None] - jnp.arange(end)[None, :]
        bias = rel_table[t5_bucket(jnp.maximum(dist, 0))].astype(jnp.float32)
        logits = logits + bias.transpose(2, 0, 1)[None, :, None]
        logits = jnp.where(dist >= 0, logits, -jnp.inf)
        p = jax.nn.softmax(logits, axis=-1)
        a = p[:, :, 0] - lam * p[:, :, 1]
        outs.append(jnp.einsum('bhqk,bhkd->bhqd', a.astype(v.dtype), v[:, :, :end]))
    o = jnp.concatenate(outs, axis=2)
    o = rms_norm(o, subln_g) * (1.0 - lambda_init)
    return o.transpose(0, 2, 1, 3).reshape(bsz, s, C_WIDTH)


def short_gated_conv(bg, cg, xin, conv_w):
    z = cg * xin
    s = z.shape[1]
    y = conv_w[CONV_WIDTH - 1] * z
    for j in range(CONV_WIDTH - 1):
        shift = CONV_WIDTH - 1 - j
        y = y + conv_w[j] * jnp.pad(z[:, :s - shift], ((0, 0), (shift, 0), (0, 0)))
    return bg * y


def even_mixer(h, w_in, ln_g, ln_b, w_s, b_s, pool_w, pool_scale, w_out):
    proj = h @ w_in
    ya = chunked_gmlp(proj[..., :2 * A_WIDTH], ln_g, ln_b, w_s, b_s)
    yb = multiscale_pool(proj[..., 2 * A_WIDTH:], pool_w, pool_scale)
    return jnp.concatenate([ya, yb], axis=-1) @ w_out


def odd_mixer(h, w_in, rel_table, lam_params, subln_g, conv_w, w_out, lambda_init):
    bsz, s, _ = h.shape
    proj = h @ w_in
    splits = np.cumsum([C_QK_WIDTH, C_QK_WIDTH, C_WIDTH, D_WIDTH, D_WIDTH]).tolist()
    q, k, v, bg, cg, xin = jnp.split(proj, splits, axis=-1)
    q = q.reshape(bsz, s, C_HEADS, 2, C_QK_DIM)
    k = k.reshape(bsz, s, C_HEADS, 2, C_QK_DIM)
    v = v.reshape(bsz, s, C_HEADS, C_V_DIM)
    lp = lam_params.astype(jnp.float32)
    lam = jnp.exp(jnp.sum(lp[0] * lp[1])) - jnp.exp(jnp.sum(lp[2] * lp[3])) + lambda_init
    yc = diff_attention(q, k, v, rel_table, lam, lambda_init, subln_g)
    yd = short_gated_conv(bg, cg, xin, conv_w)
    return jnp.concatenate([yc, yd], axis=-1) @ w_out


def squared_relu_mlp(h, w1, w2):
    return jnp.square(jax.nn.relu(h @ w1)) @ w2


def setup_inputs(seed: int = 0) -> dict:
    key = jax.random.key(seed)
    ks = jax.random.split(key, 20)
    nrm = lambda k, shape, scale: jax.random.normal(k, shape, jnp.float32) * scale
    gain = lambda k, shape: 1.0 + 0.05 * jax.random.normal(k, shape, jnp.float32)
    return {
        "x": nrm(ks[0], (BATCH, SEQ, D_MODEL), 1.0),
        "rel_bias_table": nrm(ks[1], (REL_BUCKETS, C_HEADS), 0.5),
        "norm_g": gain(ks[2], (DEPTH, 4, D_MODEL)),
        "even_w_in": nrm(ks[3], (N_EVEN, D_MODEL, EVEN_IN), D_MODEL ** -0.5),
        "even_ln_g": gain(ks[4], (N_EVEN, A_WIDTH)),
        "even_ln_b": nrm(ks[5], (N_EVEN, A_WIDTH), 0.02),
        "even_spatial_w": nrm(ks[6], (N_EVEN, A_GROUPS, CHUNK, CHUNK), CHUNK ** -0.5),
        "even_spatial_b": gain(ks[7], (N_EVEN, A_GROUPS, CHUNK)),
        "even_pool_w": nrm(ks[8], (N_EVEN, B_GROUPS, B_GROUP_DIM, B_GROUP_DIM), B_GROUP_DIM ** -0.5),
        "even_pool_scale": gain(ks[9], (N_EVEN, B_WIDTH)),
        "even_w_out": nrm(ks[10], (N_EVEN, MIX_WIDTH, D_MODEL), MIX_WIDTH ** -0.5),
        "odd_w_in": nrm(ks[11], (N_ODD, D_MODEL, ODD_IN), D_MODEL ** -0.5),
        "odd_lambda": nrm(ks[12], (N_ODD, 4, C_QK_DIM), 0.1),
        "odd_subln_g": gain(ks[13], (N_ODD, C_V_DIM)),
        "odd_conv_w": nrm(ks[14], (N_ODD, CONV_WIDTH, D_WIDTH), CONV_WIDTH ** -0.5),
        "odd_w_out": nrm(ks[15], (N_ODD, MIX_WIDTH, D_MODEL), MIX_WIDTH ** -0.5),
        "ffn_w1": nrm(ks[16], (DEPTH, D_MODEL, D_FF), D_MODEL ** -0.5),
        "ffn_w2": nrm(ks[17], (DEPTH, D_FF, D_MODEL), D_FF ** -0.5),
    }


def reference(x, rel_bias_table, norm_g, even_w_in, even_ln_g, even_ln_b, even_spatial_w,
              even_spatial_b, even_pool_w, even_pool_scale, even_w_out, odd_w_in, odd_lambda,
              odd_subln_g, odd_conv_w, odd_w_out, ffn_w1, ffn_w2):
    h = x
    for layer in range(DEPTH):
        g = norm_g[layer]
        hn = rms_norm(h, g[0])
        if layer % 2 == 0:
            e = layer // 2
            y = even_mixer(hn, even_w_in[e], even_ln_g[e], even_ln_b[e], even_spatial_w[e],
                           even_spatial_b[e], even_pool_w[e], even_pool_scale[e], even_w_out[e])
        else:
            o = layer // 2
            lambda_init = 0.8 - 0.6 * math.exp(-0.3 * layer)
            y = odd_mixer(hn, odd_w_in[o], rel_bias_table, odd_lambda[o], odd_subln_g[o],
                          odd_conv_w[o], odd_w_out[o], lambda_init)
        h = h + rms_norm(y, g[1])
        y = squared_relu_mlp(rms_norm(h, g[2]), ffn_w1[layer], ffn_w2[layer])
        h = h + rms_norm(y, g[3])
    return h
```

```python
import functools
import math

import numpy as np
import jax
import jax.numpy as jnp
from jax import lax
from jax.experimental import pallas as pl
from jax.experimental.pallas import tpu as pltpu

D_MODEL = 1024
A_WIDTH = 512
A_GROUPS = 4
A_GROUP_DIM = 128
CHUNK = 128
B_WIDTH = 512
POOL_WINDOWS = (2, 4, 8, 16)
B_GROUP_DIM = 128
C_HEADS = 4
C_QK_DIM = 64
C_V_DIM = 128
C_WIDTH = 512
D_WIDTH = 512
CONV_WIDTH = 3
REL_BUCKETS = 32
REL_MAX_DIST = 128
D_FF = 4096
EPS = 1e-6
EVEN_IN = 2 * A_WIDTH + B_WIDTH
ODD_IN = 2 * C_WIDTH + C_WIDTH + 3 * D_WIDTH

ROW_TILE = 512
ATT_TILE = 256
FF_CHUNK = 1024
POOL_HALO = 16
CONV_HALO = 8
MASK_VALUE = -1e30
VMEM_LIMIT_BYTES = 52 * 1024 * 1024

_F32 = jnp.float32
_BF16 = jnp.bfloat16


def _rms(x, g):
    return x * lax.rsqrt(jnp.mean(x * x, axis=-1, keepdims=True) + EPS) * g


def _const_spec(shape):
    nd = len(shape)
    return pl.BlockSpec(shape, lambda *_: (0,) * nd, pipeline_mode=pl.Buffered(1))


def _row_spec(width, col=0):
    return pl.BlockSpec((None, ROW_TILE, width), lambda b, j: (b, j, col))


def _even_mixer_kernel(x_ref, g_ref, w_in_ref, ln_g_ref, ln_b_ref, ws_ref, bs_ref,
                       pw_ref, ps_ref, ya_ref, yb_ref, carry_ref, ext_ref):
    j = pl.program_id(1)
    hn = _rms(x_ref[...], g_ref[...]).astype(_BF16)
    proj = jnp.dot(hn, w_in_ref[...], preferred_element_type=_F32)

    z = jax.nn.gelu(proj[:, :2 * A_WIDTH])
    u = z[:, :A_WIDTH]
    v = z[:, A_WIDTH:]
    mu = jnp.mean(v, axis=-1, keepdims=True)
    vc = v - mu
    vn = vc * lax.rsqrt(jnp.mean(vc * vc, axis=-1, keepdims=True) + EPS)
    vn = (vn * ln_g_ref[...] + ln_b_ref[...]).astype(_BF16)
    row = lax.broadcasted_iota(jnp.int32, (CHUNK, CHUNK), 0)
    col = lax.broadcasted_iota(jnp.int32, (CHUNK, CHUNK), 1)
    for g in range(A_GROUPS):
        cs = slice(g * A_GROUP_DIM, (g + 1) * A_GROUP_DIM)
        w = jnp.where(row >= col, ws_ref[g], 0.0).astype(_BF16)
        bias = bs_ref[:, g:g + 1]
        for c in range(ROW_TILE // CHUNK):
            rs = slice(c * CHUNK, (c + 1) * CHUNK)
            mixed = jnp.dot(w, vn[rs, cs], preferred_element_type=_F32) + bias
            ya_ref[rs, cs] = (u[rs, cs] * mixed).astype(_BF16)

    p = proj[:, 2 * A_WIDTH:]

    @pl.when(j == 0)
    def _():
        carry_ref[...] = jnp.zeros_like(carry_ref)

    ext_ref[0:POOL_HALO, :] = carry_ref[...]
    ext_ref[POOL_HALO:, :] = p
    carry_ref[...] = p[ROW_TILE - POOL_HALO:, :]
    pos = j * ROW_TILE + lax.broadcasted_iota(jnp.int32, (ROW_TILE, B_GROUP_DIM), 0)
    for g, win in enumerate(POOL_WINDOWS):
        cs = slice(g * B_GROUP_DIM, (g + 1) * B_GROUP_DIM)
        tok = p[:, cs]
        acc = tok
        for k in range(1, win):
            acc = acc + ext_ref[POOL_HALO - k:POOL_HALO - k + ROW_TILE, cs]
        count = jnp.minimum(pos + 1, win).astype(_F32)
        pooled = (acc / count - tok).astype(_BF16)
        y = jnp.dot(pooled, pw_ref[g], preferred_element_type=_F32)
        yb_ref[:, cs] = (y * ps_ref[:, cs]).astype(_BF16)


def _even_mixer(h, g0, w_in, ln_g, ln_b, w_s, b_s_t, pool_w, pool_scale):
    bsz, seq, _ = h.shape
    grid = (bsz, seq // ROW_TILE)
    out = jax.ShapeDtypeStruct((bsz, seq, A_WIDTH), _BF16)
    return pl.pallas_call(
        _even_mixer_kernel,
        grid=grid,
        in_specs=[
            _row_spec(D_MODEL),
            _const_spec((1, D_MODEL)),
            _const_spec((D_MODEL, EVEN_IN)),
            _const_spec((1, A_WIDTH)),
            _const_spec((1, A_WIDTH)),
            _const_spec((A_GROUPS, CHUNK, CHUNK)),
            _const_spec((CHUNK, A_GROUPS)),
            _const_spec((len(POOL_WINDOWS), B_GROUP_DIM, B_GROUP_DIM)),
            _const_spec((1, B_WIDTH)),
        ],
        out_specs=[_row_spec(A_WIDTH), _row_spec(B_WIDTH)],
        out_shape=[out, out],
        scratch_shapes=[
            pltpu.VMEM((POOL_HALO, B_WIDTH), _F32),
            pltpu.VMEM((ROW_TILE + POOL_HALO, B_WIDTH), _F32),
        ],
        compiler_params=pltpu.CompilerParams(
            dimension_semantics=("arbitrary", "arbitrary"),
            vmem_limit_bytes=VMEM_LIMIT_BYTES),
        name="even_mixer",
    )(h, g0, w_in, ln_g, ln_b, w_s, b_s_t, pool_w, pool_scale)


def _odd_proj_kernel(x_ref, g_ref, w_in_ref, cw_ref, q_ref, k_ref, v_ref, yd_ref,
                     carry_ref, ext_ref):
    j = pl.program_id(1)
    hn = _rms(x_ref[...], g_ref[...]).astype(_BF16)
    proj = jnp.dot(hn, w_in_ref[...], preferred_element_type=_F32)
    q_ref[...] = (proj[:, :C_WIDTH] * (C_QK_DIM ** -0.5)).astype(_BF16)
    k_ref[...] = proj[:, C_WIDTH:2 * C_WIDTH].astype(_BF16)
    v_ref[...] = proj[:, 2 * C_WIDTH:3 * C_WIDTH].astype(_BF16)
    o = 3 * C_WIDTH
    bg = proj[:, o:o + D_WIDTH]
    z = proj[:, o + D_WIDTH:o + 2 * D_WIDTH] * proj[:, o + 2 * D_WIDTH:]

    @pl.when(j == 0)
    def _():
        carry_ref[...] = jnp.zeros_like(carry_ref)

    ext_ref[0:CONV_HALO, :] = carry_ref[...]
    ext_ref[CONV_HALO:, :] = z
    carry_ref[...] = z[ROW_TILE - CONV_HALO:, :]
    y = cw_ref[CONV_WIDTH - 1:CONV_WIDTH, :] * z
    for t in range(CONV_WIDTH - 1):
        shift = CONV_WIDTH - 1 - t
        y = y + cw_ref[t:t + 1, :] * ext_ref[CONV_HALO - shift:CONV_HALO - shift + ROW_TILE, :]
    yd_ref[...] = (bg * y).astype(_BF16)


def _odd_proj(h, g0, w_in, conv_w):
    bsz, seq, _ = h.shape
    grid = (bsz, seq // ROW_TILE)
    out = jax.ShapeDtypeStruct((bsz, seq, C_WIDTH), _BF16)
    return pl.pallas_call(
        _odd_proj_kernel,
        grid=grid,
        in_specs=[
            _row_spec(D_MODEL),
            _const_spec((1, D_MODEL)),
            _const_spec((D_MODEL, ODD_IN)),
            _const_spec((CONV_WIDTH, D_WIDTH)),
        ],
        out_specs=[_row_spec(C_WIDTH)] * 4,
        out_shape=[out] * 4,
        scratch_shapes=[
            pltpu.VMEM((CONV_HALO, D_WIDTH), _F32),
            pltpu.VMEM((ROW_TILE + CONV_HALO, D_WIDTH), _F32),
        ],
        compiler_params=pltpu.CompilerParams(
            dimension_semantics=("arbitrary", "arbitrary"),
            vmem_limit_bytes=VMEM_LIMIT_BYTES),
        name="odd_proj",
    )(h, g0, w_in, conv_w)


def _t5_bucket_upper_bounds(max_dist):
    d = np.arange(max_dist, dtype=np.int32)
    max_exact = REL_BUCKETS // 2
    nf = np.maximum(d, 1).astype(np.float32)
    large = max_exact + (np.log(nf / np.float32(max_exact))
                         / np.float32(math.log(REL_MAX_DIST / max_exact))
                         * np.float32(REL_BUCKETS - max_exact)).astype(np.int32)
    large = np.minimum(large, REL_BUCKETS - 1)
    bucket = np.where(d < max_exact, d, large)
    assert np.all(np.diff(bucket) >= 0)
    return bucket, {int(b): int(d[bucket == b].max()) for b in np.unique(bucket)}


def _attn_kernel(bucket_hi, lambda_init,
                 tab_ref, q_ref, k_ref, v_ref, lam_ref, sg_ref, o_ref,
                 qq_ref, bias_ref, m_ref, l_ref, acc_ref):
    t = ATT_TILE
    h = pl.program_id(1)
    i = pl.program_id(2)
    last_bucket = REL_BUCKETS - 1

    @pl.when(i == 0)
    def _():
        r = lax.broadcasted_iota(jnp.int32, (t, t), 0)
        c = lax.broadcasted_iota(jnp.int32, (t, t), 1)
        far = tab_ref[last_bucket, h]
        for delta in range(2):
            d = r - c + delta * t
            val = jnp.zeros((t, t), _F32)
            for b in sorted(bucket_hi, reverse=True):
                if b == last_bucket:
                    continue
                val = jnp.where(d <= bucket_hi[b], tab_ref[b, h] - far, val)
            bias_ref[delta] = jnp.where(d >= 0, val, MASK_VALUE)

    q = q_ref[...]
    lane = lax.broadcasted_iota(jnp.int32, (t, 2 * C_QK_DIM), 1)
    zero = jnp.zeros_like(q)
    qq_ref[0:t, :] = jnp.where(lane < C_QK_DIM, q, zero)
    qq_ref[t:, :] = jnp.where(lane >= C_QK_DIM, q, zero)
    m_ref[...] = jnp.full_like(m_ref, MASK_VALUE)
    l_ref[...] = jnp.zeros_like(l_ref)
    acc_ref[...] = jnp.zeros_like(acc_ref)

    def step(jk, bias):
        start = pl.multiple_of(jk * t, t)
        k_t = k_ref[pl.ds(start, t), :]
        v_t = v_ref[pl.ds(start, t), :]
        s = lax.dot_general(qq_ref[...], k_t, (((1,), (1,)), ((), ())),
                            preferred_element_type=_F32)
        if bias is not None:
            s = s + jnp.concatenate([bias, bias], axis=0)
        m_prev = m_ref[...]
        m_next = jnp.maximum(m_prev, jnp.max(s, axis=-1, keepdims=True))
        alpha = jnp.exp(m_prev - m_next)
        p = jnp.exp(s - m_next)
        l_ref[...] = alpha * l_ref[...] + jnp.sum(p, axis=-1, keepdims=True)
        acc_ref[...] = alpha * acc_ref[...] + jnp.dot(p.astype(_BF16), v_t,
                                                      preferred_element_type=_F32)
        m_ref[...] = m_next

    def far_step(jk, carry):
        step(jk, None)
        return carry

    lax.fori_loop(0, jnp.maximum(i - 1, 0), far_step, 0)

    @pl.when(i >= 1)
    def _():
        step(i - 1, bias_ref[1])

    step(i, bias_ref[0])

    lp = lam_ref[...]
    lam = (jnp.exp(jnp.sum(lp[0:1] * lp[1:2], axis=-1, keepdims=True))
           - jnp.exp(jnp.sum(lp[2:3] * lp[3:4], axis=-1, keepdims=True)) + lambda_init)
    o = acc_ref[...] / l_ref[...]
    a = o[:t] - lam * o[t:]
    o_ref[...] = (_rms(a, sg_ref[...]) * (1.0 - lambda_init)).astype(_BF16)


def _diff_attention(q, k, v, rel_table, lam_params, subln_g, lambda_init):
    bsz, seq, _ = q.shape
    t = ATT_TILE
    bucket, bucket_hi = _t5_bucket_upper_bounds(seq)
    assert np.all(bucket[t + 1:] == REL_BUCKETS - 1)
    bucket_hi = {b: hi for b, hi in bucket_hi.items()}
    grid = (bsz, C_HEADS, seq // t)
    head_dim = 2 * C_QK_DIM
    kernel = functools.partial(_attn_kernel, bucket_hi, lambda_init)
    return pl.pallas_call(
        kernel,
        grid=grid,
        in_specs=[
            pl.BlockSpec(memory_space=pltpu.SMEM),
            pl.BlockSpec((None, t, head_dim), lambda b, h, i: (b, i, h)),
            pl.BlockSpec((None, seq, head_dim), lambda b, h, i: (b, 0, h)),
            pl.BlockSpec((None, seq, C_V_DIM), lambda b, h, i: (b, 0, h)),
            pl.BlockSpec((4, C_QK_DIM), lambda b, h, i: (0, 0)),
            pl.BlockSpec((1, C_V_DIM), lambda b, h, i: (0, 0)),
        ],
        out_specs=pl.BlockSpec((None, t, C_V_DIM), lambda b, h, i: (b, i, h)),
        out_shape=jax.ShapeDtypeStruct((bsz, seq, C_WIDTH), _BF16),
        scratch_shapes=[
            pltpu.VMEM((2 * t, head_dim), _BF16),
            pltpu.VMEM((2, t, t), _F32),
            pltpu.VMEM((2 * t, 1), _F32),
            pltpu.VMEM((2 * t, 1), _F32),
            pltpu.VMEM((2 * t, C_V_DIM), _F32),
        ],
        compiler_params=pltpu.CompilerParams(
            dimension_semantics=("arbitrary", "arbitrary", "arbitrary"),
            vmem_limit_bytes=VMEM_LIMIT_BYTES),
        name="diff_attention",
    )(rel_table, q, k, v, lam_params, subln_g)


def _out_mlp_kernel(h_ref, ya_ref, yb_ref, g_ref, w_out_ref, w1_ref, w2_ref, o_ref):
    half = D_MODEL // 2
    y = (jnp.dot(ya_ref[...], w_out_ref[:half, :], preferred_element_type=_F32)
         + jnp.dot(yb_ref[...], w_out_ref[half:, :], preferred_element_type=_F32))
    h1 = h_ref[...] + _rms(y, g_ref[1:2, :])
    hn = _rms(h1, g_ref[2:3, :]).astype(_BF16)
    acc = jnp.zeros((ROW_TILE, D_MODEL), _F32)
    for c in range(D_FF // FF_CHUNK):
        cs = slice(c * FF_CHUNK, (c + 1) * FF_CHUNK)
        a = jnp.dot(hn, w1_ref[:, cs], preferred_element_type=_F32)
        a = jnp.square(jnp.maximum(a, 0.0)).astype(_BF16)
        acc = acc + jnp.dot(a, w2_ref[cs, :], preferred_element_type=_F32)
    o_ref[...] = h1 + _rms(acc, g_ref[3:4, :])


def _out_mlp(h, ya, yb, g, w_out, w1, w2):
    bsz, seq, _ = h.shape
    grid = (bsz, seq // ROW_TILE)
    return pl.pallas_call(
        _out_mlp_kernel,
        grid=grid,
        in_specs=[
            _row_spec(D_MODEL),
            _row_spec(D_MODEL // 2),
            _row_spec(D_MODEL // 2),
            _const_spec((4, D_MODEL)),
            _const_spec((D_MODEL, D_MODEL)),
            _const_spec((D_MODEL, D_FF)),
            _const_spec((D_FF, D_MODEL)),
        ],
        out_specs=_row_spec(D_MODEL),
        out_shape=jax.ShapeDtypeStruct(h.shape, h.dtype),
        compiler_params=pltpu.CompilerParams(
            dimension_semantics=("arbitrary", "arbitrary"),
            vmem_limit_bytes=VMEM_LIMIT_BYTES),
        name="out_mlp",
    )(h, ya, yb, g, w_out, w1, w2)


def kernel(x, rel_bias_table, norm_g, even_w_in, even_ln_g, even_ln_b, even_spatial_w,
           even_spatial_b, even_pool_w, even_pool_scale, even_w_out, odd_w_in, odd_lambda,
           odd_subln_g, odd_conv_w, odd_w_out, ffn_w1, ffn_w2):
    depth = norm_g.shape[0]
    bf = lambda w: w.astype(_BF16)
    h = x
    for layer in range(depth):
        g = norm_g[layer]
        if layer % 2 == 0:
            e = layer // 2
            ya, yb = _even_mixer(
                h, g[0:1], bf(even_w_in[e]), even_ln_g[e][None], even_ln_b[e][None],
                even_spatial_w[e], even_spatial_b[e].T, bf(even_pool_w[e]),
                even_pool_scale[e][None])
            w_out = even_w_out[e]
        else:
            o = layer // 2
            lambda_init = 0.8 - 0.6 * math.exp(-0.3 * layer)
            q, k, v, yb = _odd_proj(h, g[0:1], bf(odd_w_in[o]), odd_conv_w[o])
            ya = _diff_attention(q, k, v, rel_bias_table, odd_lambda[o],
                                 odd_subln_g[o][None], lambda_init)
            w_out = odd_w_out[o]
        h = _out_mlp(h, ya, yb, g, bf(w_out), bf(ffn_w1[layer]), bf(ffn_w2[layer]))
    return h
```

```python
import functools
import math

import numpy as np
import jax
import jax.numpy as jnp
from jax import lax
from jax.experimental import pallas as pl
from jax.experimental.pallas import tpu as pltpu

D_MODEL = 1024
A_WIDTH = 512
A_GROUPS = 4
A_GROUP_DIM = 128
CHUNK = 128
B_WIDTH = 512
POOL_WINDOWS = (2, 4, 8, 16)
B_GROUP_DIM = 128
C_HEADS = 4
C_QK_DIM = 64
C_V_DIM = 128
C_WIDTH = 512
D_WIDTH = 512
CONV_WIDTH = 3
REL_BUCKETS = 32
REL_MAX_DIST = 128
D_FF = 4096
EPS = 1e-6
EVEN_IN = 2 * A_WIDTH + B_WIDTH
ODD_IN = 2 * C_WIDTH + C_WIDTH + 3 * D_WIDTH

ROW_TILE = 512
ATT_TILE = 256
FF_CHUNK = 1024
POOL_HALO = 16
CONV_HALO = 8
MASK_VALUE = -1e30
VMEM_LIMIT_BYTES = 52 * 1024 * 1024

_F32 = jnp.float32
_BF16 = jnp.bfloat16


def _rms(x, g):
    return x * lax.rsqrt(jnp.mean(x * x, axis=-1, keepdims=True) + EPS) * g


def _const_spec(shape):
    nd = len(shape)
    return pl.BlockSpec(shape, lambda *_: (0,) * nd, pipeline_mode=pl.Buffered(1))


def _row_spec(width, col=0):
    return pl.BlockSpec((None, ROW_TILE, width), lambda b, j: (b, j, col))


def _even_mixer_kernel(x_ref, g_ref, w_in_ref, ln_g_ref, ln_b_ref, ws_ref, bs_ref,
                       pw_ref, ps_ref, ya_ref, yb_ref, carry_ref, ext_ref):
    j = pl.program_id(1)
    hn = _rms(x_ref[...], g_ref[...]).astype(_BF16)
    proj = jnp.dot(hn, w_in_ref[...], preferred_element_type=_F32)

    z = jax.nn.gelu(proj[:, :2 * A_WIDTH])
    u = z[:, :A_WIDTH]
    v = z[:, A_WIDTH:]
    mu = jnp.mean(v, axis=-1, keepdims=True)
    vc = v - mu
    vn = vc * lax.rsqrt(jnp.mean(vc * vc, axis=-1, keepdims=True) + EPS)
    vn = (vn * ln_g_ref[...] + ln_b_ref[...]).astype(_BF16)
    row = lax.broadcasted_iota(jnp.int32, (CHUNK, CHUNK), 0)
    col = lax.broadcasted_iota(jnp.int32, (CHUNK, CHUNK), 1)
    for g in range(A_GROUPS):
        cs = slice(g * A_GROUP_DIM, (g + 1) * A_GROUP_DIM)
        w = jnp.where(row >= col, ws_ref[g], 0.0).astype(_BF16)
        bias = bs_ref[:, g:g + 1]
        for c in range(ROW_TILE // CHUNK):
            rs = slice(c * CHUNK, (c + 1) * CHUNK)
            mixed = jnp.dot(w, vn[rs, cs], preferred_element_type=_F32) + bias
            ya_ref[rs, cs] = (u[rs, cs] * mixed).astype(_BF16)

    p = proj[:, 2 * A_WIDTH:]

    @pl.when(j == 0)
    def _():
        carry_ref[...] = jnp.zeros_like(carry_ref)

    ext_ref[0:POOL_HALO, :] = carry_ref[...]
    ext_ref[POOL_HALO:, :] = p
    carry_ref[...] = p[ROW_TILE - POOL_HALO:, :]
    pos = j * ROW_TILE + lax.broadcasted_iota(jnp.int32, (ROW_TILE, B_GROUP_DIM), 0)
    for g, win in enumerate(POOL_WINDOWS):
        cs = slice(g * B_GROUP_DIM, (g + 1) * B_GROUP_DIM)
        tok = p[:, cs]
        acc = tok
        for k in range(1, win):
            acc = acc + ext_ref[POOL_HALO - k:POOL_HALO - k + ROW_TILE, cs]
        count = jnp.minimum(pos + 1, win).astype(_F32)
        pooled = (acc / count - tok).astype(_BF16)
        y = jnp.dot(pooled, pw_ref[g], preferred_element_type=_F32)
        yb_ref[:, cs] = (y * ps_ref[:, cs]).astype(_BF16)


def _even_mixer(h, g0, w_in, ln_g, ln_b, w_s, b_s_t, pool_w, pool_scale):
    bsz, seq, _ = h.shape
    grid = (bsz, seq // ROW_TILE)
    out = jax.ShapeDtypeStruct((bsz, seq, A_WIDTH), _BF16)
    return pl.pallas_call(
        _even_mixer_kernel,
        grid=grid,
        in_specs=[
            _row_spec(D_MODEL),
            _const_spec((1, D_MODEL)),
            _const_spec((D_MODEL, EVEN_IN)),
            _const_spec((1, A_WIDTH)),
            _const_spec((1, A_WIDTH)),
            _const_spec((A_GROUPS, CHUNK, CHUNK)),
            _const_spec((CHUNK, A_GROUPS)),
            _const_spec((len(POOL_WINDOWS), B_GROUP_DIM, B_GROUP_DIM)),
            _const_spec((1, B_WIDTH)),
        ],
        out_specs=[_row_spec(A_WIDTH), _row_spec(B_WIDTH)],
        out_shape=[out, out],
        scratch_shapes=[
            pltpu.VMEM((POOL_HALO, B_WIDTH), _F32),
            pltpu.VMEM((ROW_TILE + POOL_HALO, B_WIDTH), _F32),
        ],
        compiler_params=pltpu.CompilerParams(
            dimension_semantics=("arbitrary", "arbitrary"),
            vmem_limit_bytes=VMEM_LIMIT_BYTES),
        name="even_mixer",
    )(h, g0, w_in, ln_g, ln_b, w_s, b_s_t, pool_w, pool_scale)


def _odd_proj_kernel(x_ref, g_ref, w_in_ref, cw_ref, q_ref, k_ref, vt_ref, yd_ref,
                     carry_ref, ext_ref):
    j = pl.program_id(1)
    hn = _rms(x_ref[...], g_ref[...]).astype(_BF16)
    proj = jnp.dot(hn, w_in_ref[...], preferred_element_type=_F32)
    q_ref[...] = (proj[:, :C_WIDTH] * (C_QK_DIM ** -0.5)).astype(_BF16)
    k_ref[...] = proj[:, C_WIDTH:2 * C_WIDTH].astype(_BF16)
    vt_ref[...] = proj[:, 2 * C_WIDTH:3 * C_WIDTH].T.astype(_BF16)
    o = 3 * C_WIDTH
    bg = proj[:, o:o + D_WIDTH]
    z = proj[:, o + D_WIDTH:o + 2 * D_WIDTH] * proj[:, o + 2 * D_WIDTH:]

    @pl.when(j == 0)
    def _():
        carry_ref[...] = jnp.zeros_like(carry_ref)

    ext_ref[0:CONV_HALO, :] = carry_ref[...]
    ext_ref[CONV_HALO:, :] = z
    carry_ref[...] = z[ROW_TILE - CONV_HALO:, :]
    y = cw_ref[CONV_WIDTH - 1:CONV_WIDTH, :] * z
    for t in range(CONV_WIDTH - 1):
        shift = CONV_WIDTH - 1 - t
        y = y + cw_ref[t:t + 1, :] * ext_ref[CONV_HALO - shift:CONV_HALO - shift + ROW_TILE, :]
    yd_ref[...] = (bg * y).astype(_BF16)


def _odd_proj(h, g0, w_in, conv_w):
    bsz, seq, _ = h.shape
    grid = (bsz, seq // ROW_TILE)
    out = jax.ShapeDtypeStruct((bsz, seq, C_WIDTH), _BF16)
    out_t = jax.ShapeDtypeStruct((bsz, C_WIDTH, seq), _BF16)
    spec_t = pl.BlockSpec((None, C_WIDTH, ROW_TILE), lambda b, j: (b, 0, j))
    return pl.pallas_call(
        _odd_proj_kernel,
        grid=grid,
        in_specs=[
            _row_spec(D_MODEL),
            _const_spec((1, D_MODEL)),
            _const_spec((D_MODEL, ODD_IN)),
            _const_spec((CONV_WIDTH, D_WIDTH)),
        ],
        out_specs=[_row_spec(C_WIDTH), _row_spec(C_WIDTH), spec_t, _row_spec(D_WIDTH)],
        out_shape=[out, out, out_t, out],
        scratch_shapes=[
            pltpu.VMEM((CONV_HALO, D_WIDTH), _F32),
            pltpu.VMEM((ROW_TILE + CONV_HALO, D_WIDTH), _F32),
        ],
        compiler_params=pltpu.CompilerParams(
            dimension_semantics=("arbitrary", "arbitrary"),
            vmem_limit_bytes=VMEM_LIMIT_BYTES),
        name="odd_proj",
    )(h, g0, w_in, conv_w)


def _t5_bucket_upper_bounds(max_dist):
    d = np.arange(max_dist, dtype=np.int32)
    max_exact = REL_BUCKETS // 2
    nf = np.maximum(d, 1).astype(np.float32)
    large = max_exact + (np.log(nf / np.float32(max_exact))
                         / np.float32(math.log(REL_MAX_DIST / max_exact))
                         * np.float32(REL_BUCKETS - max_exact)).astype(np.int32)
    large = np.minimum(large, REL_BUCKETS - 1)
    bucket = np.where(d < max_exact, d, large)
    assert np.all(np.diff(bucket) >= 0)
    return bucket, {int(b): int(d[bucket == b].max()) for b in np.unique(bucket)}


def _attn_kernel(bucket_hi, lambda_init,
                 tab_ref, q_ref, k_ref, vt_ref, lam_ref, sg_ref, o_ref,
                 qq_ref, bias_ref, acc_ref):
    t = ATT_TILE
    h = pl.program_id(1)
    i = pl.program_id(2)
    last_bucket = REL_BUCKETS - 1

    @pl.when(i == 0)
    def _():
        kpos = lax.broadcasted_iota(jnp.int32, (t, t), 0)
        qpos = lax.broadcasted_iota(jnp.int32, (t, t), 1)
        far = tab_ref[last_bucket, h]
        for delta in range(2):
            d = qpos - kpos + delta * t
            val = jnp.zeros((t, t), _F32)
            for b in sorted(bucket_hi, reverse=True):
                if b == last_bucket:
                    continue
                val = jnp.where(d <= bucket_hi[b], tab_ref[b, h] - far, val)
            bias_ref[delta] = jnp.where(d >= 0, val, MASK_VALUE)

    q = q_ref[...]
    lane = lax.broadcasted_iota(jnp.int32, (t, 2 * C_QK_DIM), 1)
    zero = jnp.zeros_like(q)
    qq_ref[0:t, :] = jnp.where(lane < C_QK_DIM, q, zero)
    qq_ref[t:, :] = jnp.where(lane >= C_QK_DIM, q, zero)
    acc_ref[...] = jnp.zeros_like(acc_ref)

    def step(jk, m_prev, l_prev, bias):
        start = pl.multiple_of(jk * t, t)
        k_t = k_ref[pl.ds(start, t), :]
        vt = vt_ref[:, pl.ds(start, t)]
        s = lax.dot_general(k_t, qq_ref[...], (((1,), (1,)), ((), ())),
                            preferred_element_type=_F32)
        if bias is not None:
            s = s + jnp.concatenate([bias, bias], axis=1)
        m_next = jnp.maximum(m_prev, jnp.max(s, axis=0, keepdims=True))
        alpha = jnp.exp(m_prev - m_next)
        p = jnp.exp(s - m_next)
        l_next = alpha * l_prev + jnp.sum(p, axis=0, keepdims=True)
        acc_ref[...] = alpha * acc_ref[...] + jnp.dot(vt, p.astype(_BF16),
                                                      preferred_element_type=_F32)
        return m_next, l_next

    m0 = jnp.full((1, 2 * t), MASK_VALUE, _F32)
    l0 = jnp.zeros((1, 2 * t), _F32)
    m1, l1 = lax.fori_loop(0, jnp.maximum(i - 1, 0),
                           lambda jk, c: step(jk, c[0], c[1], None), (m0, l0))
    m2, l2 = lax.cond(i >= 1,
                      lambda: step(i - 1, m1, l1, bias_ref[1]),
                      lambda: (m1, l1))
    m3, l3 = step(i, m2, l2, bias_ref[0])

    lp = lam_ref[...]
    lam = (jnp.exp(jnp.sum(lp[0:1] * lp[1:2], axis=-1, keepdims=True))
           - jnp.exp(jnp.sum(lp[2:3] * lp[3:4], axis=-1, keepdims=True)) + lambda_init)
    o = acc_ref[...] / l3
    a = o[:, :t] - lam * o[:, t:]
    ms = jnp.mean(a * a, axis=0, keepdims=True)
    y = a * lax.rsqrt(ms + EPS) * sg_ref[...] * (1.0 - lambda_init)
    o_ref[...] = y.T.astype(_BF16)


def _diff_attention(q, k, vt, rel_table, lam_params, subln_g, lambda_init):
    bsz, seq, _ = q.shape
    t = ATT_TILE
    bucket, bucket_hi = _t5_bucket_upper_bounds(seq)
    assert np.all(bucket[t + 1:] == REL_BUCKETS - 1)
    grid = (bsz, C_HEADS, seq // t)
    head_dim = 2 * C_QK_DIM
    kernel = functools.partial(_attn_kernel, bucket_hi, lambda_init)
    return pl.pallas_call(
        kernel,
        grid=grid,
        in_specs=[
            pl.BlockSpec(memory_space=pltpu.SMEM),
            pl.BlockSpec((None, t, head_dim), lambda b, h, i: (b, i, h)),
            pl.BlockSpec((None, seq, head_dim), lambda b, h, i: (b, 0, h)),
            pl.BlockSpec((None, C_V_DIM, seq), lambda b, h, i: (b, h, 0)),
            pl.BlockSpec((4, C_QK_DIM), lambda b, h, i: (0, 0)),
            pl.BlockSpec((C_V_DIM, 1), lambda b, h, i: (0, 0)),
        ],
        out_specs=pl.BlockSpec((None, t, C_V_DIM), lambda b, h, i: (b, i, h)),
        out_shape=jax.ShapeDtypeStruct((bsz, seq, C_WIDTH), _BF16),
        scratch_shapes=[
            pltpu.VMEM((2 * t, head_dim), _BF16),
            pltpu.VMEM((2, t, t), _F32),
            pltpu.VMEM((C_V_DIM, 2 * t), _F32),
        ],
        compiler_params=pltpu.CompilerParams(
            dimension_semantics=("arbitrary", "arbitrary", "arbitrary"),
            vmem_limit_bytes=VMEM_LIMIT_BYTES),
        name="diff_attention",
    )(rel_table, q, k, vt, lam_params, subln_g)


def _out_mlp_kernel(h_ref, ya_ref, yb_ref, g_ref, w_out_ref, w1_ref, w2_ref, o_ref):
    half = D_MODEL // 2
    y = (jnp.dot(ya_ref[...], w_out_ref[:half, :], preferred_element_type=_F32)
         + jnp.dot(yb_ref[...], w_out_ref[half:, :], preferred_element_type=_F32))
    h1 = h_ref[...] + _rms(y, g_ref[1:2, :])
    hn = _rms(h1, g_ref[2:3, :]).astype(_BF16)
    acc = jnp.zeros((ROW_TILE, D_MODEL), _F32)
    for c in range(D_FF // FF_CHUNK):
        cs = slice(c * FF_CHUNK, (c + 1) * FF_CHUNK)
        a = jnp.dot(hn, w1_ref[:, cs], preferred_element_type=_F32)
        a = jnp.square(jnp.maximum(a, 0.0)).astype(_BF16)
        acc = acc + jnp.dot(a, w2_ref[cs, :], preferred_element_type=_F32)
    o_ref[...] = h1 + _rms(acc, g_ref[3:4, :])


def _out_mlp(h, ya, yb, g, w_out, w1, w2):
    bsz, seq, _ = h.shape
    grid = (bsz, seq // ROW_TILE)
    return pl.pallas_call(
        _out_mlp_kernel,
        grid=grid,
        in_specs=[
            _row_spec(D_MODEL),
            _row_spec(D_MODEL // 2),
            _row_spec(D_MODEL // 2),
            _const_spec((4, D_MODEL)),
            _const_spec((D_MODEL, D_MODEL)),
            _const_spec((D_MODEL, D_FF)),
            _const_spec((D_FF, D_MODEL)),
        ],
        out_specs=_row_spec(D_MODEL),
        out_shape=jax.ShapeDtypeStruct(h.shape, h.dtype),
        compiler_params=pltpu.CompilerParams(
            dimension_semantics=("arbitrary", "arbitrary"),
            vmem_limit_bytes=VMEM_LIMIT_BYTES),
        name="out_mlp",
    )(h, ya, yb, g, w_out, w1, w2)


def kernel(x, rel_bias_table, norm_g, even_w_in, even_ln_g, even_ln_b, even_spatial_w,
           even_spatial_b, even_pool_w, even_pool_scale, even_w_out, odd_w_in, odd_lambda,
           odd_subln_g, odd_conv_w, odd_w_out, ffn_w1, ffn_w2):
    depth = norm_g.shape[0]
    bf = lambda w: w.astype(_BF16)
    h = x
    for layer in range(depth):
        g = norm_g[layer]
        if layer % 2 == 0:
            e = layer // 2
            ya, yb = _even_mixer(
                h, g[0:1], bf(even_w_in[e]), even_ln_g[e][None], even_ln_b[e][None],
                even_spatial_w[e], even_spatial_b[e].T, bf(even_pool_w[e]),
                even_pool_scale[e][None])
            w_out = even_w_out[e]
        else:
            o = layer // 2
            lambda_init = 0.8 - 0.6 * math.exp(-0.3 * layer)
            q, k, vt, yb = _odd_proj(h, g[0:1], bf(odd_w_in[o]), odd_conv_w[o])
            ya = _diff_attention(q, k, vt, rel_bias_table, odd_lambda[o],
                                 odd_subln_g[o][:, None], lambda_init)
            w_out = odd_w_out[o]
        h = _out_mlp(h, ya, yb, g, bf(w_out), bf(ffn_w1[layer]), bf(ffn_w2[layer]))
    return h
```

```python
import functools
import math

import numpy as np
import jax
import jax.numpy as jnp
from jax import lax
from jax.experimental import pallas as pl
from jax.experimental.pallas import tpu as pltpu

D_MODEL = 1024
A_WIDTH = 512
A_GROUPS = 4
A_GROUP_DIM = 128
CHUNK = 128
B_WIDTH = 512
POOL_WINDOWS = (2, 4, 8, 16)
B_GROUP_DIM = 128
C_HEADS = 4
C_QK_DIM = 64
C_V_DIM = 128
C_WIDTH = 512
D_WIDTH = 512
CONV_WIDTH = 3
REL_BUCKETS = 32
REL_MAX_DIST = 128
D_FF = 4096
EPS = 1e-6
EVEN_IN = 2 * A_WIDTH + B_WIDTH
ODD_IN = 2 * C_WIDTH + C_WIDTH + 3 * D_WIDTH

ROW_TILE = 512
ATT_TILE = 256
FF_CHUNK = 1024
POOL_HALO = 16
CONV_HALO = 8
LOG2E = math.log2(math.e)
MASK_VALUE = -1e30
VMEM_LIMIT_BYTES = 52 * 1024 * 1024

_F32 = jnp.float32
_BF16 = jnp.bfloat16


def _rms(x, g):
    return x * lax.rsqrt(jnp.mean(x * x, axis=-1, keepdims=True) + EPS) * g


def _const_spec(shape):
    nd = len(shape)
    return pl.BlockSpec(shape, lambda *_: (0,) * nd, pipeline_mode=pl.Buffered(1))


def _row_spec(width, col=0):
    return pl.BlockSpec((None, ROW_TILE, width), lambda b, j: (b, j, col))


def _even_mixer_kernel(x_ref, g_ref, w_in_ref, ln_g_ref, ln_b_ref, ws_ref, bs_ref,
                       pw_ref, ps_ref, ya_ref, yb_ref, carry_ref, ext_ref):
    j = pl.program_id(1)
    hn = _rms(x_ref[...], g_ref[...]).astype(_BF16)
    proj = jnp.dot(hn, w_in_ref[...], preferred_element_type=_F32)

    z = jax.nn.gelu(proj[:, :2 * A_WIDTH])
    u = z[:, :A_WIDTH]
    v = z[:, A_WIDTH:]
    mu = jnp.mean(v, axis=-1, keepdims=True)
    vc = v - mu
    vn = vc * lax.rsqrt(jnp.mean(vc * vc, axis=-1, keepdims=True) + EPS)
    vn = (vn * ln_g_ref[...] + ln_b_ref[...]).astype(_BF16)
    row = lax.broadcasted_iota(jnp.int32, (CHUNK, CHUNK), 0)
    col = lax.broadcasted_iota(jnp.int32, (CHUNK, CHUNK), 1)
    for g in range(A_GROUPS):
        cs = slice(g * A_GROUP_DIM, (g + 1) * A_GROUP_DIM)
        w = jnp.where(row >= col, ws_ref[g], 0.0).astype(_BF16)
        bias = bs_ref[:, g:g + 1]
        for c in range(ROW_TILE // CHUNK):
            rs = slice(c * CHUNK, (c + 1) * CHUNK)
            mixed = jnp.dot(w, vn[rs, cs], preferred_element_type=_F32) + bias
            ya_ref[rs, cs] = (u[rs, cs] * mixed).astype(_BF16)

    p = proj[:, 2 * A_WIDTH:]

    @pl.when(j == 0)
    def _():
        carry_ref[...] = jnp.zeros_like(carry_ref)

    ext_ref[0:POOL_HALO, :] = carry_ref[...]
    ext_ref[POOL_HALO:, :] = p
    carry_ref[...] = p[ROW_TILE - POOL_HALO:, :]
    pos = j * ROW_TILE + lax.broadcasted_iota(jnp.int32, (ROW_TILE, B_GROUP_DIM), 0)
    for g, win in enumerate(POOL_WINDOWS):
        cs = slice(g * B_GROUP_DIM, (g + 1) * B_GROUP_DIM)
        tok = p[:, cs]
        acc = tok
        for k in range(1, win):
            acc = acc + ext_ref[POOL_HALO - k:POOL_HALO - k + ROW_TILE, cs]
        count = jnp.minimum(pos + 1, win).astype(_F32)
        pooled = (acc / count - tok).astype(_BF16)
        y = jnp.dot(pooled, pw_ref[g], preferred_element_type=_F32)
        yb_ref[:, cs] = (y * ps_ref[:, cs]).astype(_BF16)


def _even_mixer(h, g0, w_in, ln_g, ln_b, w_s, b_s_t, pool_w, pool_scale):
    bsz, seq, _ = h.shape
    grid = (bsz, seq // ROW_TILE)
    out = jax.ShapeDtypeStruct((bsz, seq, A_WIDTH), _BF16)
    return pl.pallas_call(
        _even_mixer_kernel,
        grid=grid,
        in_specs=[
            _row_spec(D_MODEL),
            _const_spec((1, D_MODEL)),
            _const_spec((D_MODEL, EVEN_IN)),
            _const_spec((1, A_WIDTH)),
            _const_spec((1, A_WIDTH)),
            _const_spec((A_GROUPS, CHUNK, CHUNK)),
            _const_spec((CHUNK, A_GROUPS)),
            _const_spec((len(POOL_WINDOWS), B_GROUP_DIM, B_GROUP_DIM)),
            _const_spec((1, B_WIDTH)),
        ],
        out_specs=[_row_spec(A_WIDTH), _row_spec(B_WIDTH)],
        out_shape=[out, out],
        scratch_shapes=[
            pltpu.VMEM((POOL_HALO, B_WIDTH), _F32),
            pltpu.VMEM((ROW_TILE + POOL_HALO, B_WIDTH), _F32),
        ],
        compiler_params=pltpu.CompilerParams(
            dimension_semantics=("arbitrary", "arbitrary"),
            vmem_limit_bytes=VMEM_LIMIT_BYTES),
        name="even_mixer",
    )(h, g0, w_in, ln_g, ln_b, w_s, b_s_t, pool_w, pool_scale)


def _odd_proj_kernel(x_ref, g_ref, w_in_ref, cw_ref, q_ref, k_ref, vt_ref, yd_ref,
                     carry_ref, ext_ref):
    j = pl.program_id(1)
    hn = _rms(x_ref[...], g_ref[...]).astype(_BF16)
    proj = jnp.dot(hn, w_in_ref[...], preferred_element_type=_F32)
    q_ref[...] = (proj[:, :C_WIDTH] * (C_QK_DIM ** -0.5 * LOG2E)).astype(_BF16)
    k_ref[...] = proj[:, C_WIDTH:2 * C_WIDTH].astype(_BF16)
    vt_ref[...] = proj[:, 2 * C_WIDTH:3 * C_WIDTH].T.astype(_BF16)
    o = 3 * C_WIDTH
    bg = proj[:, o:o + D_WIDTH]
    z = proj[:, o + D_WIDTH:o + 2 * D_WIDTH] * proj[:, o + 2 * D_WIDTH:]

    @pl.when(j == 0)
    def _():
        carry_ref[...] = jnp.zeros_like(carry_ref)

    ext_ref[0:CONV_HALO, :] = carry_ref[...]
    ext_ref[CONV_HALO:, :] = z
    carry_ref[...] = z[ROW_TILE - CONV_HALO:, :]
    y = cw_ref[CONV_WIDTH - 1:CONV_WIDTH, :] * z
    for t in range(CONV_WIDTH - 1):
        shift = CONV_WIDTH - 1 - t
        y = y + cw_ref[t:t + 1, :] * ext_ref[CONV_HALO - shift:CONV_HALO - shift + ROW_TILE, :]
    yd_ref[...] = (bg * y).astype(_BF16)


def _odd_proj(h, g0, w_in, conv_w):
    bsz, seq, _ = h.shape
    grid = (bsz, seq // ROW_TILE)
    out = jax.ShapeDtypeStruct((bsz, seq, C_WIDTH), _BF16)
    out_t = jax.ShapeDtypeStruct((bsz, C_WIDTH, seq), _BF16)
    spec_t = pl.BlockSpec((None, C_WIDTH, ROW_TILE), lambda b, j: (b, 0, j))
    return pl.pallas_call(
        _odd_proj_kernel,
        grid=grid,
        in_specs=[
            _row_spec(D_MODEL),
            _const_spec((1, D_MODEL)),
            _const_spec((D_MODEL, ODD_IN)),
            _const_spec((CONV_WIDTH, D_WIDTH)),
        ],
        out_specs=[_row_spec(C_WIDTH), _row_spec(C_WIDTH), spec_t, _row_spec(D_WIDTH)],
        out_shape=[out, out, out_t, out],
        scratch_shapes=[
            pltpu.VMEM((CONV_HALO, D_WIDTH), _F32),
            pltpu.VMEM((ROW_TILE + CONV_HALO, D_WIDTH), _F32),
        ],
        compiler_params=pltpu.CompilerParams(
            dimension_semantics=("arbitrary", "arbitrary"),
            vmem_limit_bytes=VMEM_LIMIT_BYTES),
        name="odd_proj",
    )(h, g0, w_in, conv_w)


def _t5_bucket_upper_bounds(max_dist):
    d = np.arange(max_dist, dtype=np.int32)
    max_exact = REL_BUCKETS // 2
    nf = np.maximum(d, 1).astype(np.float32)
    large = max_exact + (np.log(nf / np.float32(max_exact))
                         / np.float32(math.log(REL_MAX_DIST / max_exact))
                         * np.float32(REL_BUCKETS - max_exact)).astype(np.int32)
    large = np.minimum(large, REL_BUCKETS - 1)
    bucket = np.where(d < max_exact, d, large)
    assert np.all(np.diff(bucket) >= 0)
    return bucket, {int(b): int(d[bucket == b].max()) for b in np.unique(bucket)}


def _attn_kernel(bucket_hi, lambda_init,
                 tab_ref, q_ref, k_ref, vt_ref, lam_ref, sg_ref, o_ref,
                 qq_ref, bias_ref, m_ref, l_ref, acc_ref, s_ref, p_ref, al_ref):
    t = ATT_TILE
    hd_dim = 2 * C_QK_DIM
    i = pl.program_id(1)
    last_bucket = REL_BUCKETS - 1
    heads = range(C_HEADS)

    @pl.when(i == 0)
    def _():
        kpos = lax.broadcasted_iota(jnp.int32, (t, t), 0)
        qpos = lax.broadcasted_iota(jnp.int32, (t, t), 1)
        for delta in range(2):
            d = qpos - kpos + delta * t
            for hd in heads:
                far = tab_ref[last_bucket, hd]
                val = jnp.zeros((t, t), _F32)
                for b in sorted(bucket_hi, reverse=True):
                    if b == last_bucket:
                        continue
                    val = jnp.where(d <= bucket_hi[b], (tab_ref[b, hd] - far) * LOG2E, val)
                val = jnp.where(d >= 0, val, MASK_VALUE)
                bias_ref[delta, hd] = jnp.concatenate([val, val], axis=1)

    lane = lax.broadcasted_iota(jnp.int32, (t, hd_dim), 1)
    for hd in heads:
        q = q_ref[:, hd * hd_dim:(hd + 1) * hd_dim]
        zero = jnp.zeros_like(q)
        qq_ref[hd, 0:t, :] = jnp.where(lane < C_QK_DIM, q, zero)
        qq_ref[hd, t:, :] = jnp.where(lane >= C_QK_DIM, q, zero)
    m_ref[...] = jnp.full_like(m_ref, MASK_VALUE)
    l_ref[...] = jnp.zeros_like(l_ref)
    acc_ref[...] = jnp.zeros_like(acc_ref)
    p_ref[C_HEADS - 1] = jnp.zeros((t, 2 * t), _BF16)
    al_ref[C_HEADS - 1] = jnp.ones((1, 2 * t), _F32)

    def scores(jk, hd):
        start = pl.multiple_of(jk * t, t)
        k_t = k_ref[pl.ds(start, t), hd * hd_dim:(hd + 1) * hd_dim]
        s_ref[hd] = lax.dot_general(k_t, qq_ref[hd], (((1,), (1,)), ((), ())),
                                    preferred_element_type=_F32)

    def values(jk, hd):
        start = pl.multiple_of(jk * t, t)
        vt = vt_ref[hd * C_V_DIM:(hd + 1) * C_V_DIM, pl.ds(start, t)]
        acc_ref[hd] = al_ref[hd] * acc_ref[hd] + jnp.dot(
            vt, p_ref[hd], preferred_element_type=_F32)

    def step(jk, delta, jk_next):
        for hd in heads:
            if hd + 1 < C_HEADS:
                scores(jk, hd + 1)
            elif jk_next is not None:
                scores(jk_next, 0)
            s = s_ref[hd]
            if delta is not None:
                s = s + bias_ref[delta, hd]
            m_prev = m_ref[hd]
            m_next = jnp.maximum(m_prev, jnp.max(s, axis=0, keepdims=True))
            alpha = jnp.exp2(m_prev - m_next)
            p = jnp.exp2(s - m_next)
            l_ref[hd] = alpha * l_ref[hd] + jnp.sum(p, axis=0, keepdims=True)
            m_ref[hd] = m_next
            if hd == 0:
                values(jnp.maximum(jk - 1, 0), C_HEADS - 1)
            else:
                values(jk, hd - 1)
            p_ref[hd] = p.astype(_BF16)
            al_ref[hd] = alpha

    scores(0, 0)

    def far_step(jk, carry):
        step(jk, None, jk + 1)
        return carry

    lax.fori_loop(0, jnp.maximum(i - 1, 0), far_step, 0)

    @pl.when(i >= 1)
    def _():
        step(i - 1, 1, i)

    step(i, 0, None)
    values(i, C_HEADS - 1)

    lp = lam_ref[...]
    lam = (jnp.exp(jnp.sum(lp[0:1] * lp[1:2], axis=-1, keepdims=True))
           - jnp.exp(jnp.sum(lp[2:3] * lp[3:4], axis=-1, keepdims=True)) + lambda_init)
    for hd in heads:
        o = acc_ref[hd] / l_ref[hd]
        a = o[:, :t] - lam * o[:, t:]
        ms = jnp.mean(a * a, axis=0, keepdims=True)
        y = a * lax.rsqrt(ms + EPS) * sg_ref[...] * (1.0 - lambda_init)
        o_ref[:, hd * C_V_DIM:(hd + 1) * C_V_DIM] = y.T.astype(_BF16)


def _diff_attention(q, k, vt, rel_table, lam_params, subln_g, lambda_init):
    bsz, seq, _ = q.shape
    t = ATT_TILE
    bucket, bucket_hi = _t5_bucket_upper_bounds(seq)
    assert np.all(bucket[t + 1:] == REL_BUCKETS - 1)
    grid = (bsz, seq // t)
    kernel = functools.partial(_attn_kernel, bucket_hi, lambda_init)
    return pl.pallas_call(
        kernel,
        grid=grid,
        in_specs=[
            pl.BlockSpec(memory_space=pltpu.SMEM),
            pl.BlockSpec((None, t, C_WIDTH), lambda b, i: (b, i, 0)),
            pl.BlockSpec((None, seq, C_WIDTH), lambda b, i: (b, 0, 0)),
            pl.BlockSpec((None, C_WIDTH, seq), lambda b, i: (b, 0, 0)),
            pl.BlockSpec((4, C_QK_DIM), lambda b, i: (0, 0)),
            pl.BlockSpec((C_V_DIM, 1), lambda b, i: (0, 0)),
        ],
        out_specs=pl.BlockSpec((None, t, C_WIDTH), lambda b, i: (b, i, 0)),
        out_shape=jax.ShapeDtypeStruct((bsz, seq, C_WIDTH), _BF16),
        scratch_shapes=[
            pltpu.VMEM((C_HEADS, 2 * t, 2 * C_QK_DIM), _BF16),
            pltpu.VMEM((2, C_HEADS, t, 2 * t), _F32),
            pltpu.VMEM((C_HEADS, 1, 2 * t), _F32),
            pltpu.VMEM((C_HEADS, 1, 2 * t), _F32),
            pltpu.VMEM((C_HEADS, C_V_DIM, 2 * t), _F32),
            pltpu.VMEM((C_HEADS, t, 2 * t), _F32),
            pltpu.VMEM((C_HEADS, t, 2 * t), _BF16),
            pltpu.VMEM((C_HEADS, 1, 2 * t), _F32),
        ],
        compiler_params=pltpu.CompilerParams(
            dimension_semantics=("arbitrary", "arbitrary"),
            vmem_limit_bytes=VMEM_LIMIT_BYTES),
        name="diff_attention",
    )(rel_table, q, k, vt, lam_params, subln_g)


def _out_mlp_kernel(h_ref, ya_ref, yb_ref, g_ref, w_out_ref, w1_ref, w2_ref, o_ref):
    half = D_MODEL // 2
    y = (jnp.dot(ya_ref[...], w_out_ref[:half, :], preferred_element_type=_F32)
         + jnp.dot(yb_ref[...], w_out_ref[half:, :], preferred_element_type=_F32))
    h1 = h_ref[...] + _rms(y, g_ref[1:2, :])
    hn = _rms(h1, g_ref[2:3, :]).astype(_BF16)
    acc = jnp.zeros((ROW_TILE, D_MODEL), _F32)
    for c in range(D_FF // FF_CHUNK):
        cs = slice(c * FF_CHUNK, (c + 1) * FF_CHUNK)
        a = jnp.dot(hn, w1_ref[:, cs], preferred_element_type=_F32)
        a = jnp.square(jnp.maximum(a, 0.0)).astype(_BF16)
        acc = acc + jnp.dot(a, w2_ref[cs, :], preferred_element_type=_F32)
    o_ref[...] = h1 + _rms(acc, g_ref[3:4, :])


def _out_mlp(h, ya, yb, g, w_out, w1, w2):
    bsz, seq, _ = h.shape
    grid = (bsz, seq // ROW_TILE)
    return pl.pallas_call(
        _out_mlp_kernel,
        grid=grid,
        in_specs=[
            _row_spec(D_MODEL),
            _row_spec(D_MODEL // 2),
            _row_spec(D_MODEL // 2),
            _const_spec((4, D_MODEL)),
            _const_spec((D_MODEL, D_MODEL)),
            _const_spec((D_MODEL, D_FF)),
            _const_spec((D_FF, D_MODEL)),
        ],
        out_specs=_row_spec(D_MODEL),
        out_shape=jax.ShapeDtypeStruct(h.shape, h.dtype),
        compiler_params=pltpu.CompilerParams(
            dimension_semantics=("arbitrary", "arbitrary"),
            vmem_limit_bytes=VMEM_LIMIT_BYTES),
        name="out_mlp",
    )(h, ya, yb, g, w_out, w1, w2)


def kernel(x, rel_bias_table, norm_g, even_w_in, even_ln_g, even_ln_b, even_spatial_w,
           even_spatial_b, even_pool_w, even_pool_scale, even_w_out, odd_w_in, odd_lambda,
           odd_subln_g, odd_conv_w, odd_w_out, ffn_w1, ffn_w2):
    depth = norm_g.shape[0]
    bf = lambda w: w.astype(_BF16)
    h = x
    for layer in range(depth):
        g = norm_g[layer]
        if layer % 2 == 0:
            e = layer // 2
            ya, yb = _even_mixer(
                h, g[0:1], bf(even_w_in[e]), even_ln_g[e][None], even_ln_b[e][None],
                even_spatial_w[e], even_spatial_b[e].T, bf(even_pool_w[e]),
                even_pool_scale[e][None])
            w_out = even_w_out[e]
        else:
            o = layer // 2
            lambda_init = 0.8 - 0.6 * math.exp(-0.3 * layer)
            q, k, vt, yb = _odd_proj(h, g[0:1], bf(odd_w_in[o]), odd_conv_w[o])
            ya = _diff_attention(q, k, vt, rel_bias_table, odd_lambda[o],
                                 odd_subln_g[o][:, None], lambda_init)
            w_out = odd_w_out[o]
        h = _out_mlp(h, ya, yb, g, bf(w_out), bf(ffn_w1[layer]), bf(ffn_w2[layer]))
    return h
```

```python
import functools
import math

import numpy as np
import jax
import jax.numpy as jnp
from jax import lax
from jax.experimental import pallas as pl
from jax.experimental.pallas import tpu as pltpu

D_MODEL = 1024
A_WIDTH = 512
A_GROUPS = 4
A_GROUP_DIM = 128
CHUNK = 128
B_WIDTH = 512
POOL_WINDOWS = (2, 4, 8, 16)
B_GROUP_DIM = 128
C_HEADS = 4
C_QK_DIM = 64
C_V_DIM = 128
C_WIDTH = 512
D_WIDTH = 512
CONV_WIDTH = 3
REL_BUCKETS = 32
REL_MAX_DIST = 128
D_FF = 4096
EPS = 1e-6
EVEN_IN = 2 * A_WIDTH + B_WIDTH
ODD_IN = 2 * C_WIDTH + C_WIDTH + 3 * D_WIDTH

ROW_TILE = 512
ATT_TILE = 256
FF_CHUNK = 1024
POOL_HALO = 16
CONV_HALO = 8
LOG2E = math.log2(math.e)
SCORE_LOOKAHEAD = 1
SUM_ROWS = 16
MASK_VALUE = -1e30
VMEM_LIMIT_BYTES = 52 * 1024 * 1024

_F32 = jnp.float32
_BF16 = jnp.bfloat16


def _rms(x, g):
    return x * lax.rsqrt(jnp.mean(x * x, axis=-1, keepdims=True) + EPS) * g


def _const_spec(shape):
    nd = len(shape)
    return pl.BlockSpec(shape, lambda *_: (0,) * nd, pipeline_mode=pl.Buffered(1))


def _row_spec(width, col=0):
    return pl.BlockSpec((None, ROW_TILE, width), lambda b, j: (b, j, col))


def _even_mixer_kernel(x_ref, g_ref, w_in_ref, ln_g_ref, ln_b_ref, ws_ref, bs_ref,
                       pw_ref, ps_ref, ya_ref, yb_ref, carry_ref, ext_ref):
    j = pl.program_id(1)
    hn = _rms(x_ref[...], g_ref[...]).astype(_BF16)
    proj = jnp.dot(hn, w_in_ref[...], preferred_element_type=_F32)

    z = jax.nn.gelu(proj[:, :2 * A_WIDTH])
    u = z[:, :A_WIDTH]
    v = z[:, A_WIDTH:]
    mu = jnp.mean(v, axis=-1, keepdims=True)
    vc = v - mu
    vn = vc * lax.rsqrt(jnp.mean(vc * vc, axis=-1, keepdims=True) + EPS)
    vn = (vn * ln_g_ref[...] + ln_b_ref[...]).astype(_BF16)
    row = lax.broadcasted_iota(jnp.int32, (CHUNK, CHUNK), 0)
    col = lax.broadcasted_iota(jnp.int32, (CHUNK, CHUNK), 1)
    for g in range(A_GROUPS):
        cs = slice(g * A_GROUP_DIM, (g + 1) * A_GROUP_DIM)
        w = jnp.where(row >= col, ws_ref[g], 0.0).astype(_BF16)
        bias = bs_ref[:, g:g + 1]
        for c in range(ROW_TILE // CHUNK):
            rs = slice(c * CHUNK, (c + 1) * CHUNK)
            mixed = jnp.dot(w, vn[rs, cs], preferred_element_type=_F32) + bias
            ya_ref[rs, cs] = (u[rs, cs] * mixed).astype(_BF16)

    p = proj[:, 2 * A_WIDTH:]

    @pl.when(j == 0)
    def _():
        carry_ref[...] = jnp.zeros_like(carry_ref)

    ext_ref[0:POOL_HALO, :] = carry_ref[...]
    ext_ref[POOL_HALO:, :] = p
    carry_ref[...] = p[ROW_TILE - POOL_HALO:, :]
    pos = j * ROW_TILE + lax.broadcasted_iota(jnp.int32, (ROW_TILE, B_GROUP_DIM), 0)
    for g, win in enumerate(POOL_WINDOWS):
        cs = slice(g * B_GROUP_DIM, (g + 1) * B_GROUP_DIM)
        tok = p[:, cs]
        acc = tok
        for k in range(1, win):
            acc = acc + ext_ref[POOL_HALO - k:POOL_HALO - k + ROW_TILE, cs]
        count = jnp.minimum(pos + 1, win).astype(_F32)
        pooled = (acc / count - tok).astype(_BF16)
        y = jnp.dot(pooled, pw_ref[g], preferred_element_type=_F32)
        yb_ref[:, cs] = (y * ps_ref[:, cs]).astype(_BF16)


def _even_mixer(h, g0, w_in, ln_g, ln_b, w_s, b_s_t, pool_w, pool_scale):
    bsz, seq, _ = h.shape
    grid = (bsz, seq // ROW_TILE)
    out = jax.ShapeDtypeStruct((bsz, seq, A_WIDTH), _BF16)
    return pl.pallas_call(
        _even_mixer_kernel,
        grid=grid,
        in_specs=[
            _row_spec(D_MODEL),
            _const_spec((1, D_MODEL)),
            _const_spec((D_MODEL, EVEN_IN)),
            _const_spec((1, A_WIDTH)),
            _const_spec((1, A_WIDTH)),
            _const_spec((A_GROUPS, CHUNK, CHUNK)),
            _const_spec((CHUNK, A_GROUPS)),
            _const_spec((len(POOL_WINDOWS), B_GROUP_DIM, B_GROUP_DIM)),
            _const_spec((1, B_WIDTH)),
        ],
        out_specs=[_row_spec(A_WIDTH), _row_spec(B_WIDTH)],
        out_shape=[out, out],
        scratch_shapes=[
            pltpu.VMEM((POOL_HALO, B_WIDTH), _F32),
            pltpu.VMEM((ROW_TILE + POOL_HALO, B_WIDTH), _F32),
        ],
        compiler_params=pltpu.CompilerParams(
            dimension_semantics=("arbitrary", "arbitrary"),
            vmem_limit_bytes=VMEM_LIMIT_BYTES),
        name="even_mixer",
    )(h, g0, w_in, ln_g, ln_b, w_s, b_s_t, pool_w, pool_scale)


def _odd_proj_kernel(x_ref, g_ref, w_in_ref, cw_ref, q_ref, k_ref, vt_ref, yd_ref,
                     carry_ref, ext_ref):
    j = pl.program_id(1)
    hn = _rms(x_ref[...], g_ref[...]).astype(_BF16)
    proj = jnp.dot(hn, w_in_ref[...], preferred_element_type=_F32)
    q_ref[...] = (proj[:, :C_WIDTH] * (C_QK_DIM ** -0.5 * LOG2E)).astype(_BF16)
    k_ref[...] = proj[:, C_WIDTH:2 * C_WIDTH].astype(_BF16)
    vt_ref[...] = proj[:, 2 * C_WIDTH:3 * C_WIDTH].T.astype(_BF16)
    o = 3 * C_WIDTH
    bg = proj[:, o:o + D_WIDTH]
    z = proj[:, o + D_WIDTH:o + 2 * D_WIDTH] * proj[:, o + 2 * D_WIDTH:]

    @pl.when(j == 0)
    def _():
        carry_ref[...] = jnp.zeros_like(carry_ref)

    ext_ref[0:CONV_HALO, :] = carry_ref[...]
    ext_ref[CONV_HALO:, :] = z
    carry_ref[...] = z[ROW_TILE - CONV_HALO:, :]
    y = cw_ref[CONV_WIDTH - 1:CONV_WIDTH, :] * z
    for t in range(CONV_WIDTH - 1):
        shift = CONV_WIDTH - 1 - t
        y = y + cw_ref[t:t + 1, :] * ext_ref[CONV_HALO - shift:CONV_HALO - shift + ROW_TILE, :]
    yd_ref[...] = (bg * y).astype(_BF16)


def _odd_proj(h, g0, w_in, conv_w):
    bsz, seq, _ = h.shape
    grid = (bsz, seq // ROW_TILE)
    out = jax.ShapeDtypeStruct((bsz, seq, C_WIDTH), _BF16)
    out_t = jax.ShapeDtypeStruct((bsz, C_WIDTH, seq), _BF16)
    spec_t = pl.BlockSpec((None, C_WIDTH, ROW_TILE), lambda b, j: (b, 0, j))
    return pl.pallas_call(
        _odd_proj_kernel,
        grid=grid,
        in_specs=[
            _row_spec(D_MODEL),
            _const_spec((1, D_MODEL)),
            _const_spec((D_MODEL, ODD_IN)),
            _const_spec((CONV_WIDTH, D_WIDTH)),
        ],
        out_specs=[_row_spec(C_WIDTH), _row_spec(C_WIDTH), spec_t, _row_spec(D_WIDTH)],
        out_shape=[out, out, out_t, out],
        scratch_shapes=[
            pltpu.VMEM((CONV_HALO, D_WIDTH), _F32),
            pltpu.VMEM((ROW_TILE + CONV_HALO, D_WIDTH), _F32),
        ],
        compiler_params=pltpu.CompilerParams(
            dimension_semantics=("arbitrary", "arbitrary"),
            vmem_limit_bytes=VMEM_LIMIT_BYTES),
        name="odd_proj",
    )(h, g0, w_in, conv_w)


def _t5_bucket_upper_bounds(max_dist):
    d = np.arange(max_dist, dtype=np.int32)
    max_exact = REL_BUCKETS // 2
    nf = np.maximum(d, 1).astype(np.float32)
    large = max_exact + (np.log(nf / np.float32(max_exact))
                         / np.float32(math.log(REL_MAX_DIST / max_exact))
                         * np.float32(REL_BUCKETS - max_exact)).astype(np.int32)
    large = np.minimum(large, REL_BUCKETS - 1)
    bucket = np.where(d < max_exact, d, large)
    assert np.all(np.diff(bucket) >= 0)
    return bucket, {int(b): int(d[bucket == b].max()) for b in np.unique(bucket)}


def _attn_kernel(bucket_hi, lambda_init,
                 tab_ref, q_ref, k_ref, vt_ref, lam_ref, sg_ref, o_ref,
                 qq_ref, bias_ref, m_ref, acc_ref, s_ref, p_ref, al_ref):
    t = ATT_TILE
    hd_dim = 2 * C_QK_DIM
    i = pl.program_id(1)
    last_bucket = REL_BUCKETS - 1
    heads = range(C_HEADS)

    @pl.when(i == 0)
    def _():
        kpos = lax.broadcasted_iota(jnp.int32, (t, t), 0)
        qpos = lax.broadcasted_iota(jnp.int32, (t, t), 1)
        for delta in range(2):
            d = qpos - kpos + delta * t
            for hd in heads:
                far = tab_ref[last_bucket, hd]
                val = jnp.zeros((t, t), _F32)
                for b in sorted(bucket_hi, reverse=True):
                    if b == last_bucket:
                        continue
                    val = jnp.where(d <= bucket_hi[b], (tab_ref[b, hd] - far) * LOG2E, val)
                val = jnp.where(d >= 0, val, MASK_VALUE)
                bias_ref[delta, hd] = jnp.concatenate([val, val], axis=1)

    lane = lax.broadcasted_iota(jnp.int32, (t, hd_dim), 1)
    for hd in heads:
        q = q_ref[:, hd * hd_dim:(hd + 1) * hd_dim]
        zero = jnp.zeros_like(q)
        qq_ref[hd, 0:t, :] = jnp.where(lane < C_QK_DIM, q, zero)
        qq_ref[hd, t:, :] = jnp.where(lane >= C_QK_DIM, q, zero)
    m_ref[...] = jnp.full_like(m_ref, MASK_VALUE)
    acc_ref[...] = jnp.zeros_like(acc_ref)
    p_ref[C_HEADS - 1] = jnp.zeros((t, 2 * t), _BF16)
    al_ref[C_HEADS - 1] = jnp.ones((1, 2 * t), _F32)

    def scores(jk, hd):
        start = pl.multiple_of(jk * t, t)
        k_t = k_ref[pl.ds(start, t), hd * hd_dim:(hd + 1) * hd_dim]
        s_ref[hd] = lax.dot_general(k_t, qq_ref[hd], (((1,), (1,)), ((), ())),
                                    preferred_element_type=_F32)

    def values(jk, hd):
        start = pl.multiple_of(jk * t, t)
        vt = vt_ref[hd * C_V_DIM:(hd + 1) * C_V_DIM, pl.ds(start, t)]
        vt_ones = jnp.concatenate([vt, jnp.ones((SUM_ROWS, t), _BF16)], axis=0)
        acc_ref[hd] = al_ref[hd] * acc_ref[hd] + jnp.dot(
            vt_ones, p_ref[hd], preferred_element_type=_F32)

    def step(jk, delta, jk_next):
        for hd in heads:
            ahead = hd + SCORE_LOOKAHEAD
            if ahead < C_HEADS:
                scores(jk, ahead)
            elif jk_next is not None:
                scores(jk_next, ahead - C_HEADS)
            s = s_ref[hd]
            if delta is not None:
                s = s + bias_ref[delta, hd]
            m_prev = m_ref[hd]
            m_next = jnp.maximum(m_prev, jnp.max(s, axis=0, keepdims=True))
            alpha = jnp.exp2(m_prev - m_next)
            p = jnp.exp2(s - m_next)
            m_ref[hd] = m_next
            if hd == 0:
                values(jnp.maximum(jk - 1, 0), C_HEADS - 1)
            else:
                values(jk, hd - 1)
            p_ref[hd] = p.astype(_BF16)
            al_ref[hd] = alpha

    for hd in range(SCORE_LOOKAHEAD):
        scores(0, hd)

    n_far = jnp.maximum(i - 1, 0)

    def far_pair(r, carry):
        step(2 * r, None, 2 * r + 1)
        step(2 * r + 1, None, 2 * r + 2)
        return carry

    lax.fori_loop(0, lax.shift_right_logical(n_far, 1), far_pair, 0)

    @pl.when(n_far % 2 == 1)
    def _():
        step(n_far - 1, None, n_far)

    @pl.when(i >= 1)
    def _():
        step(i - 1, 1, i)

    step(i, 0, None)
    values(i, C_HEADS - 1)

    lp = lam_ref[...]
    lam = (jnp.exp(jnp.sum(lp[0:1] * lp[1:2], axis=-1, keepdims=True))
           - jnp.exp(jnp.sum(lp[2:3] * lp[3:4], axis=-1, keepdims=True)) + lambda_init)
    for hd in heads:
        o = acc_ref[hd, :C_V_DIM, :] / acc_ref[hd, C_V_DIM:C_V_DIM + 1, :]
        a = o[:, :t] - lam * o[:, t:]
        ms = jnp.mean(a * a, axis=0, keepdims=True)
        y = a * lax.rsqrt(ms + EPS) * sg_ref[...] * (1.0 - lambda_init)
        o_ref[:, hd * C_V_DIM:(hd + 1) * C_V_DIM] = y.T.astype(_BF16)


def _diff_attention(q, k, vt, rel_table, lam_params, subln_g, lambda_init):
    bsz, seq, _ = q.shape
    t = ATT_TILE
    bucket, bucket_hi = _t5_bucket_upper_bounds(seq)
    assert np.all(bucket[t + 1:] == REL_BUCKETS - 1)
    grid = (bsz, seq // t)
    kernel = functools.partial(_attn_kernel, bucket_hi, lambda_init)
    return pl.pallas_call(
        kernel,
        grid=grid,
        in_specs=[
            pl.BlockSpec(memory_space=pltpu.SMEM),
            pl.BlockSpec((None, t, C_WIDTH), lambda b, i: (b, i, 0)),
            pl.BlockSpec((None, seq, C_WIDTH), lambda b, i: (b, 0, 0)),
            pl.BlockSpec((None, C_WIDTH, seq), lambda b, i: (b, 0, 0)),
            pl.BlockSpec((4, C_QK_DIM), lambda b, i: (0, 0)),
            pl.BlockSpec((C_V_DIM, 1), lambda b, i: (0, 0)),
        ],
        out_specs=pl.BlockSpec((None, t, C_WIDTH), lambda b, i: (b, i, 0)),
        out_shape=jax.ShapeDtypeStruct((bsz, seq, C_WIDTH), _BF16),
        scratch_shapes=[
            pltpu.VMEM((C_HEADS, 2 * t, 2 * C_QK_DIM), _BF16),
            pltpu.VMEM((2, C_HEADS, t, 2 * t), _F32),
            pltpu.VMEM((C_HEADS, 1, 2 * t), _F32),
            pltpu.VMEM((C_HEADS, C_V_DIM + SUM_ROWS, 2 * t), _F32),
            pltpu.VMEM((C_HEADS, t, 2 * t), _F32),
            pltpu.VMEM((C_HEADS, t, 2 * t), _BF16),
            pltpu.VMEM((C_HEADS, 1, 2 * t), _F32),
        ],
        compiler_params=pltpu.CompilerParams(
            dimension_semantics=("arbitrary", "arbitrary"),
            vmem_limit_bytes=VMEM_LIMIT_BYTES),
        name="diff_attention",
    )(rel_table, q, k, vt, lam_params, subln_g)


def _out_mlp_kernel(h_ref, ya_ref, yb_ref, g_ref, w_out_ref, w1_ref, w2_ref, o_ref):
    half = D_MODEL // 2
    y = (jnp.dot(ya_ref[...], w_out_ref[:half, :], preferred_element_type=_F32)
         + jnp.dot(yb_ref[...], w_out_ref[half:, :], preferred_element_type=_F32))
    h1 = h_ref[...] + _rms(y, g_ref[1:2, :])
    hn = _rms(h1, g_ref[2:3, :]).astype(_BF16)
    acc = jnp.zeros((ROW_TILE, D_MODEL), _F32)
    for c in range(D_FF // FF_CHUNK):
        cs = slice(c * FF_CHUNK, (c + 1) * FF_CHUNK)
        a = jnp.dot(hn, w1_ref[:, cs], preferred_element_type=_F32)
        a = jnp.square(jnp.maximum(a, 0.0)).astype(_BF16)
        acc = acc + jnp.dot(a, w2_ref[cs, :], preferred_element_type=_F32)
    o_ref[...] = h1 + _rms(acc, g_ref[3:4, :])


def _out_mlp(h, ya, yb, g, w_out, w1, w2):
    bsz, seq, _ = h.shape
    grid = (bsz, seq // ROW_TILE)
    return pl.pallas_call(
        _out_mlp_kernel,
        grid=grid,
        in_specs=[
            _row_spec(D_MODEL),
            _row_spec(D_MODEL // 2),
            _row_spec(D_MODEL // 2),
            _const_spec((4, D_MODEL)),
            _const_spec((D_MODEL, D_MODEL)),
            _const_spec((D_MODEL, D_FF)),
            _const_spec((D_FF, D_MODEL)),
        ],
        out_specs=_row_spec(D_MODEL),
        out_shape=jax.ShapeDtypeStruct(h.shape, h.dtype),
        compiler_params=pltpu.CompilerParams(
            dimension_semantics=("arbitrary", "arbitrary"),
            vmem_limit_bytes=VMEM_LIMIT_BYTES),
        name="out_mlp",
    )(h, ya, yb, g, w_out, w1, w2)


def kernel(x, rel_bias_table, norm_g, even_w_in, even_ln_g, even_ln_b, even_spatial_w,
           even_spatial_b, even_pool_w, even_pool_scale, even_w_out, odd_w_in, odd_lambda,
           odd_subln_g, odd_conv_w, odd_w_out, ffn_w1, ffn_w2):
    depth = norm_g.shape[0]
    bf = lambda w: w.astype(_BF16)
    h = x
    for layer in range(depth):
        g = norm_g[layer]
        if layer % 2 == 0:
            e = layer // 2
            ya, yb = _even_mixer(
                h, g[0:1], bf(even_w_in[e]), even_ln_g[e][None], even_ln_b[e][None],
                even_spatial_w[e], even_spatial_b[e].T, bf(even_pool_w[e]),
                even_pool_scale[e][None])
            w_out = even_w_out[e]
        else:
            o = layer // 2
            lambda_init = 0.8 - 0.6 * math.exp(-0.3 * layer)
            q, k, vt, yb = _odd_proj(h, g[0:1], bf(odd_w_in[o]), odd_conv_w[o])
            ya = _diff_attention(q, k, vt, rel_bias_table, odd_lambda[o],
                                 odd_subln_g[o][:, None], lambda_init)
            w_out = odd_w_out[o]
        h = _out_mlp(h, ya, yb, g, bf(w_out), bf(ffn_w1[layer]), bf(ffn_w2[layer]))
    return h
```

```python
import functools
import math

import numpy as np
import jax
import jax.numpy as jnp
from jax import lax
from jax.experimental import pallas as pl
from jax.experimental.pallas import tpu as pltpu

D_MODEL = 1024
A_WIDTH = 512
A_GROUPS = 4
A_GROUP_DIM = 128
CHUNK = 128
B_WIDTH = 512
POOL_WINDOWS = (2, 4, 8, 16)
B_GROUP_DIM = 128
C_HEADS = 4
C_QK_DIM = 64
C_V_DIM = 128
C_WIDTH = 512
D_WIDTH = 512
CONV_WIDTH = 3
REL_BUCKETS = 32
REL_MAX_DIST = 128
D_FF = 4096
EPS = 1e-6
EVEN_IN = 2 * A_WIDTH + B_WIDTH
ODD_IN = 2 * C_WIDTH + C_WIDTH + 3 * D_WIDTH

ROW_TILE = 512
ATT_TILE = 256
FF_CHUNK = 1024
MLP_ROW_TILE = 1024
MLP_SUBTILES = 2
POOL_HALO = 16
CONV_HALO = 8
LOG2E = math.log2(math.e)
SCORE_LOOKAHEAD = 1
SUM_ROWS = 16
MASK_VALUE = -1e30
VMEM_LIMIT_BYTES = 52 * 1024 * 1024

_F32 = jnp.float32
_BF16 = jnp.bfloat16


def _rms(x, g):
    return x * lax.rsqrt(jnp.mean(x * x, axis=-1, keepdims=True) + EPS) * g


def _gelu_tanh(x):
    c = math.sqrt(2.0 / math.pi)
    hx = 0.5 * x
    return hx + hx * jnp.tanh(x * (c + (c * 0.044715) * (x * x)))


def _const_spec(shape):
    nd = len(shape)
    return pl.BlockSpec(shape, lambda *_: (0,) * nd, pipeline_mode=pl.Buffered(1))


def _row_spec(width, col=0, tile=None):
    return pl.BlockSpec((None, tile or ROW_TILE, width), lambda b, j: (b, j, col))


def _even_mixer_kernel(x_ref, g_ref, w_in_ref, ln_g_ref, ln_b_ref, ws_ref, bs_ref,
                       pw_ref, ps_ref, ya_ref, yb_ref, carry_ref):
    j = pl.program_id(1)
    hn = _rms(x_ref[...], g_ref[...]).astype(_BF16)
    proj = jnp.dot(hn, w_in_ref[...], preferred_element_type=_F32)

    z = _gelu_tanh(proj[:, :2 * A_WIDTH])
    u = z[:, :A_WIDTH]
    v = z[:, A_WIDTH:]
    mu = jnp.mean(v, axis=-1, keepdims=True)
    vc = v - mu
    vn = vc * lax.rsqrt(jnp.mean(vc * vc, axis=-1, keepdims=True) + EPS)
    vn = (vn * ln_g_ref[...] + ln_b_ref[...]).astype(_BF16)
    row = lax.broadcasted_iota(jnp.int32, (CHUNK, CHUNK), 0)
    col = lax.broadcasted_iota(jnp.int32, (CHUNK, CHUNK), 1)
    for g in range(A_GROUPS):
        cs = slice(g * A_GROUP_DIM, (g + 1) * A_GROUP_DIM)
        w = jnp.where(row >= col, ws_ref[g], 0.0).astype(_BF16)
        bias = bs_ref[:, g:g + 1]
        for c in range(0, ROW_TILE // CHUNK, 2):
            r0 = slice(c * CHUNK, (c + 1) * CHUNK)
            r1 = slice((c + 1) * CHUNK, (c + 2) * CHUNK)
            rhs = jnp.concatenate([vn[r0, cs], vn[r1, cs]], axis=1)
            mixed = jnp.dot(w, rhs, preferred_element_type=_F32) + bias
            ya_ref[r0, cs] = (u[r0, cs] * mixed[:, :A_GROUP_DIM]).astype(_BF16)
            ya_ref[r1, cs] = (u[r1, cs] * mixed[:, A_GROUP_DIM:]).astype(_BF16)

    p = proj[:, 2 * A_WIDTH:]

    @pl.when(j == 0)
    def _():
        carry_ref[...] = jnp.zeros_like(carry_ref)

    cur = jnp.concatenate([carry_ref[...], p], axis=0)
    carry_ref[...] = p[ROW_TILE - POOL_HALO:, :]
    cur_win = 1
    pos = j * ROW_TILE + lax.broadcasted_iota(jnp.int32, (ROW_TILE, B_GROUP_DIM), 0)
    tokens_so_far = (pos + 1).astype(_F32)
    for g, win in enumerate(POOL_WINDOWS):
        cs = slice(g * B_GROUP_DIM, (g + 1) * B_GROUP_DIM)
        while cur_win < win:
            cur = cur + pltpu.roll(cur, cur_win, axis=0)
            cur_win *= 2
        assert cur_win == win and win <= POOL_HALO
        count = jnp.minimum(tokens_so_far, float(win))
        pooled = (cur[POOL_HALO:, :B_GROUP_DIM] / count - p[:, cs]).astype(_BF16)
        y = jnp.dot(pooled, pw_ref[g], preferred_element_type=_F32)
        yb_ref[:, cs] = (y * ps_ref[:, cs]).astype(_BF16)
        cur = cur[:, B_GROUP_DIM:]


def _even_mixer(h, g0, w_in, ln_g, ln_b, w_s, b_s_t, pool_w, pool_scale):
    bsz, seq, _ = h.shape
    grid = (bsz, seq // ROW_TILE)
    out = jax.ShapeDtypeStruct((bsz, seq, A_WIDTH), _BF16)
    return pl.pallas_call(
        _even_mixer_kernel,
        grid=grid,
        in_specs=[
            _row_spec(D_MODEL),
            _const_spec((1, D_MODEL)),
            _const_spec((D_MODEL, EVEN_IN)),
            _const_spec((1, A_WIDTH)),
            _const_spec((1, A_WIDTH)),
            _const_spec((A_GROUPS, CHUNK, CHUNK)),
            _const_spec((CHUNK, A_GROUPS)),
            _const_spec((len(POOL_WINDOWS), B_GROUP_DIM, B_GROUP_DIM)),
            _const_spec((1, B_WIDTH)),
        ],
        out_specs=[_row_spec(A_WIDTH), _row_spec(B_WIDTH)],
        out_shape=[out, out],
        scratch_shapes=[
            pltpu.VMEM((POOL_HALO, B_WIDTH), _F32),
        ],
        compiler_params=pltpu.CompilerParams(
            dimension_semantics=("arbitrary", "arbitrary"),
            vmem_limit_bytes=VMEM_LIMIT_BYTES),
        name="even_mixer",
    )(h, g0, w_in, ln_g, ln_b, w_s, b_s_t, pool_w, pool_scale)


def _odd_proj_kernel(x_ref, g_ref, w_in_ref, cw_ref, q_ref, k_ref, vt_ref, yd_ref,
                     carry_ref, ext_ref):
    j = pl.program_id(1)
    hn = _rms(x_ref[...], g_ref[...]).astype(_BF16)
    proj = jnp.dot(hn, w_in_ref[...], preferred_element_type=_F32)
    q_ref[...] = (proj[:, :C_WIDTH] * (C_QK_DIM ** -0.5 * LOG2E)).astype(_BF16)
    k_ref[...] = proj[:, C_WIDTH:2 * C_WIDTH].astype(_BF16)
    vt_ref[...] = proj[:, 2 * C_WIDTH:3 * C_WIDTH].T.astype(_BF16)
    o = 3 * C_WIDTH
    bg = proj[:, o:o + D_WIDTH]
    z = proj[:, o + D_WIDTH:o + 2 * D_WIDTH] * proj[:, o + 2 * D_WIDTH:]

    @pl.when(j == 0)
    def _():
        carry_ref[...] = jnp.zeros_like(carry_ref)

    ext_ref[0:CONV_HALO, :] = carry_ref[...]
    ext_ref[CONV_HALO:, :] = z
    carry_ref[...] = z[ROW_TILE - CONV_HALO:, :]
    y = cw_ref[CONV_WIDTH - 1:CONV_WIDTH, :] * z
    for t in range(CONV_WIDTH - 1):
        shift = CONV_WIDTH - 1 - t
        y = y + cw_ref[t:t + 1, :] * ext_ref[CONV_HALO - shift:CONV_HALO - shift + ROW_TILE, :]
    yd_ref[...] = (bg * y).astype(_BF16)


def _odd_proj(h, g0, w_in, conv_w):
    bsz, seq, _ = h.shape
    grid = (bsz, seq // ROW_TILE)
    out = jax.ShapeDtypeStruct((bsz, seq, C_WIDTH), _BF16)
    out_t = jax.ShapeDtypeStruct((bsz, C_WIDTH, seq), _BF16)
    spec_t = pl.BlockSpec((None, C_WIDTH, ROW_TILE), lambda b, j: (b, 0, j))
    return pl.pallas_call(
        _odd_proj_kernel,
        grid=grid,
        in_specs=[
            _row_spec(D_MODEL),
            _const_spec((1, D_MODEL)),
            _const_spec((D_MODEL, ODD_IN)),
            _const_spec((CONV_WIDTH, D_WIDTH)),
        ],
        out_specs=[_row_spec(C_WIDTH), _row_spec(C_WIDTH), spec_t, _row_spec(D_WIDTH)],
        out_shape=[out, out, out_t, out],
        scratch_shapes=[
            pltpu.VMEM((CONV_HALO, D_WIDTH), _F32),
            pltpu.VMEM((ROW_TILE + CONV_HALO, D_WIDTH), _F32),
        ],
        compiler_params=pltpu.CompilerParams(
            dimension_semantics=("arbitrary", "arbitrary"),
            vmem_limit_bytes=VMEM_LIMIT_BYTES),
        name="odd_proj",
    )(h, g0, w_in, conv_w)


def _t5_bucket_upper_bounds(max_dist):
    d = np.arange(max_dist, dtype=np.int32)
    max_exact = REL_BUCKETS // 2
    nf = np.maximum(d, 1).astype(np.float32)
    large = max_exact + (np.log(nf / np.float32(max_exact))
                         / np.float32(math.log(REL_MAX_DIST / max_exact))
                         * np.float32(REL_BUCKETS - max_exact)).astype(np.int32)
    large = np.minimum(large, REL_BUCKETS - 1)
    bucket = np.where(d < max_exact, d, large)
    assert np.all(np.diff(bucket) >= 0)
    return bucket, {int(b): int(d[bucket == b].max()) for b in np.unique(bucket)}


def _attn_kernel(bucket_hi, lambda_init,
                 tab_ref, q_ref, k_ref, vt_ref, lam_ref, sg_ref, o_ref,
                 qq_ref, bias_ref, m_ref, acc_ref, s_ref, p_ref, al_ref):
    t = ATT_TILE
    hd_dim = 2 * C_QK_DIM
    i = pl.program_id(1)
    last_bucket = REL_BUCKETS - 1
    heads = range(C_HEADS)

    @pl.when(i == 0)
    def _():
        kpos = lax.broadcasted_iota(jnp.int32, (t, t), 0)
        qpos = lax.broadcasted_iota(jnp.int32, (t, t), 1)
        for delta in range(2):
            d = qpos - kpos + delta * t
            for hd in heads:
                far = tab_ref[last_bucket, hd]
                val = jnp.zeros((t, t), _F32)
                for b in sorted(bucket_hi, reverse=True):
                    if b == last_bucket:
                        continue
                    val = jnp.where(d <= bucket_hi[b], (tab_ref[b, hd] - far) * LOG2E, val)
                val = jnp.where(d >= 0, val, MASK_VALUE)
                bias_ref[delta, hd] = jnp.concatenate([val, val], axis=1)

    lane = lax.broadcasted_iota(jnp.int32, (t, hd_dim), 1)
    for hd in heads:
        q = q_ref[:, hd * hd_dim:(hd + 1) * hd_dim]
        zero = jnp.zeros_like(q)
        qq_ref[hd, 0:t, :] = jnp.where(lane < C_QK_DIM, q, zero)
        qq_ref[hd, t:, :] = jnp.where(lane >= C_QK_DIM, q, zero)
    m_ref[...] = jnp.full_like(m_ref, MASK_VALUE)
    acc_ref[...] = jnp.zeros_like(acc_ref)
    p_ref[C_HEADS - 1] = jnp.zeros((t, 2 * t), _BF16)
    al_ref[C_HEADS - 1] = jnp.ones((1, 2 * t), _F32)

    def scores(jk, hd):
        start = pl.multiple_of(jk * t, t)
        k_t = k_ref[pl.ds(start, t), hd * hd_dim:(hd + 1) * hd_dim]
        s_ref[hd] = lax.dot_general(k_t, qq_ref[hd], (((1,), (1,)), ((), ())),
                                    preferred_element_type=_F32)

    def values(jk, hd):
        start = pl.multiple_of(jk * t, t)
        vt = vt_ref[hd * C_V_DIM:(hd + 1) * C_V_DIM, pl.ds(start, t)]
        vt_ones = jnp.concatenate([vt, jnp.ones((SUM_ROWS, t), _BF16)], axis=0)
        acc_ref[hd] = al_ref[hd] * acc_ref[hd] + jnp.dot(
            vt_ones, p_ref[hd], preferred_element_type=_F32)

    def step(jk, delta, jk_next):
        for hd in heads:
            ahead = hd + SCORE_LOOKAHEAD
            if ahead < C_HEADS:
                scores(jk, ahead)
            elif jk_next is not None:
                scores(jk_next, ahead - C_HEADS)
            s = s_ref[hd]
            if delta is not None:
                s = s + bias_ref[delta, hd]
            m_prev = m_ref[hd]
            m_next = jnp.maximum(m_prev, jnp.max(s, axis=0, keepdims=True))
            alpha = jnp.exp2(m_prev - m_next)
            p = jnp.exp2(s - m_next)
            m_ref[hd] = m_next
            if hd == 0:
                values(jnp.maximum(jk - 1, 0), C_HEADS - 1)
            else:
                values(jk, hd - 1)
            p_ref[hd] = p.astype(_BF16)
            al_ref[hd] = alpha

    for hd in range(SCORE_LOOKAHEAD):
        scores(0, hd)

    n_far = jnp.maximum(i - 1, 0)

    def far_pair(r, carry):
        step(2 * r, None, 2 * r + 1)
        step(2 * r + 1, None, 2 * r + 2)
        return carry

    lax.fori_loop(0, lax.shift_right_logical(n_far, 1), far_pair, 0)

    def finish(odd_far_tile, sub_diagonal):
        if odd_far_tile:
            step(n_far - 1, None, n_far)
        if sub_diagonal:
            step(i - 1, 1, i)
        step(i, 0, None)
        values(i, C_HEADS - 1)
        lp = lam_ref[...]
        lam = (jnp.exp(jnp.sum(lp[0:1] * lp[1:2], axis=-1, keepdims=True))
               - jnp.exp(jnp.sum(lp[2:3] * lp[3:4], axis=-1, keepdims=True)) + lambda_init)
        for hd in heads:
            o = acc_ref[hd, :C_V_DIM, :] / acc_ref[hd, C_V_DIM:C_V_DIM + 1, :]
            a = o[:, :t] - lam * o[:, t:]
            ms = jnp.mean(a * a, axis=0, keepdims=True)
            y = a * lax.rsqrt(ms + EPS) * sg_ref[...] * (1.0 - lambda_init)
            o_ref[:, hd * C_V_DIM:(hd + 1) * C_V_DIM] = y.T.astype(_BF16)

    odd_far = n_far % 2 == 1
    pl.when(i == 0)(lambda: finish(False, False))
    pl.when(jnp.logical_and(i >= 1, jnp.logical_not(odd_far)))(lambda: finish(False, True))
    pl.when(odd_far)(lambda: finish(True, True))


def _diff_attention(q, k, vt, rel_table, lam_params, subln_g, lambda_init):
    bsz, seq, _ = q.shape
    t = ATT_TILE
    bucket, bucket_hi = _t5_bucket_upper_bounds(seq)
    assert np.all(bucket[t + 1:] == REL_BUCKETS - 1)
    grid = (bsz, seq // t)
    kernel = functools.partial(_attn_kernel, bucket_hi, lambda_init)
    return pl.pallas_call(
        kernel,
        grid=grid,
        in_specs=[
            pl.BlockSpec(memory_space=pltpu.SMEM),
            pl.BlockSpec((None, t, C_WIDTH), lambda b, i: (b, i, 0)),
            pl.BlockSpec((None, seq, C_WIDTH), lambda b, i: (b, 0, 0)),
            pl.BlockSpec((None, C_WIDTH, seq), lambda b, i: (b, 0, 0)),
            pl.BlockSpec((4, C_QK_DIM), lambda b, i: (0, 0)),
            pl.BlockSpec((C_V_DIM, 1), lambda b, i: (0, 0)),
        ],
        out_specs=pl.BlockSpec((None, t, C_WIDTH), lambda b, i: (b, i, 0)),
        out_shape=jax.ShapeDtypeStruct((bsz, seq, C_WIDTH), _BF16),
        scratch_shapes=[
            pltpu.VMEM((C_HEADS, 2 * t, 2 * C_QK_DIM), _BF16),
            pltpu.VMEM((2, C_HEADS, t, 2 * t), _F32),
            pltpu.VMEM((C_HEADS, 1, 2 * t), _F32),
            pltpu.VMEM((C_HEADS, C_V_DIM + SUM_ROWS, 2 * t), _F32),
            pltpu.VMEM((C_HEADS, t, 2 * t), _F32),
            pltpu.VMEM((C_HEADS, t, 2 * t), _BF16),
            pltpu.VMEM((C_HEADS, 1, 2 * t), _F32),
        ],
        compiler_params=pltpu.CompilerParams(
            dimension_semantics=("arbitrary", "arbitrary"),
            vmem_limit_bytes=VMEM_LIMIT_BYTES),
        name="diff_attention",
    )(rel_table, q, k, vt, lam_params, subln_g)


def _out_mlp_kernel(h_ref, ya_ref, yb_ref, g_ref, w_out_ref, w1_ref, w2_ref, o_ref):
    half = D_MODEL // 2
    sub = MLP_ROW_TILE // MLP_SUBTILES
    rows = [slice(r * sub, (r + 1) * sub) for r in range(MLP_SUBTILES)]
    ys = [jnp.dot(ya_ref[rs, :], w_out_ref[:half, :], preferred_element_type=_F32)
          + jnp.dot(yb_ref[rs, :], w_out_ref[half:, :], preferred_element_type=_F32)
          for rs in rows]
    h1s = [h_ref[rs, :] + _rms(y, g_ref[1:2, :]) for rs, y in zip(rows, ys)]
    hns = [_rms(h1, g_ref[2:3, :]).astype(_BF16) for h1 in h1s]
    for rs, h1, hn in zip(rows, h1s, hns):
        acc = jnp.zeros((sub, D_MODEL), _F32)
        for c in range(D_FF // FF_CHUNK):
            cs = slice(c * FF_CHUNK, (c + 1) * FF_CHUNK)
            a = jnp.dot(hn, w1_ref[:, cs], preferred_element_type=_F32)
            a = jnp.square(jnp.maximum(a, 0.0)).astype(_BF16)
            acc = acc + jnp.dot(a, w2_ref[cs, :], preferred_element_type=_F32)
        o_ref[rs, :] = h1 + _rms(acc, g_ref[3:4, :])


def _out_mlp(h, ya, yb, g, w_out, w1, w2):
    bsz, seq, _ = h.shape
    grid = (bsz, seq // MLP_ROW_TILE)
    row_spec = functools.partial(_row_spec, tile=MLP_ROW_TILE)
    return pl.pallas_call(
        _out_mlp_kernel,
        grid=grid,
        in_specs=[
            row_spec(D_MODEL),
            row_spec(D_MODEL // 2),
            row_spec(D_MODEL // 2),
            _const_spec((4, D_MODEL)),
            _const_spec((D_MODEL, D_MODEL)),
            _const_spec((D_MODEL, D_FF)),
            _const_spec((D_FF, D_MODEL)),
        ],
        out_specs=row_spec(D_MODEL),
        out_shape=jax.ShapeDtypeStruct(h.shape, h.dtype),
        compiler_params=pltpu.CompilerParams(
            dimension_semantics=("arbitrary", "arbitrary"),
            vmem_limit_bytes=VMEM_LIMIT_BYTES),
        name="out_mlp",
    )(h, ya, yb, g, w_out, w1, w2)


def kernel(x, rel_bias_table, norm_g, even_w_in, even_ln_g, even_ln_b, even_spatial_w,
           even_spatial_b, even_pool_w, even_pool_scale, even_w_out, odd_w_in, odd_lambda,
           odd_subln_g, odd_conv_w, odd_w_out, ffn_w1, ffn_w2):
    depth = norm_g.shape[0]
    bf = lambda w: w.astype(_BF16)
    h = x
    for layer in range(depth):
        g = norm_g[layer]
        if layer % 2 == 0:
            e = layer // 2
            ya, yb = _even_mixer(
                h, g[0:1], bf(even_w_in[e]), even_ln_g[e][None], even_ln_b[e][None],
                even_spatial_w[e], even_spatial_b[e].T, bf(even_pool_w[e]),
                even_pool_scale[e][None])
            w_out = even_w_out[e]
        else:
            o = layer // 2
            lambda_init = 0.8 - 0.6 * math.exp(-0.3 * layer)
            q, k, vt, yb = _odd_proj(h, g[0:1], bf(odd_w_in[o]), odd_conv_w[o])
            ya = _diff_attention(q, k, vt, rel_bias_table, odd_lambda[o],
                                 odd_subln_g[o][:, None], lambda_init)
            w_out = odd_w_out[o]
        h = _out_mlp(h, ya, yb, g, bf(w_out), bf(ffn_w1[layer]), bf(ffn_w2[layer]))
    return h
```

```python
import functools
import math

import numpy as np
import jax
import jax.numpy as jnp
from jax import lax
from jax.experimental import pallas as pl
from jax.experimental.pallas import tpu as pltpu

D_MODEL = 1024
A_WIDTH = 512
A_GROUPS = 4
A_GROUP_DIM = 128
CHUNK = 128
B_WIDTH = 512
POOL_WINDOWS = (2, 4, 8, 16)
B_GROUP_DIM = 128
C_HEADS = 4
C_QK_DIM = 64
C_V_DIM = 128
C_WIDTH = 512
D_WIDTH = 512
CONV_WIDTH = 3
REL_BUCKETS = 32
REL_MAX_DIST = 128
D_FF = 4096
EPS = 1e-6
EVEN_IN = 2 * A_WIDTH + B_WIDTH
ODD_IN = 2 * C_WIDTH + C_WIDTH + 3 * D_WIDTH

ROW_TILE = 512
ATT_TILE = 256
FF_CHUNK = 1024
PROJ_SUBTILES = 2
MLP_ROW_TILE = 1024
MLP_SUBTILES = 2
POOL_HALO = 16
CONV_HALO = 8
LOG2E = math.log2(math.e)
SCORE_LOOKAHEAD = 1
SUM_ROWS = 16
MASK_VALUE = -1e30
VMEM_LIMIT_BYTES = 52 * 1024 * 1024

_F32 = jnp.float32
_BF16 = jnp.bfloat16


def _rms(x, g):
    return x * lax.rsqrt(jnp.mean(x * x, axis=-1, keepdims=True) + EPS) * g


def _gelu_tanh(x):
    c = math.sqrt(2.0 / math.pi)
    hx = 0.5 * x
    return hx + hx * jnp.tanh(x * (c + (c * 0.044715) * (x * x)))


def _norm_project(x_ref, g_ref, w_ref):
    rows = x_ref.shape[0] // PROJ_SUBTILES
    parts = []
    for r in range(PROJ_SUBTILES):
        hn = _rms(x_ref[r * rows:(r + 1) * rows, :], g_ref[...]).astype(_BF16)
        parts.append(jnp.dot(hn, w_ref[...], preferred_element_type=_F32))
    return jnp.concatenate(parts, axis=0)


def _const_spec(shape):
    nd = len(shape)
    return pl.BlockSpec(shape, lambda *_: (0,) * nd, pipeline_mode=pl.Buffered(1))


def _row_spec(width, col=0, tile=None):
    return pl.BlockSpec((None, tile or ROW_TILE, width), lambda b, j: (b, j, col))


def _even_mixer_kernel(x_ref, g_ref, w_in_ref, ln_g_ref, ln_b_ref, ws_ref, bs_ref,
                       pw_ref, ps_ref, ya_ref, yb_ref, carry_ref):
    j = pl.program_id(1)
    proj = _norm_project(x_ref, g_ref, w_in_ref)

    z = _gelu_tanh(proj[:, :2 * A_WIDTH])
    u = z[:, :A_WIDTH]
    v = z[:, A_WIDTH:]
    mu = jnp.mean(v, axis=-1, keepdims=True)
    vc = v - mu
    vn = vc * lax.rsqrt(jnp.mean(vc * vc, axis=-1, keepdims=True) + EPS)
    vn = (vn * ln_g_ref[...] + ln_b_ref[...]).astype(_BF16)
    row = lax.broadcasted_iota(jnp.int32, (CHUNK, CHUNK), 0)
    col = lax.broadcasted_iota(jnp.int32, (CHUNK, CHUNK), 1)
    for g in range(A_GROUPS):
        cs = slice(g * A_GROUP_DIM, (g + 1) * A_GROUP_DIM)
        w = jnp.where(row >= col, ws_ref[g], 0.0).astype(_BF16)
        bias = bs_ref[:, g:g + 1]
        for c in range(0, ROW_TILE // CHUNK, 2):
            r0 = slice(c * CHUNK, (c + 1) * CHUNK)
            r1 = slice((c + 1) * CHUNK, (c + 2) * CHUNK)
            rhs = jnp.concatenate([vn[r0, cs], vn[r1, cs]], axis=1)
            mixed = jnp.dot(w, rhs, preferred_element_type=_F32) + bias
            ya_ref[r0, cs] = (u[r0, cs] * mixed[:, :A_GROUP_DIM]).astype(_BF16)
            ya_ref[r1, cs] = (u[r1, cs] * mixed[:, A_GROUP_DIM:]).astype(_BF16)

    p = proj[:, 2 * A_WIDTH:]

    @pl.when(j == 0)
    def _():
        carry_ref[...] = jnp.zeros_like(carry_ref)

    cur = jnp.concatenate([carry_ref[...], p], axis=0)
    carry_ref[...] = p[ROW_TILE - POOL_HALO:, :]
    cur_win = 1
    pos = j * ROW_TILE + lax.broadcasted_iota(jnp.int32, (ROW_TILE, B_GROUP_DIM), 0)
    tokens_so_far = (pos + 1).astype(_F32)
    for g, win in enumerate(POOL_WINDOWS):
        cs = slice(g * B_GROUP_DIM, (g + 1) * B_GROUP_DIM)
        while cur_win < win:
            cur = cur + pltpu.roll(cur, cur_win, axis=0)
            cur_win *= 2
        assert cur_win == win and win <= POOL_HALO
        count = jnp.minimum(tokens_so_far, float(win))
        pooled = (cur[POOL_HALO:, :B_GROUP_DIM] / count - p[:, cs]).astype(_BF16)
        y = jnp.dot(pooled, pw_ref[g], preferred_element_type=_F32)
        yb_ref[:, cs] = (y * ps_ref[:, cs]).astype(_BF16)
        cur = cur[:, B_GROUP_DIM:]


def _even_mixer(h, g0, w_in, ln_g, ln_b, w_s, b_s_t, pool_w, pool_scale):
    bsz, seq, _ = h.shape
    grid = (bsz, seq // ROW_TILE)
    out = jax.ShapeDtypeStruct((bsz, seq, A_WIDTH), _BF16)
    return pl.pallas_call(
        _even_mixer_kernel,
        grid=grid,
        in_specs=[
            _row_spec(D_MODEL),
            _const_spec((1, D_MODEL)),
            _const_spec((D_MODEL, EVEN_IN)),
            _const_spec((1, A_WIDTH)),
            _const_spec((1, A_WIDTH)),
            _const_spec((A_GROUPS, CHUNK, CHUNK)),
            _const_spec((CHUNK, A_GROUPS)),
            _const_spec((len(POOL_WINDOWS), B_GROUP_DIM, B_GROUP_DIM)),
            _const_spec((1, B_WIDTH)),
        ],
        out_specs=[_row_spec(A_WIDTH), _row_spec(B_WIDTH)],
        out_shape=[out, out],
        scratch_shapes=[
            pltpu.VMEM((POOL_HALO, B_WIDTH), _F32),
        ],
        compiler_params=pltpu.CompilerParams(
            dimension_semantics=("arbitrary", "arbitrary"),
            vmem_limit_bytes=VMEM_LIMIT_BYTES),
        name="even_mixer",
    )(h, g0, w_in, ln_g, ln_b, w_s, b_s_t, pool_w, pool_scale)


def _odd_proj_kernel(x_ref, g_ref, w_in_ref, cw_ref, q_ref, k_ref, vt_ref, yd_ref,
                     carry_ref, ext_ref):
    j = pl.program_id(1)
    proj = _norm_project(x_ref, g_ref, w_in_ref)
    q_ref[...] = (proj[:, :C_WIDTH] * (C_QK_DIM ** -0.5 * LOG2E)).astype(_BF16)
    k_ref[...] = proj[:, C_WIDTH:2 * C_WIDTH].astype(_BF16)
    vt_ref[...] = proj[:, 2 * C_WIDTH:3 * C_WIDTH].T.astype(_BF16)
    o = 3 * C_WIDTH
    bg = proj[:, o:o + D_WIDTH]
    z = proj[:, o + D_WIDTH:o + 2 * D_WIDTH] * proj[:, o + 2 * D_WIDTH:]

    @pl.when(j == 0)
    def _():
        carry_ref[...] = jnp.zeros_like(carry_ref)

    ext_ref[0:CONV_HALO, :] = carry_ref[...]
    ext_ref[CONV_HALO:, :] = z
    carry_ref[...] = z[ROW_TILE - CONV_HALO:, :]
    y = cw_ref[CONV_WIDTH - 1:CONV_WIDTH, :] * z
    for t in range(CONV_WIDTH - 1):
        shift = CONV_WIDTH - 1 - t
        y = y + cw_ref[t:t + 1, :] * ext_ref[CONV_HALO - shift:CONV_HALO - shift + ROW_TILE, :]
    yd_ref[...] = (bg * y).astype(_BF16)


def _odd_proj(h, g0, w_in, conv_w):
    bsz, seq, _ = h.shape
    grid = (bsz, seq // ROW_TILE)
    out = jax.ShapeDtypeStruct((bsz, seq, C_WIDTH), _BF16)
    out_t = jax.ShapeDtypeStruct((bsz, C_WIDTH, seq), _BF16)
    spec_t = pl.BlockSpec((None, C_WIDTH, ROW_TILE), lambda b, j: (b, 0, j))
    return pl.pallas_call(
        _odd_proj_kernel,
        grid=grid,
        in_specs=[
            _row_spec(D_MODEL),
            _const_spec((1, D_MODEL)),
            _const_spec((D_MODEL, ODD_IN)),
            _const_spec((CONV_WIDTH, D_WIDTH)),
        ],
        out_specs=[_row_spec(C_WIDTH), _row_spec(C_WIDTH), spec_t, _row_spec(D_WIDTH)],
        out_shape=[out, out, out_t, out],
        scratch_shapes=[
            pltpu.VMEM((CONV_HALO, D_WIDTH), _F32),
            pltpu.VMEM((ROW_TILE + CONV_HALO, D_WIDTH), _F32),
        ],
        compiler_params=pltpu.CompilerParams(
            dimension_semantics=("arbitrary", "arbitrary"),
            vmem_limit_bytes=VMEM_LIMIT_BYTES),
        name="odd_proj",
    )(h, g0, w_in, conv_w)


def _t5_bucket_upper_bounds(max_dist):
    d = np.arange(max_dist, dtype=np.int32)
    max_exact = REL_BUCKETS // 2
    nf = np.maximum(d, 1).astype(np.float32)
    large = max_exact + (np.log(nf / np.float32(max_exact))
                         / np.float32(math.log(REL_MAX_DIST / max_exact))
                         * np.float32(REL_BUCKETS - max_exact)).astype(np.int32)
    large = np.minimum(large, REL_BUCKETS - 1)
    bucket = np.where(d < max_exact, d, large)
    assert np.all(np.diff(bucket) >= 0)
    return bucket, {int(b): int(d[bucket == b].max()) for b in np.unique(bucket)}


def _attn_kernel(bucket_hi, lambda_init,
                 tab_ref, q_ref, k_ref, vt_ref, lam_ref, sg_ref, o_ref,
                 qq_ref, bias_ref, m_ref, acc_ref, s_ref, p_ref, al_ref):
    t = ATT_TILE
    hd_dim = 2 * C_QK_DIM
    i = pl.program_id(1)
    last_bucket = REL_BUCKETS - 1
    heads = range(C_HEADS)

    @pl.when(i == 0)
    def _():
        kpos = lax.broadcasted_iota(jnp.int32, (t, t), 0)
        qpos = lax.broadcasted_iota(jnp.int32, (t, t), 1)
        for delta in range(2):
            d = qpos - kpos + delta * t
            for hd in heads:
                far = tab_ref[last_bucket, hd]
                val = jnp.zeros((t, t), _F32)
                for b in sorted(bucket_hi, reverse=True):
                    if b == last_bucket:
                        continue
                    val = jnp.where(d <= bucket_hi[b], (tab_ref[b, hd] - far) * LOG2E, val)
                val = jnp.where(d >= 0, val, MASK_VALUE)
                bias_ref[delta, hd] = jnp.concatenate([val, val], axis=1)

    feat = lax.broadcasted_iota(jnp.int32, (hd_dim, t), 0)
    for hd in heads:
        qt = q_ref[:, hd * hd_dim:(hd + 1) * hd_dim].astype(_F32).T
        qq_ref[hd] = jnp.concatenate(
            [jnp.where(feat < C_QK_DIM, qt, 0.0), jnp.where(feat >= C_QK_DIM, qt, 0.0)],
            axis=1).astype(_BF16)
    m_ref[...] = jnp.full_like(m_ref, MASK_VALUE)
    acc_ref[...] = jnp.zeros_like(acc_ref)
    p_ref[C_HEADS - 1] = jnp.zeros((t, 2 * t), _BF16)
    al_ref[C_HEADS - 1] = jnp.ones((1, 2 * t), _F32)

    def scores(jk, hd):
        start = pl.multiple_of(jk * t, t)
        k_t = k_ref[pl.ds(start, t), hd * hd_dim:(hd + 1) * hd_dim]
        s_ref[hd] = jnp.dot(k_t, qq_ref[hd], preferred_element_type=_F32)

    def values(jk, hd):
        start = pl.multiple_of(jk * t, t)
        vt = vt_ref[hd * C_V_DIM:(hd + 1) * C_V_DIM, pl.ds(start, t)]
        vt_ones = jnp.concatenate([vt, jnp.ones((SUM_ROWS, t), _BF16)], axis=0)
        acc_ref[hd] = al_ref[hd] * acc_ref[hd] + jnp.dot(
            vt_ones, p_ref[hd], preferred_element_type=_F32)

    def step(jk, delta, jk_next):
        for hd in heads:
            ahead = hd + SCORE_LOOKAHEAD
            if ahead < C_HEADS:
                scores(jk, ahead)
            elif jk_next is not None:
                scores(jk_next, ahead - C_HEADS)
            s = s_ref[hd]
            if delta is not None:
                s = s + bias_ref[delta, hd]
            m_prev = m_ref[hd]
            m_next = jnp.maximum(m_prev, jnp.max(s, axis=0, keepdims=True))
            alpha = jnp.exp2(m_prev - m_next)
            p = jnp.exp2(s - m_next)
            m_ref[hd] = m_next
            if hd == 0:
                values(jnp.maximum(jk - 1, 0), C_HEADS - 1)
            else:
                values(jk, hd - 1)
            p_ref[hd] = p.astype(_BF16)
            al_ref[hd] = alpha

    for hd in range(SCORE_LOOKAHEAD):
        scores(0, hd)

    n_far = jnp.maximum(i - 1, 0)

    def far_pair(r, carry):
        step(2 * r, None, 2 * r + 1)
        step(2 * r + 1, None, 2 * r + 2)
        return carry

    lax.fori_loop(0, lax.shift_right_logical(n_far, 1), far_pair, 0)

    def finish(odd_far_tile, sub_diagonal):
        if odd_far_tile:
            step(n_far - 1, None, n_far)
        if sub_diagonal:
            step(i - 1, 1, i)
        step(i, 0, None)
        values(i, C_HEADS - 1)
        lp = lam_ref[...]
        lam = (jnp.exp(jnp.sum(lp[0:1] * lp[1:2], axis=-1, keepdims=True))
               - jnp.exp(jnp.sum(lp[2:3] * lp[3:4], axis=-1, keepdims=True)) + lambda_init)
        for hd in heads:
            o = acc_ref[hd, :C_V_DIM, :] / acc_ref[hd, C_V_DIM:C_V_DIM + 1, :]
            a = o[:, :t] - lam * o[:, t:]
            ms = jnp.mean(a * a, axis=0, keepdims=True)
            y = a * lax.rsqrt(ms + EPS) * sg_ref[...] * (1.0 - lambda_init)
            o_ref[:, hd * C_V_DIM:(hd + 1) * C_V_DIM] = y.T.astype(_BF16)

    odd_far = n_far % 2 == 1
    pl.when(i == 0)(lambda: finish(False, False))
    pl.when(jnp.logical_and(i >= 1, jnp.logical_not(odd_far)))(lambda: finish(False, True))
    pl.when(odd_far)(lambda: finish(True, True))


def _diff_attention(q, k, vt, rel_table, lam_params, subln_g, lambda_init):
    bsz, seq, _ = q.shape
    t = ATT_TILE
    bucket, bucket_hi = _t5_bucket_upper_bounds(seq)
    assert np.all(bucket[t + 1:] == REL_BUCKETS - 1)
    grid = (bsz, seq // t)
    kernel = functools.partial(_attn_kernel, bucket_hi, lambda_init)
    return pl.pallas_call(
        kernel,
        grid=grid,
        in_specs=[
            pl.BlockSpec(memory_space=pltpu.SMEM),
            pl.BlockSpec((None, t, C_WIDTH), lambda b, i: (b, i, 0)),
            pl.BlockSpec((None, seq, C_WIDTH), lambda b, i: (b, 0, 0)),
            pl.BlockSpec((None, C_WIDTH, seq), lambda b, i: (b, 0, 0)),
            pl.BlockSpec((4, C_QK_DIM), lambda b, i: (0, 0)),
            pl.BlockSpec((C_V_DIM, 1), lambda b, i: (0, 0)),
        ],
        out_specs=pl.BlockSpec((None, t, C_WIDTH), lambda b, i: (b, i, 0)),
        out_shape=jax.ShapeDtypeStruct((bsz, seq, C_WIDTH), _BF16),
        scratch_shapes=[
            pltpu.VMEM((C_HEADS, 2 * C_QK_DIM, 2 * t), _BF16),
            pltpu.VMEM((2, C_HEADS, t, 2 * t), _F32),
            pltpu.VMEM((C_HEADS, 1, 2 * t), _F32),
            pltpu.VMEM((C_HEADS, C_V_DIM + SUM_ROWS, 2 * t), _F32),
            pltpu.VMEM((C_HEADS, t, 2 * t), _F32),
            pltpu.VMEM((C_HEADS, t, 2 * t), _BF16),
            pltpu.VMEM((C_HEADS, 1, 2 * t), _F32),
        ],
        compiler_params=pltpu.CompilerParams(
            dimension_semantics=("arbitrary", "arbitrary"),
            vmem_limit_bytes=VMEM_LIMIT_BYTES),
        name="diff_attention",
    )(rel_table, q, k, vt, lam_params, subln_g)


def _out_mlp_kernel(h_ref, ya_ref, yb_ref, g_ref, w_out_ref, w1_ref, w2_ref, o_ref):
    half = D_MODEL // 2
    sub = MLP_ROW_TILE // MLP_SUBTILES
    rows = [slice(r * sub, (r + 1) * sub) for r in range(MLP_SUBTILES)]
    ys = [jnp.dot(ya_ref[rs, :], w_out_ref[:half, :], preferred_element_type=_F32)
          + jnp.dot(yb_ref[rs, :], w_out_ref[half:, :], preferred_element_type=_F32)
          for rs in rows]
    h1s = [h_ref[rs, :] + _rms(y, g_ref[1:2, :]) for rs, y in zip(rows, ys)]
    hns = [_rms(h1, g_ref[2:3, :]).astype(_BF16) for h1 in h1s]
    for rs, h1, hn in zip(rows, h1s, hns):
        acc = jnp.zeros((sub, D_MODEL), _F32)
        for c in range(D_FF // FF_CHUNK):
            cs = slice(c * FF_CHUNK, (c + 1) * FF_CHUNK)
            a = jnp.dot(hn, w1_ref[:, cs], preferred_element_type=_F32)
            a = jnp.square(jnp.maximum(a, 0.0)).astype(_BF16)
            acc = acc + jnp.dot(a, w2_ref[cs, :], preferred_element_type=_F32)
        o_ref[rs, :] = h1 + _rms(acc, g_ref[3:4, :])


def _out_mlp(h, ya, yb, g, w_out, w1, w2):
    bsz, seq, _ = h.shape
    grid = (bsz, seq // MLP_ROW_TILE)
    row_spec = functools.partial(_row_spec, tile=MLP_ROW_TILE)
    return pl.pallas_call(
        _out_mlp_kernel,
        grid=grid,
        in_specs=[
            row_spec(D_MODEL),
            row_spec(D_MODEL // 2),
            row_spec(D_MODEL // 2),
            _const_spec((4, D_MODEL)),
            _const_spec((D_MODEL, D_MODEL)),
            _const_spec((D_MODEL, D_FF)),
            _const_spec((D_FF, D_MODEL)),
        ],
        out_specs=row_spec(D_MODEL),
        out_shape=jax.ShapeDtypeStruct(h.shape, h.dtype),
        compiler_params=pltpu.CompilerParams(
            dimension_semantics=("arbitrary", "arbitrary"),
            vmem_limit_bytes=VMEM_LIMIT_BYTES),
        name="out_mlp",
    )(h, ya, yb, g, w_out, w1, w2)


def kernel(x, rel_bias_table, norm_g, even_w_in, even_ln_g, even_ln_b, even_spatial_w,
           even_spatial_b, even_pool_w, even_pool_scale, even_w_out, odd_w_in, odd_lambda,
           odd_subln_g, odd_conv_w, odd_w_out, ffn_w1, ffn_w2):
    depth = norm_g.shape[0]
    bf = lambda w: w.astype(_BF16)
    h = x
    for layer in range(depth):
        g = norm_g[layer]
        if layer % 2 == 0:
            e = layer // 2
            ya, yb = _even_mixer(
                h, g[0:1], bf(even_w_in[e]), even_ln_g[e][None], even_ln_b[e][None],
                even_spatial_w[e], even_spatial_b[e].T, bf(even_pool_w[e]),
                even_pool_scale[e][None])
            w_out = even_w_out[e]
        else:
            o = layer // 2
            lambda_init = 0.8 - 0.6 * math.exp(-0.3 * layer)
            q, k, vt, yb = _odd_proj(h, g[0:1], bf(odd_w_in[o]), odd_conv_w[o])
            ya = _diff_attention(q, k, vt, rel_bias_table, odd_lambda[o],
                                 odd_subln_g[o][:, None], lambda_init)
            w_out = odd_w_out[o]
        h = _out_mlp(h, ya, yb, g, bf(w_out), bf(ffn_w1[layer]), bf(ffn_w2[layer]))
    return h
```

```python
import functools
import math

import numpy as np
import jax
import jax.numpy as jnp
from jax import lax
from jax.experimental import pallas as pl
from jax.experimental.pallas import tpu as pltpu

D_MODEL = 1024
A_WIDTH = 512
A_GROUPS = 4
A_GROUP_DIM = 128
CHUNK = 128
B_WIDTH = 512
POOL_WINDOWS = (2, 4, 8, 16)
B_GROUP_DIM = 128
C_HEADS = 4
C_QK_DIM = 64
C_V_DIM = 128
C_WIDTH = 512
D_WIDTH = 512
CONV_WIDTH = 3
REL_BUCKETS = 32
REL_MAX_DIST = 128
D_FF = 4096
EPS = 1e-6
EVEN_IN = 2 * A_WIDTH + B_WIDTH
ODD_IN = 2 * C_WIDTH + C_WIDTH + 3 * D_WIDTH

ROW_TILE = 512
ATT_TILE = 256
ATT_TILES_PER_STEP = 4
FF_CHUNK = 1024
PROJ_SUBTILES = 2
MLP_ROW_TILE = 1024
MLP_SUBTILES = 4
POOL_HALO = 16
CONV_HALO = 8
LOG2E = math.log2(math.e)
SCORE_LOOKAHEAD = 1
SUM_ROWS = 16
MASK_VALUE = -1e30
VMEM_LIMIT_BYTES = 52 * 1024 * 1024

_F32 = jnp.float32
_BF16 = jnp.bfloat16


def _rms(x, g):
    return x * lax.rsqrt(jnp.mean(x * x, axis=-1, keepdims=True) + EPS) * g


def _gelu_tanh(x):
    c = math.sqrt(2.0 / math.pi)
    hx = 0.5 * x
    return hx + hx * jnp.tanh(x * (c + (c * 0.044715) * (x * x)))


def _norm_project(x_ref, g_ref, w_ref):
    rows = x_ref.shape[0] // PROJ_SUBTILES
    parts = []
    for r in range(PROJ_SUBTILES):
        hn = _rms(x_ref[r * rows:(r + 1) * rows, :], g_ref[...]).astype(_BF16)
        parts.append(jnp.dot(hn, w_ref[...], preferred_element_type=_F32))
    return jnp.concatenate(parts, axis=0)


def _const_spec(shape):
    nd = len(shape)
    return pl.BlockSpec(shape, lambda *_: (0,) * nd, pipeline_mode=pl.Buffered(1))


def _row_spec(width, col=0, tile=None):
    return pl.BlockSpec((None, tile or ROW_TILE, width), lambda b, j: (b, j, col))


def _even_mixer_kernel(x_ref, g_ref, w_in_ref, ln_g_ref, ln_b_ref, ws_ref, bs_ref,
                       pw_ref, ps_ref, ya_ref, yb_ref, carry_ref):
    j = pl.program_id(1)
    proj = _norm_project(x_ref, g_ref, w_in_ref)

    z = _gelu_tanh(proj[:, :2 * A_WIDTH])
    u = z[:, :A_WIDTH]
    v = z[:, A_WIDTH:]
    mu = jnp.mean(v, axis=-1, keepdims=True)
    vc = v - mu
    vn = vc * lax.rsqrt(jnp.mean(vc * vc, axis=-1, keepdims=True) + EPS)
    vn = (vn * ln_g_ref[...] + ln_b_ref[...]).astype(_BF16)
    row = lax.broadcasted_iota(jnp.int32, (CHUNK, CHUNK), 0)
    col = lax.broadcasted_iota(jnp.int32, (CHUNK, CHUNK), 1)
    for g in range(A_GROUPS):
        cs = slice(g * A_GROUP_DIM, (g + 1) * A_GROUP_DIM)
        w = jnp.where(row >= col, ws_ref[g], 0.0).astype(_BF16)
        bias = bs_ref[:, g:g + 1]
        for c in range(0, ROW_TILE // CHUNK, 2):
            r0 = slice(c * CHUNK, (c + 1) * CHUNK)
            r1 = slice((c + 1) * CHUNK, (c + 2) * CHUNK)
            rhs = jnp.concatenate([vn[r0, cs], vn[r1, cs]], axis=1)
            mixed = jnp.dot(w, rhs, preferred_element_type=_F32) + bias
            ya_ref[r0, cs] = (u[r0, cs] * mixed[:, :A_GROUP_DIM]).astype(_BF16)
            ya_ref[r1, cs] = (u[r1, cs] * mixed[:, A_GROUP_DIM:]).astype(_BF16)

    p = proj[:, 2 * A_WIDTH:]

    @pl.when(j == 0)
    def _():
        carry_ref[...] = jnp.zeros_like(carry_ref)

    cur = jnp.concatenate([carry_ref[...], p], axis=0)
    carry_ref[...] = p[ROW_TILE - POOL_HALO:, :]
    cur_win = 1
    pos = j * ROW_TILE + lax.broadcasted_iota(jnp.int32, (ROW_TILE, B_GROUP_DIM), 0)
    tokens_so_far = (pos + 1).astype(_F32)
    for g, win in enumerate(POOL_WINDOWS):
        cs = slice(g * B_GROUP_DIM, (g + 1) * B_GROUP_DIM)
        while cur_win < win:
            cur = cur + pltpu.roll(cur, cur_win, axis=0)
            cur_win *= 2
        assert cur_win == win and win <= POOL_HALO
        count = jnp.minimum(tokens_so_far, float(win))
        pooled = (cur[POOL_HALO:, :B_GROUP_DIM] / count - p[:, cs]).astype(_BF16)
        y = jnp.dot(pooled, pw_ref[g], preferred_element_type=_F32)
        yb_ref[:, cs] = (y * ps_ref[:, cs]).astype(_BF16)
        cur = cur[:, B_GROUP_DIM:]


def _even_mixer(h, g0, w_in, ln_g, ln_b, w_s, b_s_t, pool_w, pool_scale):
    bsz, seq, _ = h.shape
    grid = (bsz, seq // ROW_TILE)
    out = jax.ShapeDtypeStruct((bsz, seq, A_WIDTH), _BF16)
    return pl.pallas_call(
        _even_mixer_kernel,
        grid=grid,
        in_specs=[
            _row_spec(D_MODEL),
            _const_spec((1, D_MODEL)),
            _const_spec((D_MODEL, EVEN_IN)),
            _const_spec((1, A_WIDTH)),
            _const_spec((1, A_WIDTH)),
            _const_spec((A_GROUPS, CHUNK, CHUNK)),
            _const_spec((CHUNK, A_GROUPS)),
            _const_spec((len(POOL_WINDOWS), B_GROUP_DIM, B_GROUP_DIM)),
            _const_spec((1, B_WIDTH)),
        ],
        out_specs=[_row_spec(A_WIDTH), _row_spec(B_WIDTH)],
        out_shape=[out, out],
        scratch_shapes=[
            pltpu.VMEM((POOL_HALO, B_WIDTH), _F32),
        ],
        compiler_params=pltpu.CompilerParams(
            dimension_semantics=("arbitrary", "arbitrary"),
            vmem_limit_bytes=VMEM_LIMIT_BYTES),
        name="even_mixer",
    )(h, g0, w_in, ln_g, ln_b, w_s, b_s_t, pool_w, pool_scale)


def _odd_proj_kernel(x_ref, g_ref, w_in_ref, cw_ref, q_ref, k_ref, vt_ref, yd_ref,
                     carry_ref, ext_ref):
    j = pl.program_id(1)
    proj = _norm_project(x_ref, g_ref, w_in_ref)
    q_ref[...] = (proj[:, :C_WIDTH] * (C_QK_DIM ** -0.5 * LOG2E)).astype(_BF16)
    k_ref[...] = proj[:, C_WIDTH:2 * C_WIDTH].astype(_BF16)
    vt_ref[...] = proj[:, 2 * C_WIDTH:3 * C_WIDTH].T.astype(_BF16)
    o = 3 * C_WIDTH
    bg = proj[:, o:o + D_WIDTH]
    z = proj[:, o + D_WIDTH:o + 2 * D_WIDTH] * proj[:, o + 2 * D_WIDTH:]

    @pl.when(j == 0)
    def _():
        carry_ref[...] = jnp.zeros_like(carry_ref)

    ext_ref[0:CONV_HALO, :] = carry_ref[...]
    ext_ref[CONV_HALO:, :] = z
    carry_ref[...] = z[ROW_TILE - CONV_HALO:, :]
    y = cw_ref[CONV_WIDTH - 1:CONV_WIDTH, :] * z
    for t in range(CONV_WIDTH - 1):
        shift = CONV_WIDTH - 1 - t
        y = y + cw_ref[t:t + 1, :] * ext_ref[CONV_HALO - shift:CONV_HALO - shift + ROW_TILE, :]
    yd_ref[...] = (bg * y).astype(_BF16)


def _odd_proj(h, g0, w_in, conv_w):
    bsz, seq, _ = h.shape
    grid = (bsz, seq // ROW_TILE)
    out = jax.ShapeDtypeStruct((bsz, seq, C_WIDTH), _BF16)
    out_t = jax.ShapeDtypeStruct((bsz, C_WIDTH, seq), _BF16)
    spec_t = pl.BlockSpec((None, C_WIDTH, ROW_TILE), lambda b, j: (b, 0, j))
    return pl.pallas_call(
        _odd_proj_kernel,
        grid=grid,
        in_specs=[
            _row_spec(D_MODEL),
            _const_spec((1, D_MODEL)),
            _const_spec((D_MODEL, ODD_IN)),
            _const_spec((CONV_WIDTH, D_WIDTH)),
        ],
        out_specs=[_row_spec(C_WIDTH), _row_spec(C_WIDTH), spec_t, _row_spec(D_WIDTH)],
        out_shape=[out, out, out_t, out],
        scratch_shapes=[
            pltpu.VMEM((CONV_HALO, D_WIDTH), _F32),
            pltpu.VMEM((ROW_TILE + CONV_HALO, D_WIDTH), _F32),
        ],
        compiler_params=pltpu.CompilerParams(
            dimension_semantics=("arbitrary", "arbitrary"),
            vmem_limit_bytes=VMEM_LIMIT_BYTES),
        name="odd_proj",
    )(h, g0, w_in, conv_w)


def _t5_bucket_upper_bounds(max_dist):
    d = np.arange(max_dist, dtype=np.int32)
    max_exact = REL_BUCKETS // 2
    nf = np.maximum(d, 1).astype(np.float32)
    large = max_exact + (np.log(nf / np.float32(max_exact))
                         / np.float32(math.log(REL_MAX_DIST / max_exact))
                         * np.float32(REL_BUCKETS - max_exact)).astype(np.int32)
    large = np.minimum(large, REL_BUCKETS - 1)
    bucket = np.where(d < max_exact, d, large)
    assert np.all(np.diff(bucket) >= 0)
    return bucket, {int(b): int(d[bucket == b].max()) for b in np.unique(bucket)}


def _attn_kernel(bucket_hi, lambda_init,
                 tab_ref, q_ref, k_ref, vt_ref, lam_ref, sg_ref, o_ref,
                 qq_ref, bias_ref, *scratch):
    t = ATT_TILE
    blk = pl.program_id(1)
    last_bucket = REL_BUCKETS - 1

    @pl.when(blk == 0)
    def _():
        kpos = lax.broadcasted_iota(jnp.int32, (t, t), 0)
        qpos = lax.broadcasted_iota(jnp.int32, (t, t), 1)
        for delta in range(2):
            d = qpos - kpos + delta * t
            for hd in range(C_HEADS):
                far = tab_ref[last_bucket, hd]
                val = jnp.zeros((t, t), _F32)
                for b in sorted(bucket_hi, reverse=True):
                    if b == last_bucket:
                        continue
                    val = jnp.where(d <= bucket_hi[b], (tab_ref[b, hd] - far) * LOG2E, val)
                val = jnp.where(d >= 0, val, MASK_VALUE)
                bias_ref[delta, hd] = jnp.concatenate([val, val], axis=1)

    def query_tile(sub, carry):
        rows = pl.ds(pl.multiple_of(sub * t, t), t)
        _attn_query_tile(lambda_init, blk * ATT_TILES_PER_STEP + sub,
                         q_ref.at[rows], k_ref, vt_ref, lam_ref, sg_ref, o_ref.at[rows],
                         qq_ref, bias_ref, *scratch)
        return carry

    lax.fori_loop(0, ATT_TILES_PER_STEP, query_tile, 0)


def _attn_query_tile(lambda_init, i, q_ref, k_ref, vt_ref, lam_ref, sg_ref, o_ref,
                     qq_ref, bias_ref, m_ref, acc_ref, s_ref, p_ref, al_ref):
    t = ATT_TILE
    hd_dim = 2 * C_QK_DIM
    heads = range(C_HEADS)

    feat = lax.broadcasted_iota(jnp.int32, (hd_dim, t), 0)
    for hd in heads:
        qt = q_ref[:, hd * hd_dim:(hd + 1) * hd_dim].astype(_F32).T
        qq_ref[hd] = jnp.concatenate(
            [jnp.where(feat < C_QK_DIM, qt, 0.0), jnp.where(feat >= C_QK_DIM, qt, 0.0)],
            axis=1).astype(_BF16)
    m_ref[...] = jnp.full_like(m_ref, MASK_VALUE)
    acc_ref[...] = jnp.zeros_like(acc_ref)
    p_ref[C_HEADS - 1] = jnp.zeros((t, 2 * t), _BF16)
    al_ref[C_HEADS - 1] = jnp.ones((1, 2 * t), _F32)

    def scores(jk, hd):
        start = pl.multiple_of(jk * t, t)
        k_t = k_ref[pl.ds(start, t), hd * hd_dim:(hd + 1) * hd_dim]
        s_ref[hd] = jnp.dot(k_t, qq_ref[hd], preferred_element_type=_F32)

    def values(jk, hd):
        start = pl.multiple_of(jk * t, t)
        vt = vt_ref[hd * C_V_DIM:(hd + 1) * C_V_DIM, pl.ds(start, t)]
        vt_ones = jnp.concatenate([vt, jnp.ones((SUM_ROWS, t), _BF16)], axis=0)
        acc_ref[hd] = al_ref[hd] * acc_ref[hd] + jnp.dot(
            vt_ones, p_ref[hd], preferred_element_type=_F32)

    def step(jk, delta, jk_next):
        for hd in heads:
            ahead = hd + SCORE_LOOKAHEAD
            if ahead < C_HEADS:
                scores(jk, ahead)
            elif jk_next is not None:
                scores(jk_next, ahead - C_HEADS)
            s = s_ref[hd]
            if delta is not None:
                s = s + bias_ref[delta, hd]
            m_prev = m_ref[hd]
            m_next = jnp.maximum(m_prev, jnp.max(s, axis=0, keepdims=True))
            alpha = jnp.exp2(m_prev - m_next)
            p = jnp.exp2(s - m_next)
            m_ref[hd] = m_next
            if hd == 0:
                values(jnp.maximum(jk - 1, 0), C_HEADS - 1)
            else:
                values(jk, hd - 1)
            p_ref[hd] = p.astype(_BF16)
            al_ref[hd] = alpha

    for hd in range(SCORE_LOOKAHEAD):
        scores(0, hd)

    n_far = jnp.maximum(i - 1, 0)

    def far_pair(r, carry):
        step(2 * r, None, 2 * r + 1)
        step(2 * r + 1, None, 2 * r + 2)
        return carry

    lax.fori_loop(0, lax.shift_right_logical(n_far, 1), far_pair, 0)

    def finish(odd_far_tile, sub_diagonal):
        if odd_far_tile:
            step(n_far - 1, None, n_far)
        if sub_diagonal:
            step(i - 1, 1, i)
        step(i, 0, None)
        values(i, C_HEADS - 1)
        lp = lam_ref[...]
        lam = (jnp.exp(jnp.sum(lp[0:1] * lp[1:2], axis=-1, keepdims=True))
               - jnp.exp(jnp.sum(lp[2:3] * lp[3:4], axis=-1, keepdims=True)) + lambda_init)
        for hd in heads:
            o = acc_ref[hd, :C_V_DIM, :] / acc_ref[hd, C_V_DIM:C_V_DIM + 1, :]
            a = o[:, :t] - lam * o[:, t:]
            ms = jnp.mean(a * a, axis=0, keepdims=True)
            y = a * lax.rsqrt(ms + EPS) * sg_ref[...] * (1.0 - lambda_init)
            o_ref[:, hd * C_V_DIM:(hd + 1) * C_V_DIM] = y.T.astype(_BF16)

    odd_far = n_far % 2 == 1
    pl.when(i == 0)(lambda: finish(False, False))
    pl.when(jnp.logical_and(i >= 1, jnp.logical_not(odd_far)))(lambda: finish(False, True))
    pl.when(odd_far)(lambda: finish(True, True))


def _diff_attention(q, k, vt, rel_table, lam_params, subln_g, lambda_init):
    bsz, seq, _ = q.shape
    t = ATT_TILE
    bucket, bucket_hi = _t5_bucket_upper_bounds(seq)
    assert np.all(bucket[t + 1:] == REL_BUCKETS - 1)
    rows = t * ATT_TILES_PER_STEP
    grid = (bsz, seq // rows)
    kernel = functools.partial(_attn_kernel, bucket_hi, lambda_init)
    return pl.pallas_call(
        kernel,
        grid=grid,
        in_specs=[
            pl.BlockSpec(memory_space=pltpu.SMEM),
            pl.BlockSpec((None, rows, C_WIDTH), lambda b, i: (b, i, 0)),
            pl.BlockSpec((None, seq, C_WIDTH), lambda b, i: (b, 0, 0)),
            pl.BlockSpec((None, C_WIDTH, seq), lambda b, i: (b, 0, 0)),
            pl.BlockSpec((4, C_QK_DIM), lambda b, i: (0, 0)),
            pl.BlockSpec((C_V_DIM, 1), lambda b, i: (0, 0)),
        ],
        out_specs=pl.BlockSpec((None, rows, C_WIDTH), lambda b, i: (b, i, 0)),
        out_shape=jax.ShapeDtypeStruct((bsz, seq, C_WIDTH), _BF16),
        scratch_shapes=[
            pltpu.VMEM((C_HEADS, 2 * C_QK_DIM, 2 * t), _BF16),
            pltpu.VMEM((2, C_HEADS, t, 2 * t), _F32),
            pltpu.VMEM((C_HEADS, 1, 2 * t), _F32),
            pltpu.VMEM((C_HEADS, C_V_DIM + SUM_ROWS, 2 * t), _F32),
            pltpu.VMEM((C_HEADS, t, 2 * t), _F32),
            pltpu.VMEM((C_HEADS, t, 2 * t), _BF16),
            pltpu.VMEM((C_HEADS, 1, 2 * t), _F32),
        ],
        compiler_params=pltpu.CompilerParams(
            dimension_semantics=("arbitrary", "arbitrary"),
            vmem_limit_bytes=VMEM_LIMIT_BYTES),
        name="diff_attention",
    )(rel_table, q, k, vt, lam_params, subln_g)


def _out_mlp_kernel(h_ref, ya_ref, yb_ref, g_ref, w_out_ref, w1_ref, w2_ref, o_ref):
    half = D_MODEL // 2
    sub = MLP_ROW_TILE // MLP_SUBTILES
    rows = [slice(r * sub, (r + 1) * sub) for r in range(MLP_SUBTILES)]
    ys = [jnp.dot(ya_ref[rs, :], w_out_ref[:half, :], preferred_element_type=_F32)
          + jnp.dot(yb_ref[rs, :], w_out_ref[half:, :], preferred_element_type=_F32)
          for rs in rows]
    h1s = [h_ref[rs, :] + _rms(y, g_ref[1:2, :]) for rs, y in zip(rows, ys)]
    hns = [_rms(h1, g_ref[2:3, :]).astype(_BF16) for h1 in h1s]
    for rs, h1, hn in zip(rows, h1s, hns):
        acc = jnp.zeros((sub, D_MODEL), _F32)
        for c in range(D_FF // FF_CHUNK):
            cs = slice(c * FF_CHUNK, (c + 1) * FF_CHUNK)
            a = jnp.dot(hn, w1_ref[:, cs], preferred_element_type=_F32)
            a = jnp.square(jnp.maximum(a, 0.0)).astype(_BF16)
            acc = acc + jnp.dot(a, w2_ref[cs, :], preferred_element_type=_F32)
        o_ref[rs, :] = h1 + _rms(acc, g_ref[3:4, :])


def _out_mlp(h, ya, yb, g, w_out, w1, w2):
    bsz, seq, _ = h.shape
    grid = (bsz, seq // MLP_ROW_TILE)
    row_spec = functools.partial(_row_spec, tile=MLP_ROW_TILE)
    return pl.pallas_call(
        _out_mlp_kernel,
        grid=grid,
        in_specs=[
            row_spec(D_MODEL),
            row_spec(D_MODEL // 2),
            row_spec(D_MODEL // 2),
            _const_spec((4, D_MODEL)),
            _const_spec((D_MODEL, D_MODEL)),
            _const_spec((D_MODEL, D_FF)),
            _const_spec((D_FF, D_MODEL)),
        ],
        out_specs=row_spec(D_MODEL),
        out_shape=jax.ShapeDtypeStruct(h.shape, h.dtype),
        compiler_params=pltpu.CompilerParams(
            dimension_semantics=("arbitrary", "arbitrary"),
            vmem_limit_bytes=VMEM_LIMIT_BYTES),
        name="out_mlp",
    )(h, ya, yb, g, w_out, w1, w2)


def kernel(x, rel_bias_table, norm_g, even_w_in, even_ln_g, even_ln_b, even_spatial_w,
           even_spatial_b, even_pool_w, even_pool_scale, even_w_out, odd_w_in, odd_lambda,
           odd_subln_g, odd_conv_w, odd_w_out, ffn_w1, ffn_w2):
    depth = norm_g.shape[0]
    bf = lambda w: w.astype(_BF16)
    h = x
    for layer in range(depth):
        g = norm_g[layer]
        if layer % 2 == 0:
            e = layer // 2
            ya, yb = _even_mixer(
                h, g[0:1], bf(even_w_in[e]), even_ln_g[e][None], even_ln_b[e][None],
                even_spatial_w[e], even_spatial_b[e].T, bf(even_pool_w[e]),
                even_pool_scale[e][None])
            w_out = even_w_out[e]
        else:
            o = layer // 2
            lambda_init = 0.8 - 0.6 * math.exp(-0.3 * layer)
            q, k, vt, yb = _odd_proj(h, g[0:1], bf(odd_w_in[o]), odd_conv_w[o])
            ya = _diff_attention(q, k, vt, rel_bias_table, odd_lambda[o],
                                 odd_subln_g[o][:, None], lambda_init)
            w_out = odd_w_out[o]
        h = _out_mlp(h, ya, yb, g, bf(w_out), bf(ffn_w1[layer]), bf(ffn_w2[layer]))
    return h
```

```python
import functools
import math

import numpy as np
import jax
import jax.numpy as jnp
from jax import lax
from jax.experimental import pallas as pl
from jax.experimental.pallas import tpu as pltpu

D_MODEL = 1024
A_WIDTH = 512
A_GROUPS = 4
A_GROUP_DIM = 128
CHUNK = 128
B_WIDTH = 512
POOL_WINDOWS = (2, 4, 8, 16)
B_GROUP_DIM = 128
C_HEADS = 4
C_QK_DIM = 64
C_V_DIM = 128
C_WIDTH = 512
D_WIDTH = 512
CONV_WIDTH = 3
REL_BUCKETS = 32
REL_MAX_DIST = 128
D_FF = 4096
EPS = 1e-6
EVEN_IN = 2 * A_WIDTH + B_WIDTH
ODD_IN = 2 * C_WIDTH + C_WIDTH + 3 * D_WIDTH

ROW_TILE = 1024
ATT_TILE = 256
ATT_TILES_PER_STEP = 4
FF_CHUNK = 1024
PROJ_SUBTILES = 2
MLP_ROW_TILE = 1024
MLP_SUBTILES = 4
POOL_HALO = 16
CONV_HALO = 8
LOG2E = math.log2(math.e)
SCORE_LOOKAHEAD = 1
SUM_ROWS = 16
MASK_VALUE = -1e30
VMEM_LIMIT_BYTES = 52 * 1024 * 1024

_F32 = jnp.float32
_BF16 = jnp.bfloat16


def _rms(x, g):
    return x * lax.rsqrt(jnp.mean(x * x, axis=-1, keepdims=True) + EPS) * g


def _gelu_tanh(x):
    c = math.sqrt(2.0 / math.pi)
    hx = 0.5 * x
    return hx + hx * jnp.tanh(x * (c + (c * 0.044715) * (x * x)))


def _norm_project(x_ref, g_ref, w_ref):
    rows = x_ref.shape[0] // PROJ_SUBTILES
    parts = []
    for r in range(PROJ_SUBTILES):
        hn = _rms(x_ref[r * rows:(r + 1) * rows, :], g_ref[...]).astype(_BF16)
        parts.append(jnp.dot(hn, w_ref[...], preferred_element_type=_F32))
    return jnp.concatenate(parts, axis=0)


def _const_spec(shape):
    nd = len(shape)
    return pl.BlockSpec(shape, lambda *_: (0,) * nd, pipeline_mode=pl.Buffered(1))


def _row_spec(width, col=0, tile=None):
    return pl.BlockSpec((None, tile or ROW_TILE, width), lambda b, j: (b, j, col))


def _even_mixer_kernel(x_ref, g_ref, w_in_ref, ln_g_ref, ln_b_ref, ws_ref, bs_ref,
                       pw_ref, ps_ref, ya_ref, yb_ref, carry_ref):
    j = pl.program_id(1)
    proj = _norm_project(x_ref, g_ref, w_in_ref)

    z = _gelu_tanh(proj[:, :2 * A_WIDTH])
    u = z[:, :A_WIDTH]
    v = z[:, A_WIDTH:]
    mu = jnp.mean(v, axis=-1, keepdims=True)
    vc = v - mu
    vn = vc * lax.rsqrt(jnp.mean(vc * vc, axis=-1, keepdims=True) + EPS)
    vn = (vn * ln_g_ref[...] + ln_b_ref[...]).astype(_BF16)
    row = lax.broadcasted_iota(jnp.int32, (CHUNK, CHUNK), 0)
    col = lax.broadcasted_iota(jnp.int32, (CHUNK, CHUNK), 1)
    for g in range(A_GROUPS):
        cs = slice(g * A_GROUP_DIM, (g + 1) * A_GROUP_DIM)
        w = jnp.where(row >= col, ws_ref[g], 0.0).astype(_BF16)
        bias = bs_ref[:, g:g + 1]
        for c in range(0, ROW_TILE // CHUNK, 2):
            r0 = slice(c * CHUNK, (c + 1) * CHUNK)
            r1 = slice((c + 1) * CHUNK, (c + 2) * CHUNK)
            rhs = jnp.concatenate([vn[r0, cs], vn[r1, cs]], axis=1)
            mixed = jnp.dot(w, rhs, preferred_element_type=_F32) + bias
            ya_ref[r0, cs] = (u[r0, cs] * mixed[:, :A_GROUP_DIM]).astype(_BF16)
            ya_ref[r1, cs] = (u[r1, cs] * mixed[:, A_GROUP_DIM:]).astype(_BF16)

    p = proj[:, 2 * A_WIDTH:]

    @pl.when(j == 0)
    def _():
        carry_ref[...] = jnp.zeros_like(carry_ref)

    cur = jnp.concatenate([carry_ref[...], p], axis=0)
    carry_ref[...] = p[ROW_TILE - POOL_HALO:, :]
    cur_win = 1
    pos = j * ROW_TILE + lax.broadcasted_iota(jnp.int32, (ROW_TILE, B_GROUP_DIM), 0)
    tokens_so_far = (pos + 1).astype(_F32)
    for g, win in enumerate(POOL_WINDOWS):
        cs = slice(g * B_GROUP_DIM, (g + 1) * B_GROUP_DIM)
        while cur_win < win:
            cur = cur + pltpu.roll(cur, cur_win, axis=0)
            cur_win *= 2
        assert cur_win == win and win <= POOL_HALO
        count = jnp.minimum(tokens_so_far, float(win))
        pooled = (cur[POOL_HALO:, :B_GROUP_DIM] / count - p[:, cs]).astype(_BF16)
        y = jnp.dot(pooled, pw_ref[g], preferred_element_type=_F32)
        yb_ref[:, cs] = (y * ps_ref[:, cs]).astype(_BF16)
        cur = cur[:, B_GROUP_DIM:]


def _even_mixer(h, g0, w_in, ln_g, ln_b, w_s, b_s_t, pool_w, pool_scale):
    bsz, seq, _ = h.shape
    grid = (bsz, seq // ROW_TILE)
    out = jax.ShapeDtypeStruct((bsz, seq, A_WIDTH), _BF16)
    return pl.pallas_call(
        _even_mixer_kernel,
        grid=grid,
        in_specs=[
            _row_spec(D_MODEL),
            _const_spec((1, D_MODEL)),
            _const_spec((D_MODEL, EVEN_IN)),
            _const_spec((1, A_WIDTH)),
            _const_spec((1, A_WIDTH)),
            _const_spec((A_GROUPS, CHUNK, CHUNK)),
            _const_spec((CHUNK, A_GROUPS)),
            _const_spec((len(POOL_WINDOWS), B_GROUP_DIM, B_GROUP_DIM)),
            _const_spec((1, B_WIDTH)),
        ],
        out_specs=[_row_spec(A_WIDTH), _row_spec(B_WIDTH)],
        out_shape=[out, out],
        scratch_shapes=[
            pltpu.VMEM((POOL_HALO, B_WIDTH), _F32),
        ],
        compiler_params=pltpu.CompilerParams(
            dimension_semantics=("arbitrary", "arbitrary"),
            vmem_limit_bytes=VMEM_LIMIT_BYTES),
        name="even_mixer",
    )(h, g0, w_in, ln_g, ln_b, w_s, b_s_t, pool_w, pool_scale)


def _odd_proj_kernel(x_ref, g_ref, w_in_ref, cw_ref, q_ref, k_ref, vt_ref, yd_ref,
                     carry_ref, ext_ref):
    j = pl.program_id(1)
    proj = _norm_project(x_ref, g_ref, w_in_ref)
    q_ref[...] = (proj[:, :C_WIDTH] * (C_QK_DIM ** -0.5 * LOG2E)).astype(_BF16)
    k_ref[...] = proj[:, C_WIDTH:2 * C_WIDTH].astype(_BF16)
    vt_ref[...] = proj[:, 2 * C_WIDTH:3 * C_WIDTH].T.astype(_BF16)
    o = 3 * C_WIDTH
    bg = proj[:, o:o + D_WIDTH]
    z = proj[:, o + D_WIDTH:o + 2 * D_WIDTH] * proj[:, o + 2 * D_WIDTH:]

    @pl.when(j == 0)
    def _():
        carry_ref[...] = jnp.zeros_like(carry_ref)

    ext_ref[0:CONV_HALO, :] = carry_ref[...]
    ext_ref[CONV_HALO:, :] = z
    carry_ref[...] = z[ROW_TILE - CONV_HALO:, :]
    y = cw_ref[CONV_WIDTH - 1:CONV_WIDTH, :] * z
    for t in range(CONV_WIDTH - 1):
        shift = CONV_WIDTH - 1 - t
        y = y + cw_ref[t:t + 1, :] * ext_ref[CONV_HALO - shift:CONV_HALO - shift + ROW_TILE, :]
    yd_ref[...] = (bg * y).astype(_BF16)


def _odd_proj(h, g0, w_in, conv_w):
    bsz, seq, _ = h.shape
    grid = (bsz, seq // ROW_TILE)
    out = jax.ShapeDtypeStruct((bsz, seq, C_WIDTH), _BF16)
    out_t = jax.ShapeDtypeStruct((bsz, C_WIDTH, seq), _BF16)
    spec_t = pl.BlockSpec((None, C_WIDTH, ROW_TILE), lambda b, j: (b, 0, j))
    return pl.pallas_call(
        _odd_proj_kernel,
        grid=grid,
        in_specs=[
            _row_spec(D_MODEL),
            _const_spec((1, D_MODEL)),
            _const_spec((D_MODEL, ODD_IN)),
            _const_spec((CONV_WIDTH, D_WIDTH)),
        ],
        out_specs=[_row_spec(C_WIDTH), _row_spec(C_WIDTH), spec_t, _row_spec(D_WIDTH)],
        out_shape=[out, out, out_t, out],
        scratch_shapes=[
            pltpu.VMEM((CONV_HALO, D_WIDTH), _F32),
            pltpu.VMEM((ROW_TILE + CONV_HALO, D_WIDTH), _F32),
        ],
        compiler_params=pltpu.CompilerParams(
            dimension_semantics=("arbitrary", "arbitrary"),
            vmem_limit_bytes=VMEM_LIMIT_BYTES),
        name="odd_proj",
    )(h, g0, w_in, conv_w)


def _t5_bucket_upper_bounds(max_dist):
    d = np.arange(max_dist, dtype=np.int32)
    max_exact = REL_BUCKETS // 2
    nf = np.maximum(d, 1).astype(np.float32)
    large = max_exact + (np.log(nf / np.float32(max_exact))
                         / np.float32(math.log(REL_MAX_DIST / max_exact))
                         * np.float32(REL_BUCKETS - max_exact)).astype(np.int32)
    large = np.minimum(large, REL_BUCKETS - 1)
    bucket = np.where(d < max_exact, d, large)
    assert np.all(np.diff(bucket) >= 0)
    return bucket, {int(b): int(d[bucket == b].max()) for b in np.unique(bucket)}


def _attn_kernel(bucket_hi, lambda_init,
                 tab_ref, q_ref, k_ref, vt_ref, lam_ref, sg_ref, o_ref,
                 qq_ref, bias_ref, *scratch):
    t = ATT_TILE
    blk = pl.program_id(1)
    last_bucket = REL_BUCKETS - 1

    @pl.when(blk == 0)
    def _():
        kpos = lax.broadcasted_iota(jnp.int32, (t, t), 0)
        qpos = lax.broadcasted_iota(jnp.int32, (t, t), 1)
        for delta in range(2):
            d = qpos - kpos + delta * t
            for hd in range(C_HEADS):
                far = tab_ref[last_bucket, hd]
                val = jnp.zeros((t, t), _F32)
                for b in sorted(bucket_hi, reverse=True):
                    if b == last_bucket:
                        continue
                    val = jnp.where(d <= bucket_hi[b], (tab_ref[b, hd] - far) * LOG2E, val)
                val = jnp.where(d >= 0, val, MASK_VALUE)
                bias_ref[delta, hd] = jnp.concatenate([val, val], axis=1)

    def query_tile(sub, carry):
        rows = pl.ds(pl.multiple_of(sub * t, t), t)
        _attn_query_tile(lambda_init, blk * ATT_TILES_PER_STEP + sub,
                         q_ref.at[rows], k_ref, vt_ref, lam_ref, sg_ref, o_ref.at[rows],
                         qq_ref, bias_ref, *scratch)
        return carry

    lax.fori_loop(0, ATT_TILES_PER_STEP, query_tile, 0)


def _attn_query_tile(lambda_init, i, q_ref, k_ref, vt_ref, lam_ref, sg_ref, o_ref,
                     qq_ref, bias_ref, m_ref, acc_ref, s_ref, p_ref, al_ref):
    t = ATT_TILE
    hd_dim = 2 * C_QK_DIM
    heads = range(C_HEADS)

    feat = lax.broadcasted_iota(jnp.int32, (hd_dim, t), 0)
    for hd in heads:
        qt = q_ref[:, hd * hd_dim:(hd + 1) * hd_dim].astype(_F32).T
        qq_ref[hd] = jnp.concatenate(
            [jnp.where(feat < C_QK_DIM, qt, 0.0), jnp.where(feat >= C_QK_DIM, qt, 0.0)],
            axis=1).astype(_BF16)
    m_ref[...] = jnp.full_like(m_ref, MASK_VALUE)
    acc_ref[...] = jnp.zeros_like(acc_ref)
    p_ref[C_HEADS - 1] = jnp.zeros((t, 2 * t), _BF16)
    al_ref[C_HEADS - 1] = jnp.ones((1, 2 * t), _F32)

    def scores(jk, hd):
        start = pl.multiple_of(jk * t, t)
        k_t = k_ref[pl.ds(start, t), hd * hd_dim:(hd + 1) * hd_dim]
        s_ref[hd] = jnp.dot(k_t, qq_ref[hd], preferred_element_type=_F32)

    def values(jk, hd):
        start = pl.multiple_of(jk * t, t)
        vt = vt_ref[hd * C_V_DIM:(hd + 1) * C_V_DIM, pl.ds(start, t)]
        vt_ones = jnp.concatenate([vt, jnp.ones((SUM_ROWS, t), _BF16)], axis=0)
        acc_ref[hd] = al_ref[hd] * acc_ref[hd] + jnp.dot(
            vt_ones, p_ref[hd], preferred_element_type=_F32)

    def step(jk, delta, jk_next):
        for hd in heads:
            ahead = hd + SCORE_LOOKAHEAD
            if ahead < C_HEADS:
                scores(jk, ahead)
            elif jk_next is not None:
                scores(jk_next, ahead - C_HEADS)
            s = s_ref[hd]
            if delta is not None:
                s = s + bias_ref[delta, hd]
            m_prev = m_ref[hd]
            m_next = jnp.maximum(m_prev, jnp.max(s, axis=0, keepdims=True))
            alpha = jnp.exp2(m_prev - m_next)
            p = jnp.exp2(s - m_next)
            m_ref[hd] = m_next
            if hd == 0:
                values(jnp.maximum(jk - 1, 0), C_HEADS - 1)
            else:
                values(jk, hd - 1)
            p_ref[hd] = p.astype(_BF16)
            al_ref[hd] = alpha

    for hd in range(SCORE_LOOKAHEAD):
        scores(0, hd)

    n_far = jnp.maximum(i - 1, 0)

    def far_pair(r, carry):
        step(2 * r, None, 2 * r + 1)
        step(2 * r + 1, None, 2 * r + 2)
        return carry

    lax.fori_loop(0, lax.shift_right_logical(n_far, 1), far_pair, 0)

    def finish(odd_far_tile, sub_diagonal):
        if odd_far_tile:
            step(n_far - 1, None, n_far)
        if sub_diagonal:
            step(i - 1, 1, i)
        step(i, 0, None)
        values(i, C_HEADS - 1)
        lp = lam_ref[...]
        lam = (jnp.exp(jnp.sum(lp[0:1] * lp[1:2], axis=-1, keepdims=True))
               - jnp.exp(jnp.sum(lp[2:3] * lp[3:4], axis=-1, keepdims=True)) + lambda_init)
        for hd in heads:
            o = acc_ref[hd, :C_V_DIM, :] / acc_ref[hd, C_V_DIM:C_V_DIM + 1, :]
            a = o[:, :t] - lam * o[:, t:]
            ms = jnp.mean(a * a, axis=0, keepdims=True)
            y = a * lax.rsqrt(ms + EPS) * sg_ref[...] * (1.0 - lambda_init)
            o_ref[:, hd * C_V_DIM:(hd + 1) * C_V_DIM] = y.T.astype(_BF16)

    odd_far = n_far % 2 == 1
    pl.when(i == 0)(lambda: finish(False, False))
    pl.when(jnp.logical_and(i >= 1, jnp.logical_not(odd_far)))(lambda: finish(False, True))
    pl.when(odd_far)(lambda: finish(True, True))


def _diff_attention(q, k, vt, rel_table, lam_params, subln_g, lambda_init):
    bsz, seq, _ = q.shape
    t = ATT_TILE
    bucket, bucket_hi = _t5_bucket_upper_bounds(seq)
    assert np.all(bucket[t + 1:] == REL_BUCKETS - 1)
    rows = t * ATT_TILES_PER_STEP
    grid = (bsz, seq // rows)
    kernel = functools.partial(_attn_kernel, bucket_hi, lambda_init)
    return pl.pallas_call(
        kernel,
        grid=grid,
        in_specs=[
            pl.BlockSpec(memory_space=pltpu.SMEM),
            pl.BlockSpec((None, rows, C_WIDTH), lambda b, i: (b, i, 0)),
            pl.BlockSpec((None, seq, C_WIDTH), lambda b, i: (b, 0, 0)),
            pl.BlockSpec((None, C_WIDTH, seq), lambda b, i: (b, 0, 0)),
            pl.BlockSpec((4, C_QK_DIM), lambda b, i: (0, 0)),
            pl.BlockSpec((C_V_DIM, 1), lambda b, i: (0, 0)),
        ],
        out_specs=pl.BlockSpec((None, rows, C_WIDTH), lambda b, i: (b, i, 0)),
        out_shape=jax.ShapeDtypeStruct((bsz, seq, C_WIDTH), _BF16),
        scratch_shapes=[
            pltpu.VMEM((C_HEADS, 2 * C_QK_DIM, 2 * t), _BF16),
            pltpu.VMEM((2, C_HEADS, t, 2 * t), _F32),
            pltpu.VMEM((C_HEADS, 1, 2 * t), _F32),
            pltpu.VMEM((C_HEADS, C_V_DIM + SUM_ROWS, 2 * t), _F32),
            pltpu.VMEM((C_HEADS, t, 2 * t), _F32),
            pltpu.VMEM((C_HEADS, t, 2 * t), _BF16),
            pltpu.VMEM((C_HEADS, 1, 2 * t), _F32),
        ],
        compiler_params=pltpu.CompilerParams(
            dimension_semantics=("arbitrary", "arbitrary"),
            vmem_limit_bytes=VMEM_LIMIT_BYTES),
        name="diff_attention",
    )(rel_table, q, k, vt, lam_params, subln_g)


def _out_mlp_kernel(h_ref, ya_ref, yb_ref, g_ref, w_out_ref, w1_ref, w2_ref, o_ref):
    half = D_MODEL // 2
    sub = MLP_ROW_TILE // MLP_SUBTILES
    rows = [slice(r * sub, (r + 1) * sub) for r in range(MLP_SUBTILES)]
    ys = [jnp.dot(ya_ref[rs, :], w_out_ref[:half, :], preferred_element_type=_F32)
          + jnp.dot(yb_ref[rs, :], w_out_ref[half:, :], preferred_element_type=_F32)
          for rs in rows]
    h1s = [h_ref[rs, :] + _rms(y, g_ref[1:2, :]) for rs, y in zip(rows, ys)]
    hns = [_rms(h1, g_ref[2:3, :]).astype(_BF16) for h1 in h1s]
    for rs, h1, hn in zip(rows, h1s, hns):
        acc = jnp.zeros((sub, D_MODEL), _F32)
        for c in range(D_FF // FF_CHUNK):
            cs = slice(c * FF_CHUNK, (c + 1) * FF_CHUNK)
            a = jnp.dot(hn, w1_ref[:, cs], preferred_element_type=_F32)
            a = jnp.square(jnp.maximum(a, 0.0)).astype(_BF16)
            acc = acc + jnp.dot(a, w2_ref[cs, :], preferred_element_type=_F32)
        o_ref[rs, :] = h1 + _rms(acc, g_ref[3:4, :])


def _out_mlp(h, ya, yb, g, w_out, w1, w2):
    bsz, seq, _ = h.shape
    grid = (bsz, seq // MLP_ROW_TILE)
    row_spec = functools.partial(_row_spec, tile=MLP_ROW_TILE)
    return pl.pallas_call(
        _out_mlp_kernel,
        grid=grid,
        in_specs=[
            row_spec(D_MODEL),
            row_spec(D_MODEL // 2),
            row_spec(D_MODEL // 2),
            _const_spec((4, D_MODEL)),
            _const_spec((D_MODEL, D_MODEL)),
            _const_spec((D_MODEL, D_FF)),
            _const_spec((D_FF, D_MODEL)),
        ],
        out_specs=row_spec(D_MODEL),
        out_shape=jax.ShapeDtypeStruct(h.shape, h.dtype),
        compiler_params=pltpu.CompilerParams(
            dimension_semantics=("arbitrary", "arbitrary"),
            vmem_limit_bytes=VMEM_LIMIT_BYTES),
        name="out_mlp",
    )(h, ya, yb, g, w_out, w1, w2)


def kernel(x, rel_bias_table, norm_g, even_w_in, even_ln_g, even_ln_b, even_spatial_w,
           even_spatial_b, even_pool_w, even_pool_scale, even_w_out, odd_w_in, odd_lambda,
           odd_subln_g, odd_conv_w, odd_w_out, ffn_w1, ffn_w2):
    depth = norm_g.shape[0]
    bf = lambda w: w.astype(_BF16)
    h = x
    for layer in range(depth):
        g = norm_g[layer]
        if layer % 2 == 0:
            e = layer // 2
            ya, yb = _even_mixer(
                h, g[0:1], bf(even_w_in[e]), even_ln_g[e][None], even_ln_b[e][None],
                even_spatial_w[e], even_spatial_b[e].T, bf(even_pool_w[e]),
                even_pool_scale[e][None])
            w_out = even_w_out[e]
        else:
            o = layer // 2
            lambda_init = 0.8 - 0.6 * math.exp(-0.3 * layer)
            q, k, vt, yb = _odd_proj(h, g[0:1], bf(odd_w_in[o]), odd_conv_w[o])
            ya = _diff_attention(q, k, vt, rel_bias_table, odd_lambda[o],
                                 odd_subln_g[o][:, None], lambda_init)
            w_out = odd_w_out[o]
        h = _out_mlp(h, ya, yb, g, bf(w_out), bf(ffn_w1[layer]), bf(ffn_w2[layer]))
    return h
```

```python
import functools
import math

import numpy as np
import jax
import jax.numpy as jnp
from jax import lax
from jax.experimental import pallas as pl
from jax.experimental.pallas import tpu as pltpu

D_MODEL = 1024
A_WIDTH = 512
A_GROUPS = 4
A_GROUP_DIM = 128
CHUNK = 128
B_WIDTH = 512
POOL_WINDOWS = (2, 4, 8, 16)
B_GROUP_DIM = 128
C_HEADS = 4
C_QK_DIM = 64
C_V_DIM = 128
C_WIDTH = 512
D_WIDTH = 512
CONV_WIDTH = 3
REL_BUCKETS = 32
REL_MAX_DIST = 128
D_FF = 4096
EPS = 1e-6
EVEN_IN = 2 * A_WIDTH + B_WIDTH
ODD_IN = 2 * C_WIDTH + C_WIDTH + 3 * D_WIDTH

ROW_TILE = 1024
ATT_TILE = 256
ATT_TILES_PER_STEP = 4
FF_CHUNK = 1024
PROJ_SUBTILES = 2
MLP_ROW_TILE = 1024
MLP_SUBTILES = 4
POOL_HALO = 16
CONV_HALO = 8
LOG2E = math.log2(math.e)
FAR_UNROLL = 4
SCORE_LOOKAHEAD = 1
SUM_ROWS = 16
MASK_VALUE = -1e30
VMEM_LIMIT_BYTES = 52 * 1024 * 1024

_F32 = jnp.float32
_BF16 = jnp.bfloat16


def _rms(x, g):
    return x * lax.rsqrt(jnp.mean(x * x, axis=-1, keepdims=True) + EPS) * g


def _gelu_tanh(x):
    c = math.sqrt(2.0 / math.pi)
    hx = 0.5 * x
    return hx + hx * jnp.tanh(x * (c + (c * 0.044715) * (x * x)))


def _norm_project(x_ref, g_ref, w_ref):
    rows = x_ref.shape[0] // PROJ_SUBTILES
    parts = []
    for r in range(PROJ_SUBTILES):
        hn = _rms(x_ref[r * rows:(r + 1) * rows, :], g_ref[...]).astype(_BF16)
        parts.append(jnp.dot(hn, w_ref[...], preferred_element_type=_F32))
    return jnp.concatenate(parts, axis=0)


def _const_spec(shape):
    nd = len(shape)
    return pl.BlockSpec(shape, lambda *_: (0,) * nd, pipeline_mode=pl.Buffered(1))


def _row_spec(width, col=0, tile=None):
    return pl.BlockSpec((None, tile or ROW_TILE, width), lambda b, j: (b, j, col))


def _even_mixer_kernel(x_ref, g_ref, w_in_ref, ln_g_ref, ln_b_ref, ws_ref, bs_ref,
                       pw_ref, ps_ref, ya_ref, yb_ref, carry_ref):
    j = pl.program_id(1)
    proj = _norm_project(x_ref, g_ref, w_in_ref)

    z = _gelu_tanh(proj[:, :2 * A_WIDTH])
    u = z[:, :A_WIDTH]
    v = z[:, A_WIDTH:]
    mu = jnp.mean(v, axis=-1, keepdims=True)
    vc = v - mu
    vn = vc * lax.rsqrt(jnp.mean(vc * vc, axis=-1, keepdims=True) + EPS)
    vn = (vn * ln_g_ref[...] + ln_b_ref[...]).astype(_BF16)
    row = lax.broadcasted_iota(jnp.int32, (CHUNK, CHUNK), 0)
    col = lax.broadcasted_iota(jnp.int32, (CHUNK, CHUNK), 1)
    for g in range(A_GROUPS):
        cs = slice(g * A_GROUP_DIM, (g + 1) * A_GROUP_DIM)
        w = jnp.where(row >= col, ws_ref[g], 0.0).astype(_BF16)
        bias = bs_ref[:, g:g + 1]
        for c in range(0, ROW_TILE // CHUNK, 2):
            r0 = slice(c * CHUNK, (c + 1) * CHUNK)
            r1 = slice((c + 1) * CHUNK, (c + 2) * CHUNK)
            rhs = jnp.concatenate([vn[r0, cs], vn[r1, cs]], axis=1)
            mixed = jnp.dot(w, rhs, preferred_element_type=_F32) + bias
            ya_ref[r0, cs] = (u[r0, cs] * mixed[:, :A_GROUP_DIM]).astype(_BF16)
            ya_ref[r1, cs] = (u[r1, cs] * mixed[:, A_GROUP_DIM:]).astype(_BF16)

    p = proj[:, 2 * A_WIDTH:]

    @pl.when(j == 0)
    def _():
        carry_ref[...] = jnp.zeros_like(carry_ref)

    cur = jnp.concatenate([carry_ref[...], p], axis=0)
    carry_ref[...] = p[ROW_TILE - POOL_HALO:, :]
    cur_win = 1
    pos = j * ROW_TILE + lax.broadcasted_iota(jnp.int32, (ROW_TILE, B_GROUP_DIM), 0)
    tokens_so_far = (pos + 1).astype(_F32)
    for g, win in enumerate(POOL_WINDOWS):
        cs = slice(g * B_GROUP_DIM, (g + 1) * B_GROUP_DIM)
        while cur_win < win:
            cur = cur + pltpu.roll(cur, cur_win, axis=0)
            cur_win *= 2
        assert cur_win == win and win <= POOL_HALO
        count = jnp.minimum(tokens_so_far, float(win))
        pooled = (cur[POOL_HALO:, :B_GROUP_DIM] / count - p[:, cs]).astype(_BF16)
        y = jnp.dot(pooled, pw_ref[g], preferred_element_type=_F32)
        yb_ref[:, cs] = (y * ps_ref[:, cs]).astype(_BF16)
        cur = cur[:, B_GROUP_DIM:]


def _even_mixer(h, g0, w_in, ln_g, ln_b, w_s, b_s_t, pool_w, pool_scale):
    bsz, seq, _ = h.shape
    grid = (bsz, seq // ROW_TILE)
    out = jax.ShapeDtypeStruct((bsz, seq, A_WIDTH), _BF16)
    return pl.pallas_call(
        _even_mixer_kernel,
        grid=grid,
        in_specs=[
            _row_spec(D_MODEL),
            _const_spec((1, D_MODEL)),
            _const_spec((D_MODEL, EVEN_IN)),
            _const_spec((1, A_WIDTH)),
            _const_spec((1, A_WIDTH)),
            _const_spec((A_GROUPS, CHUNK, CHUNK)),
            _const_spec((CHUNK, A_GROUPS)),
            _const_spec((len(POOL_WINDOWS), B_GROUP_DIM, B_GROUP_DIM)),
            _const_spec((1, B_WIDTH)),
        ],
        out_specs=[_row_spec(A_WIDTH), _row_spec(B_WIDTH)],
        out_shape=[out, out],
        scratch_shapes=[
            pltpu.VMEM((POOL_HALO, B_WIDTH), _F32),
        ],
        compiler_params=pltpu.CompilerParams(
            dimension_semantics=("arbitrary", "arbitrary"),
            vmem_limit_bytes=VMEM_LIMIT_BYTES),
        name="even_mixer",
    )(h, g0, w_in, ln_g, ln_b, w_s, b_s_t, pool_w, pool_scale)


def _odd_proj_kernel(x_ref, g_ref, w_in_ref, cw_ref, q_ref, k_ref, vt_ref, yd_ref,
                     carry_ref, ext_ref):
    j = pl.program_id(1)
    proj = _norm_project(x_ref, g_ref, w_in_ref)
    q_ref[...] = (proj[:, :C_WIDTH] * (C_QK_DIM ** -0.5 * LOG2E)).astype(_BF16)
    k_ref[...] = proj[:, C_WIDTH:2 * C_WIDTH].astype(_BF16)
    vt_ref[...] = proj[:, 2 * C_WIDTH:3 * C_WIDTH].T.astype(_BF16)
    o = 3 * C_WIDTH
    bg = proj[:, o:o + D_WIDTH]
    z = proj[:, o + D_WIDTH:o + 2 * D_WIDTH] * proj[:, o + 2 * D_WIDTH:]

    @pl.when(j == 0)
    def _():
        carry_ref[...] = jnp.zeros_like(carry_ref)

    ext_ref[0:CONV_HALO, :] = carry_ref[...]
    ext_ref[CONV_HALO:, :] = z
    carry_ref[...] = z[ROW_TILE - CONV_HALO:, :]
    y = cw_ref[CONV_WIDTH - 1:CONV_WIDTH, :] * z
    for t in range(CONV_WIDTH - 1):
        shift = CONV_WIDTH - 1 - t
        y = y + cw_ref[t:t + 1, :] * ext_ref[CONV_HALO - shift:CONV_HALO - shift + ROW_TILE, :]
    yd_ref[...] = (bg * y).astype(_BF16)


def _odd_proj(h, g0, w_in, conv_w):
    bsz, seq, _ = h.shape
    grid = (bsz, seq // ROW_TILE)
    out = jax.ShapeDtypeStruct((bsz, seq, C_WIDTH), _BF16)
    out_t = jax.ShapeDtypeStruct((bsz, C_WIDTH, seq), _BF16)
    spec_t = pl.BlockSpec((None, C_WIDTH, ROW_TILE), lambda b, j: (b, 0, j))
    return pl.pallas_call(
        _odd_proj_kernel,
        grid=grid,
        in_specs=[
            _row_spec(D_MODEL),
            _const_spec((1, D_MODEL)),
            _const_spec((D_MODEL, ODD_IN)),
            _const_spec((CONV_WIDTH, D_WIDTH)),
        ],
        out_specs=[_row_spec(C_WIDTH), _row_spec(C_WIDTH), spec_t, _row_spec(D_WIDTH)],
        out_shape=[out, out, out_t, out],
        scratch_shapes=[
            pltpu.VMEM((CONV_HALO, D_WIDTH), _F32),
            pltpu.VMEM((ROW_TILE + CONV_HALO, D_WIDTH), _F32),
        ],
        compiler_params=pltpu.CompilerParams(
            dimension_semantics=("arbitrary", "arbitrary"),
            vmem_limit_bytes=VMEM_LIMIT_BYTES),
        name="odd_proj",
    )(h, g0, w_in, conv_w)


def _t5_bucket_upper_bounds(max_dist):
    d = np.arange(max_dist, dtype=np.int32)
    max_exact = REL_BUCKETS // 2
    nf = np.maximum(d, 1).astype(np.float32)
    large = max_exact + (np.log(nf / np.float32(max_exact))
                         / np.float32(math.log(REL_MAX_DIST / max_exact))
                         * np.float32(REL_BUCKETS - max_exact)).astype(np.int32)
    large = np.minimum(large, REL_BUCKETS - 1)
    bucket = np.where(d < max_exact, d, large)
    assert np.all(np.diff(bucket) >= 0)
    return bucket, {int(b): int(d[bucket == b].max()) for b in np.unique(bucket)}


def _attn_kernel(bucket_hi, lambda_init,
                 tab_ref, q_ref, k_ref, vt_ref, lam_ref, sg_ref, o_ref,
                 qq_ref, bias_ref, *scratch):
    t = ATT_TILE
    blk = pl.program_id(1)
    last_bucket = REL_BUCKETS - 1

    @pl.when(blk == 0)
    def _():
        kpos = lax.broadcasted_iota(jnp.int32, (t, t), 0)
        qpos = lax.broadcasted_iota(jnp.int32, (t, t), 1)
        for delta in range(2):
            d = qpos - kpos + delta * t
            for hd in range(C_HEADS):
                far = tab_ref[last_bucket, hd]
                val = jnp.zeros((t, t), _F32)
                for b in sorted(bucket_hi, reverse=True):
                    if b == last_bucket:
                        continue
                    val = jnp.where(d <= bucket_hi[b], (tab_ref[b, hd] - far) * LOG2E, val)
                val = jnp.where(d >= 0, val, MASK_VALUE)
                bias_ref[delta, hd] = jnp.concatenate([val, val], axis=1)

    def query_tile(sub, carry):
        rows = pl.ds(pl.multiple_of(sub * t, t), t)
        _attn_query_tile(lambda_init, blk * ATT_TILES_PER_STEP + sub,
                         q_ref.at[rows], k_ref, vt_ref, lam_ref, sg_ref, o_ref.at[rows],
                         qq_ref, bias_ref, *scratch)
        return carry

    lax.fori_loop(0, ATT_TILES_PER_STEP, query_tile, 0)


def _attn_query_tile(lambda_init, i, q_ref, k_ref, vt_ref, lam_ref, sg_ref, o_ref,
                     qq_ref, bias_ref, m_ref, acc_ref, s_ref, p_ref, al_ref):
    t = ATT_TILE
    hd_dim = 2 * C_QK_DIM
    heads = range(C_HEADS)

    feat = lax.broadcasted_iota(jnp.int32, (hd_dim, t), 0)
    for hd in heads:
        qt = q_ref[:, hd * hd_dim:(hd + 1) * hd_dim].astype(_F32).T
        qq_ref[hd] = jnp.concatenate(
            [jnp.where(feat < C_QK_DIM, qt, 0.0), jnp.where(feat >= C_QK_DIM, qt, 0.0)],
            axis=1).astype(_BF16)
    m_ref[...] = jnp.full_like(m_ref, MASK_VALUE)
    acc_ref[...] = jnp.zeros_like(acc_ref)
    p_ref[C_HEADS - 1] = jnp.zeros((t, 2 * t), _BF16)
    al_ref[C_HEADS - 1] = jnp.ones((1, 2 * t), _F32)

    def scores(jk, hd):
        start = pl.multiple_of(jk * t, t)
        k_t = k_ref[pl.ds(start, t), hd * hd_dim:(hd + 1) * hd_dim]
        s_ref[hd] = jnp.dot(k_t, qq_ref[hd], preferred_element_type=_F32)

    def values(jk, hd):
        start = pl.multiple_of(jk * t, t)
        vt = vt_ref[hd * C_V_DIM:(hd + 1) * C_V_DIM, pl.ds(start, t)]
        vt_ones = jnp.concatenate([vt, jnp.ones((SUM_ROWS, t), _BF16)], axis=0)
        acc_ref[hd] = al_ref[hd] * acc_ref[hd] + jnp.dot(
            vt_ones, p_ref[hd], preferred_element_type=_F32)

    def step(jk, delta, jk_next):
        for hd in heads:
            ahead = hd + SCORE_LOOKAHEAD
            if ahead < C_HEADS:
                scores(jk, ahead)
            elif jk_next is not None:
                scores(jk_next, ahead - C_HEADS)
            s = s_ref[hd]
            if delta is not None:
                s = s + bias_ref[delta, hd]
            m_prev = m_ref[hd]
            m_next = jnp.maximum(m_prev, jnp.max(s, axis=0, keepdims=True))
            alpha = jnp.exp2(m_prev - m_next)
            p = jnp.exp2(s - m_next)
            m_ref[hd] = m_next
            if hd == 0:
                values(jnp.maximum(jk - 1, 0), C_HEADS - 1)
            else:
                values(jk, hd - 1)
            p_ref[hd] = p.astype(_BF16)
            al_ref[hd] = alpha

    for hd in range(SCORE_LOOKAHEAD):
        scores(0, hd)

    n_far = jnp.maximum(i - 1, 0)

    def far_run(first, count):
        for n in range(count):
            step(first + n, None, first + n + 1)

    def far_trip(r, carry):
        far_run(FAR_UNROLL * r, FAR_UNROLL)
        return carry

    trips = n_far // FAR_UNROLL
    lax.fori_loop(0, trips, far_trip, 0)
    leftover_pair = (n_far % FAR_UNROLL) // 2 == 1
    pl.when(leftover_pair)(lambda: far_run(FAR_UNROLL * trips, 2))

    def finish(odd_far_tile, sub_diagonal):
        if odd_far_tile:
            step(n_far - 1, None, n_far)
        if sub_diagonal:
            step(i - 1, 1, i)
        step(i, 0, None)
        values(i, C_HEADS - 1)
        lp = lam_ref[...]
        lam = (jnp.exp(jnp.sum(lp[0:1] * lp[1:2], axis=-1, keepdims=True))
               - jnp.exp(jnp.sum(lp[2:3] * lp[3:4], axis=-1, keepdims=True)) + lambda_init)
        for hd in heads:
            inv_l = 1.0 / acc_ref[hd, C_V_DIM:C_V_DIM + 1, :]
            a = (acc_ref[hd, :C_V_DIM, :t] * inv_l[:, :t]
                 - acc_ref[hd, :C_V_DIM, t:] * (lam * inv_l[:, t:]))
            ms = jnp.mean(a * a, axis=0, keepdims=True)
            y = a * (lax.rsqrt(ms + EPS) * (1.0 - lambda_init)) * sg_ref[...]
            o_ref[:, hd * C_V_DIM:(hd + 1) * C_V_DIM] = y.T.astype(_BF16)

    odd_far = n_far % 2 == 1
    pl.when(i == 0)(lambda: finish(False, False))
    pl.when(jnp.logical_and(i >= 1, jnp.logical_not(odd_far)))(lambda: finish(False, True))
    pl.when(odd_far)(lambda: finish(True, True))


def _diff_attention(q, k, vt, rel_table, lam_params, subln_g, lambda_init):
    bsz, seq, _ = q.shape
    t = ATT_TILE
    bucket, bucket_hi = _t5_bucket_upper_bounds(seq)
    assert np.all(bucket[t + 1:] == REL_BUCKETS - 1)
    rows = t * ATT_TILES_PER_STEP
    grid = (bsz, seq // rows)
    kernel = functools.partial(_attn_kernel, bucket_hi, lambda_init)
    return pl.pallas_call(
        kernel,
        grid=grid,
        in_specs=[
            pl.BlockSpec(memory_space=pltpu.SMEM),
            pl.BlockSpec((None, rows, C_WIDTH), lambda b, i: (b, i, 0)),
            pl.BlockSpec((None, seq, C_WIDTH), lambda b, i: (b, 0, 0)),
            pl.BlockSpec((None, C_WIDTH, seq), lambda b, i: (b, 0, 0)),
            pl.BlockSpec((4, C_QK_DIM), lambda b, i: (0, 0)),
            pl.BlockSpec((C_V_DIM, 1), lambda b, i: (0, 0)),
        ],
        out_specs=pl.BlockSpec((None, rows, C_WIDTH), lambda b, i: (b, i, 0)),
        out_shape=jax.ShapeDtypeStruct((bsz, seq, C_WIDTH), _BF16),
        scratch_shapes=[
            pltpu.VMEM((C_HEADS, 2 * C_QK_DIM, 2 * t), _BF16),
            pltpu.VMEM((2, C_HEADS, t, 2 * t), _F32),
            pltpu.VMEM((C_HEADS, 1, 2 * t), _F32),
            pltpu.VMEM((C_HEADS, C_V_DIM + SUM_ROWS, 2 * t), _F32),
            pltpu.VMEM((C_HEADS, t, 2 * t), _F32),
            pltpu.VMEM((C_HEADS, t, 2 * t), _BF16),
            pltpu.VMEM((C_HEADS, 1, 2 * t), _F32),
        ],
        compiler_params=pltpu.CompilerParams(
            dimension_semantics=("arbitrary", "arbitrary"),
            vmem_limit_bytes=VMEM_LIMIT_BYTES),
        name="diff_attention",
    )(rel_table, q, k, vt, lam_params, subln_g)


def _out_mlp_kernel(h_ref, ya_ref, yb_ref, g_ref, w_out_ref, w1_ref, w2_ref, o_ref):
    half = D_MODEL // 2
    sub = MLP_ROW_TILE // MLP_SUBTILES
    rows = [slice(r * sub, (r + 1) * sub) for r in range(MLP_SUBTILES)]
    ys = [jnp.dot(ya_ref[rs, :], w_out_ref[:half, :], preferred_element_type=_F32)
          + jnp.dot(yb_ref[rs, :], w_out_ref[half:, :], preferred_element_type=_F32)
          for rs in rows]
    h1s = [h_ref[rs, :] + _rms(y, g_ref[1:2, :]) for rs, y in zip(rows, ys)]
    hns = [_rms(h1, g_ref[2:3, :]).astype(_BF16) for h1 in h1s]
    for rs, h1, hn in zip(rows, h1s, hns):
        acc = jnp.zeros((sub, D_MODEL), _F32)
        for c in range(D_FF // FF_CHUNK):
            cs = slice(c * FF_CHUNK, (c + 1) * FF_CHUNK)
            a = jnp.dot(hn, w1_ref[:, cs], preferred_element_type=_F32)
            a = jnp.square(jnp.maximum(a, 0.0)).astype(_BF16)
            acc = acc + jnp.dot(a, w2_ref[cs, :], preferred_element_type=_F32)
        o_ref[rs, :] = h1 + _rms(acc, g_ref[3:4, :])


def _out_mlp(h, ya, yb, g, w_out, w1, w2):
    bsz, seq, _ = h.shape
    grid = (bsz, seq // MLP_ROW_TILE)
    row_spec = functools.partial(_row_spec, tile=MLP_ROW_TILE)
    return pl.pallas_call(
        _out_mlp_kernel,
        grid=grid,
        in_specs=[
            row_spec(D_MODEL),
            row_spec(D_MODEL // 2),
            row_spec(D_MODEL // 2),
            _const_spec((4, D_MODEL)),
            _const_spec((D_MODEL, D_MODEL)),
            _const_spec((D_MODEL, D_FF)),
            _const_spec((D_FF, D_MODEL)),
        ],
        out_specs=row_spec(D_MODEL),
        out_shape=jax.ShapeDtypeStruct(h.shape, h.dtype),
        compiler_params=pltpu.CompilerParams(
            dimension_semantics=("arbitrary", "arbitrary"),
            vmem_limit_bytes=VMEM_LIMIT_BYTES),
        name="out_mlp",
    )(h, ya, yb, g, w_out, w1, w2)


def kernel(x, rel_bias_table, norm_g, even_w_in, even_ln_g, even_ln_b, even_spatial_w,
           even_spatial_b, even_pool_w, even_pool_scale, even_w_out, odd_w_in, odd_lambda,
           odd_subln_g, odd_conv_w, odd_w_out, ffn_w1, ffn_w2):
    depth = norm_g.shape[0]
    bf = lambda w: w.astype(_BF16)
    h = x
    for layer in range(depth):
        g = norm_g[layer]
        if layer % 2 == 0:
            e = layer // 2
            ya, yb = _even_mixer(
                h, g[0:1], bf(even_w_in[e]), even_ln_g[e][None], even_ln_b[e][None],
                even_spatial_w[e], even_spatial_b[e].T, bf(even_pool_w[e]),
                even_pool_scale[e][None])
            w_out = even_w_out[e]
        else:
            o = layer // 2
            lambda_init = 0.8 - 0.6 * math.exp(-0.3 * layer)
            q, k, vt, yb = _odd_proj(h, g[0:1], bf(odd_w_in[o]), odd_conv_w[o])
            ya = _diff_attention(q, k, vt, rel_bias_table, odd_lambda[o],
                                 odd_subln_g[o][:, None], lambda_init)
            w_out = odd_w_out[o]
        h = _out_mlp(h, ya, yb, g, bf(w_out), bf(ffn_w1[layer]), bf(ffn_w2[layer]))
    return h
```

```python
import functools
import math

import numpy as np
import jax
import jax.numpy as jnp
from jax import lax
from jax.experimental import pallas as pl
from jax.experimental.pallas import tpu as pltpu

D_MODEL = 1024
A_WIDTH = 512
A_GROUPS = 4
A_GROUP_DIM = 128
CHUNK = 128
B_WIDTH = 512
POOL_WINDOWS = (2, 4, 8, 16)
B_GROUP_DIM = 128
C_HEADS = 4
C_QK_DIM = 64
C_V_DIM = 128
C_WIDTH = 512
D_WIDTH = 512
CONV_WIDTH = 3
REL_BUCKETS = 32
REL_MAX_DIST = 128
D_FF = 4096
EPS = 1e-6
EVEN_IN = 2 * A_WIDTH + B_WIDTH
ODD_IN = 2 * C_WIDTH + C_WIDTH + 3 * D_WIDTH

ROW_TILE = 1024
ATT_TILE = 256
ATT_TILES_PER_STEP = 4
FF_CHUNK = 1024
PROJ_SUBTILES = 2
MLP_ROW_TILE = 1024
MLP_SUBTILES = 4
POOL_HALO = 16
CONV_HALO = 8
LOG2E = math.log2(math.e)
FAR_UNROLL = 4
SCORE_LOOKAHEAD = 1
SUM_ROWS = 16
MASK_VALUE = -1e30
BF16_SUBLANES = 16
VMEM_LIMIT_BYTES = 52 * 1024 * 1024

_F32 = jnp.float32
_BF16 = jnp.bfloat16


def _rms(x, g):
    return x * lax.rsqrt(jnp.mean(x * x, axis=-1, keepdims=True) + EPS) * g


def _gelu_tanh(x):
    c = math.sqrt(2.0 / math.pi)
    hx = 0.5 * x
    return hx + hx * jnp.tanh(x * (c + (c * 0.044715) * (x * x)))


def _norm_project(x_ref, g_ref, w_ref):
    rows = x_ref.shape[0] // PROJ_SUBTILES
    parts = []
    for r in range(PROJ_SUBTILES):
        hn = _rms(x_ref[r * rows:(r + 1) * rows, :], g_ref[...]).astype(_BF16)
        parts.append(jnp.dot(hn, w_ref[...], preferred_element_type=_F32))
    return jnp.concatenate(parts, axis=0)


def _const_spec(shape):
    nd = len(shape)
    return pl.BlockSpec(shape, lambda *_: (0,) * nd, pipeline_mode=pl.Buffered(1))


def _row_spec(width, col=0, tile=None):
    return pl.BlockSpec((None, tile or ROW_TILE, width), lambda b, j: (b, j, col))


def _cast_plan(weights, grid):
    steps = grid[0] * grid[1]
    in_specs, out_specs, out_shapes = [], [], []
    for w, layer in weights:
        _, rows, cols = w.shape
        slab = rows // steps
        assert slab * steps == rows and slab % BF16_SUBLANES == 0
        in_specs.append(pl.BlockSpec(
            (None, slab, cols), lambda b, j, layer=layer: (layer, b * grid[1] + j, 0)))
        out_specs.append(pl.BlockSpec((slab, cols), lambda b, j: (b * grid[1] + j, 0)))
        out_shapes.append(jax.ShapeDtypeStruct((rows, cols), _BF16))
    return in_specs, out_specs, out_shapes


def _cast_slabs(refs):
    n = len(refs) // 2
    for src, dst in zip(refs[:n], refs[n:]):
        dst[...] = src[...].astype(_BF16)


def _even_mixer_kernel(n_cast, x_ref, g_ref, w_in_ref, ln_g_ref, ln_b_ref, ws_ref, bs_ref,
                       pw_ref, ps_ref, *rest):
    cast_in, (ya_ref, yb_ref), rest = rest[:n_cast], rest[n_cast:n_cast + 2], rest[n_cast + 2:]
    cast_out, (carry_ref,) = rest[:n_cast], rest[n_cast:]
    _cast_slabs(cast_in + cast_out)
    j = pl.program_id(1)
    proj = _norm_project(x_ref, g_ref, w_in_ref)

    z = _gelu_tanh(proj[:, :2 * A_WIDTH])
    u = z[:, :A_WIDTH]
    v = z[:, A_WIDTH:]
    mu = jnp.mean(v, axis=-1, keepdims=True)
    vc = v - mu
    vn = vc * lax.rsqrt(jnp.mean(vc * vc, axis=-1, keepdims=True) + EPS)
    vn = (vn * ln_g_ref[...] + ln_b_ref[...]).astype(_BF16)
    row = lax.broadcasted_iota(jnp.int32, (CHUNK, CHUNK), 0)
    col = lax.broadcasted_iota(jnp.int32, (CHUNK, CHUNK), 1)
    for g in range(A_GROUPS):
        cs = slice(g * A_GROUP_DIM, (g + 1) * A_GROUP_DIM)
        w = jnp.where(row >= col, ws_ref[g], 0.0).astype(_BF16)
        bias = bs_ref[:, g:g + 1]
        for c in range(0, ROW_TILE // CHUNK, 2):
            r0 = slice(c * CHUNK, (c + 1) * CHUNK)
            r1 = slice((c + 1) * CHUNK, (c + 2) * CHUNK)
            rhs = jnp.concatenate([vn[r0, cs], vn[r1, cs]], axis=1)
            mixed = jnp.dot(w, rhs, preferred_element_type=_F32) + bias
            ya_ref[r0, cs] = (u[r0, cs] * mixed[:, :A_GROUP_DIM]).astype(_BF16)
            ya_ref[r1, cs] = (u[r1, cs] * mixed[:, A_GROUP_DIM:]).astype(_BF16)

    p = proj[:, 2 * A_WIDTH:]

    @pl.when(j == 0)
    def _():
        carry_ref[...] = jnp.zeros_like(carry_ref)

    cur = jnp.concatenate([carry_ref[...], p], axis=0)
    carry_ref[...] = p[ROW_TILE - POOL_HALO:, :]
    cur_win = 1
    pos = j * ROW_TILE + lax.broadcasted_iota(jnp.int32, (ROW_TILE, B_GROUP_DIM), 0)
    tokens_so_far = (pos + 1).astype(_F32)
    for g, win in enumerate(POOL_WINDOWS):
        cs = slice(g * B_GROUP_DIM, (g + 1) * B_GROUP_DIM)
        while cur_win < win:
            cur = cur + pltpu.roll(cur, cur_win, axis=0)
            cur_win *= 2
        assert cur_win == win and win <= POOL_HALO
        count = jnp.minimum(tokens_so_far, float(win))
        pooled = (cur[POOL_HALO:, :B_GROUP_DIM] / count - p[:, cs]).astype(_BF16)
        y = jnp.dot(pooled, pw_ref[g], preferred_element_type=_F32)
        yb_ref[:, cs] = (y * ps_ref[:, cs]).astype(_BF16)
        cur = cur[:, B_GROUP_DIM:]


def _even_mixer(h, g0, w_in, ln_g, ln_b, w_s, b_s_t, pool_w, pool_scale, casts):
    bsz, seq, _ = h.shape
    grid = (bsz, seq // ROW_TILE)
    out = jax.ShapeDtypeStruct((bsz, seq, A_WIDTH), _BF16)
    cast_in, cast_out, cast_shapes = _cast_plan(casts, grid)
    return pl.pallas_call(
        functools.partial(_even_mixer_kernel, len(casts)),
        grid=grid,
        in_specs=[
            _row_spec(D_MODEL),
            _const_spec((1, D_MODEL)),
            _const_spec((D_MODEL, EVEN_IN)),
            _const_spec((1, A_WIDTH)),
            _const_spec((1, A_WIDTH)),
            _const_spec((A_GROUPS, CHUNK, CHUNK)),
            _const_spec((CHUNK, A_GROUPS)),
            _const_spec((len(POOL_WINDOWS), B_GROUP_DIM, B_GROUP_DIM)),
            _const_spec((1, B_WIDTH)),
        ] + cast_in,
        out_specs=[_row_spec(A_WIDTH), _row_spec(B_WIDTH)] + cast_out,
        out_shape=[out, out] + cast_shapes,
        scratch_shapes=[
            pltpu.VMEM((POOL_HALO, B_WIDTH), _F32),
        ],
        compiler_params=pltpu.CompilerParams(
            dimension_semantics=("arbitrary", "arbitrary"),
            vmem_limit_bytes=VMEM_LIMIT_BYTES),
        name="even_mixer",
    )(h, g0, w_in, ln_g, ln_b, w_s, b_s_t, pool_w, pool_scale, *[w for w, _ in casts])


def _odd_proj_kernel(x_ref, g_ref, w_in_ref, cw_ref, q_ref, k_ref, vt_ref, yd_ref,
                     carry_ref, ext_ref):
    j = pl.program_id(1)
    proj = _norm_project(x_ref, g_ref, w_in_ref)
    q_ref[...] = (proj[:, :C_WIDTH] * (C_QK_DIM ** -0.5 * LOG2E)).astype(_BF16)
    k_ref[...] = proj[:, C_WIDTH:2 * C_WIDTH].astype(_BF16)
    vt_ref[...] = proj[:, 2 * C_WIDTH:3 * C_WIDTH].T.astype(_BF16)
    o = 3 * C_WIDTH
    bg = proj[:, o:o + D_WIDTH]
    z = proj[:, o + D_WIDTH:o + 2 * D_WIDTH] * proj[:, o + 2 * D_WIDTH:]

    @pl.when(j == 0)
    def _():
        carry_ref[...] = jnp.zeros_like(carry_ref)

    ext_ref[0:CONV_HALO, :] = carry_ref[...]
    ext_ref[CONV_HALO:, :] = z
    carry_ref[...] = z[ROW_TILE - CONV_HALO:, :]
    y = cw_ref[CONV_WIDTH - 1:CONV_WIDTH, :] * z
    for t in range(CONV_WIDTH - 1):
        shift = CONV_WIDTH - 1 - t
        y = y + cw_ref[t:t + 1, :] * ext_ref[CONV_HALO - shift:CONV_HALO - shift + ROW_TILE, :]
    yd_ref[...] = (bg * y).astype(_BF16)


def _odd_proj(h, g0, w_in, conv_w):
    bsz, seq, _ = h.shape
    grid = (bsz, seq // ROW_TILE)
    out = jax.ShapeDtypeStruct((bsz, seq, C_WIDTH), _BF16)
    out_t = jax.ShapeDtypeStruct((bsz, C_WIDTH, seq), _BF16)
    spec_t = pl.BlockSpec((None, C_WIDTH, ROW_TILE), lambda b, j: (b, 0, j))
    return pl.pallas_call(
        _odd_proj_kernel,
        grid=grid,
        in_specs=[
            _row_spec(D_MODEL),
            _const_spec((1, D_MODEL)),
            _const_spec((D_MODEL, ODD_IN)),
            _const_spec((CONV_WIDTH, D_WIDTH)),
        ],
        out_specs=[_row_spec(C_WIDTH), _row_spec(C_WIDTH), spec_t, _row_spec(D_WIDTH)],
        out_shape=[out, out, out_t, out],
        scratch_shapes=[
            pltpu.VMEM((CONV_HALO, D_WIDTH), _F32),
            pltpu.VMEM((ROW_TILE + CONV_HALO, D_WIDTH), _F32),
        ],
        compiler_params=pltpu.CompilerParams(
            dimension_semantics=("arbitrary", "arbitrary"),
            vmem_limit_bytes=VMEM_LIMIT_BYTES),
        name="odd_proj",
    )(h, g0, w_in, conv_w)


def _t5_bucket_upper_bounds(max_dist):
    d = np.arange(max_dist, dtype=np.int32)
    max_exact = REL_BUCKETS // 2
    nf = np.maximum(d, 1).astype(np.float32)
    large = max_exact + (np.log(nf / np.float32(max_exact))
                         / np.float32(math.log(REL_MAX_DIST / max_exact))
                         * np.float32(REL_BUCKETS - max_exact)).astype(np.int32)
    large = np.minimum(large, REL_BUCKETS - 1)
    bucket = np.where(d < max_exact, d, large)
    assert np.all(np.diff(bucket) >= 0)
    return bucket, {int(b): int(d[bucket == b].max()) for b in np.unique(bucket)}


def _attn_kernel(bucket_hi, lambda_init,
                 tab_ref, q_ref, k_ref, vt_ref, lam_ref, sg_ref, o_ref,
                 qq_ref, bias_ref, *scratch):
    t = ATT_TILE
    blk = pl.program_id(1)
    last_bucket = REL_BUCKETS - 1

    @pl.when(blk == 0)
    def _():
        kpos = lax.broadcasted_iota(jnp.int32, (t, t), 0)
        qpos = lax.broadcasted_iota(jnp.int32, (t, t), 1)
        for delta in range(2):
            d = qpos - kpos + delta * t
            for hd in range(C_HEADS):
                far = tab_ref[last_bucket, hd]
                val = jnp.zeros((t, t), _F32)
                for b in sorted(bucket_hi, reverse=True):
                    if b == last_bucket:
                        continue
                    val = jnp.where(d <= bucket_hi[b], (tab_ref[b, hd] - far) * LOG2E, val)
                val = jnp.where(d >= 0, val, MASK_VALUE)
                bias_ref[delta, hd] = jnp.concatenate([val, val], axis=1)

    def query_tile(sub, carry):
        rows = pl.ds(pl.multiple_of(sub * t, t), t)
        _attn_query_tile(lambda_init, blk * ATT_TILES_PER_STEP + sub,
                         q_ref.at[rows], k_ref, vt_ref, lam_ref, sg_ref, o_ref.at[rows],
                         qq_ref, bias_ref, *scratch)
        return carry

    lax.fori_loop(0, ATT_TILES_PER_STEP, query_tile, 0)


def _attn_query_tile(lambda_init, i, q_ref, k_ref, vt_ref, lam_ref, sg_ref, o_ref,
                     qq_ref, bias_ref, m_ref, acc_ref, s_ref, p_ref, al_ref):
    t = ATT_TILE
    hd_dim = 2 * C_QK_DIM
    heads = range(C_HEADS)

    feat = lax.broadcasted_iota(jnp.int32, (hd_dim, t), 0)
    for hd in heads:
        qt = q_ref[:, hd * hd_dim:(hd + 1) * hd_dim].astype(_F32).T
        qq_ref[hd] = jnp.concatenate(
            [jnp.where(feat < C_QK_DIM, qt, 0.0), jnp.where(feat >= C_QK_DIM, qt, 0.0)],
            axis=1).astype(_BF16)
    m_ref[...] = jnp.full_like(m_ref, MASK_VALUE)
    acc_ref[...] = jnp.zeros_like(acc_ref)
    p_ref[C_HEADS - 1] = jnp.zeros((t, 2 * t), _BF16)
    al_ref[C_HEADS - 1] = jnp.ones((1, 2 * t), _F32)

    def scores(jk, hd):
        start = pl.multiple_of(jk * t, t)
        k_t = k_ref[pl.ds(start, t), hd * hd_dim:(hd + 1) * hd_dim]
        s_ref[hd] = jnp.dot(k_t, qq_ref[hd], preferred_element_type=_F32)

    def values(jk, hd):
        start = pl.multiple_of(jk * t, t)
        vt = vt_ref[hd * C_V_DIM:(hd + 1) * C_V_DIM, pl.ds(start, t)]
        vt_ones = jnp.concatenate([vt, jnp.ones((SUM_ROWS, t), _BF16)], axis=0)
        acc_ref[hd] = al_ref[hd] * acc_ref[hd] + jnp.dot(
            vt_ones, p_ref[hd], preferred_element_type=_F32)

    def step(jk, delta, jk_next):
        for hd in heads:
            ahead = hd + SCORE_LOOKAHEAD
            if ahead < C_HEADS:
                scores(jk, ahead)
            elif jk_next is not None:
                scores(jk_next, ahead - C_HEADS)
            s = s_ref[hd]
            if delta is not None:
                s = s + bias_ref[delta, hd]
            m_prev = m_ref[hd]
            m_next = jnp.maximum(m_prev, jnp.max(s, axis=0, keepdims=True))
            alpha = jnp.exp2(m_prev - m_next)
            p = jnp.exp2(s - m_next)
            m_ref[hd] = m_next
            if hd == 0:
                values(jnp.maximum(jk - 1, 0), C_HEADS - 1)
            else:
                values(jk, hd - 1)
            p_ref[hd] = p.astype(_BF16)
            al_ref[hd] = alpha

    for hd in range(SCORE_LOOKAHEAD):
        scores(0, hd)

    n_far = jnp.maximum(i - 1, 0)

    def far_run(first, count):
        for n in range(count):
            step(first + n, None, first + n + 1)

    def far_trip(r, carry):
        far_run(FAR_UNROLL * r, FAR_UNROLL)
        return carry

    trips = n_far // FAR_UNROLL
    lax.fori_loop(0, trips, far_trip, 0)
    done = FAR_UNROLL * trips
    run = FAR_UNROLL // 2
    while run >= 2:
        take = (n_far - done) >= run
        pl.when(take)(functools.partial(far_run, done, run))
        done = done + jnp.where(take, run, 0)
        run //= 2

    def finish(odd_far_tile, sub_diagonal):
        if odd_far_tile:
            step(n_far - 1, None, n_far)
        if sub_diagonal:
            step(i - 1, 1, i)
        step(i, 0, None)
        values(i, C_HEADS - 1)
        lp = lam_ref[...]
        lam = (jnp.exp(jnp.sum(lp[0:1] * lp[1:2], axis=-1, keepdims=True))
               - jnp.exp(jnp.sum(lp[2:3] * lp[3:4], axis=-1, keepdims=True)) + lambda_init)
        for hd in heads:
            inv_l = 1.0 / acc_ref[hd, C_V_DIM:C_V_DIM + 1, :]
            a = (acc_ref[hd, :C_V_DIM, :t] * inv_l[:, :t]
                 - acc_ref[hd, :C_V_DIM, t:] * (lam * inv_l[:, t:]))
            ms = jnp.mean(a * a, axis=0, keepdims=True)
            y = a * (lax.rsqrt(ms + EPS) * (1.0 - lambda_init)) * sg_ref[...]
            o_ref[:, hd * C_V_DIM:(hd + 1) * C_V_DIM] = y.T.astype(_BF16)

    odd_far = n_far % 2 == 1
    pl.when(i == 0)(lambda: finish(False, False))
    pl.when(jnp.logical_and(i >= 1, jnp.logical_not(odd_far)))(lambda: finish(False, True))
    pl.when(odd_far)(lambda: finish(True, True))


def _diff_attention(q, k, vt, rel_table, lam_params, subln_g, lambda_init):
    bsz, seq, _ = q.shape
    t = ATT_TILE
    bucket, bucket_hi = _t5_bucket_upper_bounds(seq)
    assert np.all(bucket[t + 1:] == REL_BUCKETS - 1)
    rows = t * ATT_TILES_PER_STEP
    grid = (bsz, seq // rows)
    kernel = functools.partial(_attn_kernel, bucket_hi, lambda_init)
    return pl.pallas_call(
        kernel,
        grid=grid,
        in_specs=[
            pl.BlockSpec(memory_space=pltpu.SMEM),
            pl.BlockSpec((None, rows, C_WIDTH), lambda b, i: (b, i, 0)),
            pl.BlockSpec((None, seq, C_WIDTH), lambda b, i: (b, 0, 0)),
            pl.BlockSpec((None, C_WIDTH, seq), lambda b, i: (b, 0, 0)),
            pl.BlockSpec((4, C_QK_DIM), lambda b, i: (0, 0)),
            pl.BlockSpec((C_V_DIM, 1), lambda b, i: (0, 0)),
        ],
        out_specs=pl.BlockSpec((None, rows, C_WIDTH), lambda b, i: (b, i, 0)),
        out_shape=jax.ShapeDtypeStruct((bsz, seq, C_WIDTH), _BF16),
        scratch_shapes=[
            pltpu.VMEM((C_HEADS, 2 * C_QK_DIM, 2 * t), _BF16),
            pltpu.VMEM((2, C_HEADS, t, 2 * t), _F32),
            pltpu.VMEM((C_HEADS, 1, 2 * t), _F32),
            pltpu.VMEM((C_HEADS, C_V_DIM + SUM_ROWS, 2 * t), _F32),
            pltpu.VMEM((C_HEADS, t, 2 * t), _F32),
            pltpu.VMEM((C_HEADS, t, 2 * t), _BF16),
            pltpu.VMEM((C_HEADS, 1, 2 * t), _F32),
        ],
        compiler_params=pltpu.CompilerParams(
            dimension_semantics=("arbitrary", "arbitrary"),
            vmem_limit_bytes=VMEM_LIMIT_BYTES),
        name="diff_attention",
    )(rel_table, q, k, vt, lam_params, subln_g)


def _out_mlp_kernel(n_cast, h_ref, ya_ref, yb_ref, g_ref, w_out_ref, w1_ref, w2_ref, *rest):
    cast_in, o_ref, cast_out = rest[:n_cast], rest[n_cast], rest[n_cast + 1:]
    _cast_slabs(cast_in + cast_out)
    half = D_MODEL // 2
    sub = MLP_ROW_TILE // MLP_SUBTILES
    rows = [slice(r * sub, (r + 1) * sub) for r in range(MLP_SUBTILES)]
    ys = [jnp.dot(ya_ref[rs, :], w_out_ref[:half, :], preferred_element_type=_F32)
          + jnp.dot(yb_ref[rs, :], w_out_ref[half:, :], preferred_element_type=_F32)
          for rs in rows]
    h1s = [h_ref[rs, :] + _rms(y, g_ref[1:2, :]) for rs, y in zip(rows, ys)]
    hns = [_rms(h1, g_ref[2:3, :]).astype(_BF16) for h1 in h1s]
    for rs, h1, hn in zip(rows, h1s, hns):
        acc = jnp.zeros((sub, D_MODEL), _F32)
        for c in range(D_FF // FF_CHUNK):
            cs = slice(c * FF_CHUNK, (c + 1) * FF_CHUNK)
            a = jnp.dot(hn, w1_ref[:, cs], preferred_element_type=_F32)
            a = jnp.square(jnp.maximum(a, 0.0)).astype(_BF16)
            acc = acc + jnp.dot(a, w2_ref[cs, :], preferred_element_type=_F32)
        o_ref[rs, :] = h1 + _rms(acc, g_ref[3:4, :])


def _out_mlp(h, ya, yb, g, w_out, w1, w2, casts=()):
    bsz, seq, _ = h.shape
    grid = (bsz, seq // MLP_ROW_TILE)
    row_spec = functools.partial(_row_spec, tile=MLP_ROW_TILE)
    cast_in, cast_out, cast_shapes = _cast_plan(casts, grid)
    return pl.pallas_call(
        functools.partial(_out_mlp_kernel, len(casts)),
        grid=grid,
        in_specs=[
            row_spec(D_MODEL),
            row_spec(D_MODEL // 2),
            row_spec(D_MODEL // 2),
            _const_spec((4, D_MODEL)),
            _const_spec((D_MODEL, D_MODEL)),
            _const_spec((D_MODEL, D_FF)),
            _const_spec((D_FF, D_MODEL)),
        ] + cast_in,
        out_specs=[row_spec(D_MODEL)] + cast_out,
        out_shape=[jax.ShapeDtypeStruct(h.shape, h.dtype)] + cast_shapes,
        compiler_params=pltpu.CompilerParams(
            dimension_semantics=("arbitrary", "arbitrary"),
            vmem_limit_bytes=VMEM_LIMIT_BYTES),
        name="out_mlp",
    )(h, ya, yb, g, w_out, w1, w2, *[w for w, _ in casts])


def kernel(x, rel_bias_table, norm_g, even_w_in, even_ln_g, even_ln_b, even_spatial_w,
           even_spatial_b, even_pool_w, even_pool_scale, even_w_out, odd_w_in, odd_lambda,
           odd_subln_g, odd_conv_w, odd_w_out, ffn_w1, ffn_w2):
    depth = norm_g.shape[0]
    bf = lambda w: w.astype(_BF16)

    def layer_weights(layer):
        mixer = (even_w_in, even_w_out) if layer % 2 == 0 else (odd_w_in, odd_w_out)
        return [(mixer[0], layer // 2), (mixer[1], layer // 2),
                (ffn_w1, layer), (ffn_w2, layer)]

    first = layer_weights(0)
    w_in = bf(first[0][0][first[0][1]])
    w_out = w1 = w2 = None
    h = x
    for layer in range(depth):
        g = norm_g[layer]
        own_casts = [] if w_out is not None else layer_weights(layer)[1:]
        if layer % 2 == 0:
            e = layer // 2
            ya, yb, *cast = _even_mixer(
                h, g[0:1], w_in, even_ln_g[e][None], even_ln_b[e][None],
                even_spatial_w[e], even_spatial_b[e].T, bf(even_pool_w[e]),
                even_pool_scale[e][None], own_casts)
        else:
            o = layer // 2
            lambda_init = 0.8 - 0.6 * math.exp(-0.3 * layer)
            q, k, vt, yb = _odd_proj(h, g[0:1], w_in, odd_conv_w[o])
            ya = _diff_attention(q, k, vt, rel_bias_table, odd_lambda[o],
                                 odd_subln_g[o][:, None], lambda_init)
            cast = [bf(w[i]) for w, i in own_casts]
        if own_casts:
            w_out, w1, w2 = cast
        next_casts = layer_weights(layer + 1) if layer + 1 < depth else []
        h, *nxt = _out_mlp(h, ya, yb, g, w_out, w1, w2, next_casts)
        w_in, w_out, w1, w2 = nxt if nxt else (None,) * 4
    return h
```

```python
import functools
import math

import numpy as np
import jax
import jax.numpy as jnp
from jax import lax
from jax.experimental import pallas as pl
from jax.experimental.pallas import tpu as pltpu

D_MODEL = 1024
A_WIDTH = 512
A_GROUPS = 4
A_GROUP_DIM = 128
CHUNK = 128
B_WIDTH = 512
POOL_WINDOWS = (2, 4, 8, 16)
B_GROUP_DIM = 128
C_HEADS = 4
C_QK_DIM = 64
C_V_DIM = 128
C_WIDTH = 512
D_WIDTH = 512
CONV_WIDTH = 3
REL_BUCKETS = 32
REL_MAX_DIST = 128
D_FF = 4096
EPS = 1e-6
EVEN_IN = 2 * A_WIDTH + B_WIDTH
ODD_IN = 2 * C_WIDTH + C_WIDTH + 3 * D_WIDTH

ROW_TILE = 1024
ATT_TILE = 256
ATT_TILES_PER_STEP = 4
FF_CHUNK = 1024
PROJ_SUBTILES = 4
PROJ_SLOTS = 3
MLP_ROW_TILE = 1024
MLP_SUBTILES = 4
POOL_HALO = 16
CONV_HALO = 8
LOG2E = math.log2(math.e)
FAR_UNROLL = 4
SCORE_LOOKAHEAD = 1
SUM_ROWS = 16
MASK_VALUE = -1e30
BF16_SUBLANES = 16
VMEM_LIMIT_BYTES = 52 * 1024 * 1024

_F32 = jnp.float32
_BF16 = jnp.bfloat16


def _rms(x, g):
    return x * lax.rsqrt(jnp.mean(x * x, axis=-1, keepdims=True) + EPS) * g


def _gelu_tanh(x):
    c = math.sqrt(2.0 / math.pi)
    hx = 0.5 * x
    return hx + hx * jnp.tanh(x * (c + (c * 0.044715) * (x * x)))


def _norm_project(x_ref, g_ref, w_ref, proj_ref):
    rows = x_ref.shape[0] // PROJ_SUBTILES
    slots = proj_ref.shape[0]
    ahead = slots - 1

    def project(r):
        hn = _rms(x_ref[r * rows:(r + 1) * rows, :], g_ref[...]).astype(_BF16)
        proj_ref[r % slots] = jnp.dot(hn, w_ref[...], preferred_element_type=_F32)

    for r in range(min(ahead, PROJ_SUBTILES)):
        project(r)
    for r in range(PROJ_SUBTILES):
        if r + ahead < PROJ_SUBTILES:
            project(r + ahead)
        yield r, proj_ref.at[r % slots]


def _const_spec(shape):
    nd = len(shape)
    return pl.BlockSpec(shape, lambda *_: (0,) * nd, pipeline_mode=pl.Buffered(1))


def _row_spec(width, col=0, tile=None):
    return pl.BlockSpec((None, tile or ROW_TILE, width), lambda b, j: (b, j, col))


def _cast_plan(weights, grid):
    steps = grid[0] * grid[1]
    in_specs, out_specs, out_shapes = [], [], []
    for w, layer in weights:
        _, rows, cols = w.shape
        slab = rows // steps
        assert slab * steps == rows and slab % BF16_SUBLANES == 0
        in_specs.append(pl.BlockSpec(
            (None, slab, cols), lambda b, j, layer=layer: (layer, b * grid[1] + j, 0)))
        out_specs.append(pl.BlockSpec((slab, cols), lambda b, j: (b * grid[1] + j, 0)))
        out_shapes.append(jax.ShapeDtypeStruct((rows, cols), _BF16))
    return in_specs, out_specs, out_shapes


def _cast_slabs(refs):
    n = len(refs) // 2
    for src, dst in zip(refs[:n], refs[n:]):
        dst[...] = src[...].astype(_BF16)


def _even_mixer_kernel(n_cast, x_ref, g_ref, w_in_ref, ln_g_ref, ln_b_ref, ws_ref, bs_ref,
                       pw_ref, ps_ref, *rest):
    cast_in, (ya_ref, yb_ref), rest = rest[:n_cast], rest[n_cast:n_cast + 2], rest[n_cast + 2:]
    cast_out, (carry_ref, proj_ref) = rest[:n_cast], rest[n_cast:]
    _cast_slabs(cast_in + cast_out)
    j = pl.program_id(1)
    rows = ROW_TILE // PROJ_SUBTILES

    @pl.when(j == 0)
    def _():
        carry_ref[...] = jnp.zeros_like(carry_ref)

    row = lax.broadcasted_iota(jnp.int32, (CHUNK, CHUNK), 0)
    col = lax.broadcasted_iota(jnp.int32, (CHUNK, CHUNK), 1)
    tril_w = [jnp.where(row >= col, ws_ref[g], 0.0).astype(_BF16) for g in range(A_GROUPS)]
    halo = carry_ref[...]
    for sub, proj in _norm_project(x_ref, g_ref, w_in_ref, proj_ref):
        base = sub * rows

        z = _gelu_tanh(proj[:, :2 * A_WIDTH])
        u = z[:, :A_WIDTH]
        v = z[:, A_WIDTH:]
        mu = jnp.mean(v, axis=-1, keepdims=True)
        vc = v - mu
        vn = vc * lax.rsqrt(jnp.mean(vc * vc, axis=-1, keepdims=True) + EPS)
        vn = (vn * ln_g_ref[...] + ln_b_ref[...]).astype(_BF16)
        for g in range(A_GROUPS):
            cs = slice(g * A_GROUP_DIM, (g + 1) * A_GROUP_DIM)
            bias = bs_ref[:, g:g + 1]
            for c in range(0, rows // CHUNK, 2):
                r0 = slice(c * CHUNK, (c + 1) * CHUNK)
                r1 = slice((c + 1) * CHUNK, (c + 2) * CHUNK)
                rhs = jnp.concatenate([vn[r0, cs], vn[r1, cs]], axis=1)
                mixed = jnp.dot(tril_w[g], rhs, preferred_element_type=_F32) + bias
                ya_ref[base + r0.start:base + r0.stop, cs] = (
                    u[r0, cs] * mixed[:, :A_GROUP_DIM]).astype(_BF16)
                ya_ref[base + r1.start:base + r1.stop, cs] = (
                    u[r1, cs] * mixed[:, A_GROUP_DIM:]).astype(_BF16)

        p = proj[:, 2 * A_WIDTH:]
        cur = jnp.concatenate([halo, p], axis=0)
        halo = p[rows - POOL_HALO:, :]
        cur_win = 1
        pos = j * ROW_TILE + base + lax.broadcasted_iota(jnp.int32, (rows, B_GROUP_DIM), 0)
        tokens_so_far = (pos + 1).astype(_F32)
        for g, win in enumerate(POOL_WINDOWS):
            cs = slice(g * B_GROUP_DIM, (g + 1) * B_GROUP_DIM)
            while cur_win < win:
                cur = cur + pltpu.roll(cur, cur_win, axis=0)
                cur_win *= 2
            assert cur_win == win and win <= POOL_HALO
            count = jnp.minimum(tokens_so_far, float(win))
            pooled = (cur[POOL_HALO:, :B_GROUP_DIM] / count - p[:, cs]).astype(_BF16)
            y = jnp.dot(pooled, pw_ref[g], preferred_element_type=_F32)
            yb_ref[base:base + rows, cs] = (y * ps_ref[:, cs]).astype(_BF16)
            cur = cur[:, B_GROUP_DIM:]
    carry_ref[...] = halo


def _even_mixer(h, g0, w_in, ln_g, ln_b, w_s, b_s_t, pool_w, pool_scale, casts):
    bsz, seq, _ = h.shape
    grid = (bsz, seq // ROW_TILE)
    out = jax.ShapeDtypeStruct((bsz, seq, A_WIDTH), _BF16)
    cast_in, cast_out, cast_shapes = _cast_plan(casts, grid)
    return pl.pallas_call(
        functools.partial(_even_mixer_kernel, len(casts)),
        grid=grid,
        in_specs=[
            _row_spec(D_MODEL),
            _const_spec((1, D_MODEL)),
            _const_spec((D_MODEL, EVEN_IN)),
            _const_spec((1, A_WIDTH)),
            _const_spec((1, A_WIDTH)),
            _const_spec((A_GROUPS, CHUNK, CHUNK)),
            _const_spec((CHUNK, A_GROUPS)),
            _const_spec((len(POOL_WINDOWS), B_GROUP_DIM, B_GROUP_DIM)),
            _const_spec((1, B_WIDTH)),
        ] + cast_in,
        out_specs=[_row_spec(A_WIDTH), _row_spec(B_WIDTH)] + cast_out,
        out_shape=[out, out] + cast_shapes,
        scratch_shapes=[
            pltpu.VMEM((POOL_HALO, B_WIDTH), _F32),
            pltpu.VMEM((PROJ_SLOTS, ROW_TILE // PROJ_SUBTILES, EVEN_IN), _F32),
        ],
        compiler_params=pltpu.CompilerParams(
            dimension_semantics=("arbitrary", "arbitrary"),
            vmem_limit_bytes=VMEM_LIMIT_BYTES),
        name="even_mixer",
    )(h, g0, w_in, ln_g, ln_b, w_s, b_s_t, pool_w, pool_scale, *[w for w, _ in casts])


def _odd_proj_kernel(x_ref, g_ref, w_in_ref, cw_ref, q_ref, k_ref, vt_ref, yd_ref,
                     carry_ref, proj_ref):
    j = pl.program_id(1)
    rows = ROW_TILE // PROJ_SUBTILES

    @pl.when(j == 0)
    def _():
        carry_ref[...] = jnp.zeros_like(carry_ref)

    halo = carry_ref[...]
    for sub, proj in _norm_project(x_ref, g_ref, w_in_ref, proj_ref):
        rs = slice(sub * rows, (sub + 1) * rows)
        q_ref[rs, :] = (proj[:, :C_WIDTH] * (C_QK_DIM ** -0.5 * LOG2E)).astype(_BF16)
        k_ref[rs, :] = proj[:, C_WIDTH:2 * C_WIDTH].astype(_BF16)
        vt_ref[:, rs] = proj[:, 2 * C_WIDTH:3 * C_WIDTH].T.astype(_BF16)
        o = 3 * C_WIDTH
        bg = proj[:, o:o + D_WIDTH]
        z = proj[:, o + D_WIDTH:o + 2 * D_WIDTH] * proj[:, o + 2 * D_WIDTH:]
        ext = jnp.concatenate([halo, z], axis=0)
        halo = z[rows - CONV_HALO:, :]
        y = cw_ref[CONV_WIDTH - 1:CONV_WIDTH, :] * z
        for t in range(CONV_WIDTH - 1):
            shift = CONV_WIDTH - 1 - t
            y = y + cw_ref[t:t + 1, :] * pltpu.roll(ext, shift, axis=0)[CONV_HALO:, :]
        yd_ref[rs, :] = (bg * y).astype(_BF16)
    carry_ref[...] = halo


def _odd_proj(h, g0, w_in, conv_w):
    bsz, seq, _ = h.shape
    grid = (bsz, seq // ROW_TILE)
    out = jax.ShapeDtypeStruct((bsz, seq, C_WIDTH), _BF16)
    out_t = jax.ShapeDtypeStruct((bsz, C_WIDTH, seq), _BF16)
    spec_t = pl.BlockSpec((None, C_WIDTH, ROW_TILE), lambda b, j: (b, 0, j))
    return pl.pallas_call(
        _odd_proj_kernel,
        grid=grid,
        in_specs=[
            _row_spec(D_MODEL),
            _const_spec((1, D_MODEL)),
            _const_spec((D_MODEL, ODD_IN)),
            _const_spec((CONV_WIDTH, D_WIDTH)),
        ],
        out_specs=[_row_spec(C_WIDTH), _row_spec(C_WIDTH), spec_t, _row_spec(D_WIDTH)],
        out_shape=[out, out, out_t, out],
        scratch_shapes=[
            pltpu.VMEM((CONV_HALO, D_WIDTH), _F32),
            pltpu.VMEM((PROJ_SLOTS, ROW_TILE // PROJ_SUBTILES, ODD_IN), _F32),
        ],
        compiler_params=pltpu.CompilerParams(
            dimension_semantics=("arbitrary", "arbitrary"),
            vmem_limit_bytes=VMEM_LIMIT_BYTES),
        name="odd_proj",
    )(h, g0, w_in, conv_w)


def _t5_bucket_upper_bounds(max_dist):
    d = np.arange(max_dist, dtype=np.int32)
    max_exact = REL_BUCKETS // 2
    nf = np.maximum(d, 1).astype(np.float32)
    large = max_exact + (np.log(nf / np.float32(max_exact))
                         / np.float32(math.log(REL_MAX_DIST / max_exact))
                         * np.float32(REL_BUCKETS - max_exact)).astype(np.int32)
    large = np.minimum(large, REL_BUCKETS - 1)
    bucket = np.where(d < max_exact, d, large)
    assert np.all(np.diff(bucket) >= 0)
    return bucket, {int(b): int(d[bucket == b].max()) for b in np.unique(bucket)}


def _attn_kernel(bucket_hi, lambda_init,
                 tab_ref, q_ref, k_ref, vt_ref, lam_ref, sg_ref, o_ref,
                 qq_ref, bias_ref, *scratch):
    t = ATT_TILE
    blk = pl.program_id(1)
    last_bucket = REL_BUCKETS - 1

    @pl.when(blk == 0)
    def _():
        kpos = lax.broadcasted_iota(jnp.int32, (t, t), 0)
        qpos = lax.broadcasted_iota(jnp.int32, (t, t), 1)
        for delta in range(2):
            d = qpos - kpos + delta * t
            for hd in range(C_HEADS):
                far = tab_ref[last_bucket, hd]
                val = jnp.zeros((t, t), _F32)
                for b in sorted(bucket_hi, reverse=True):
                    if b == last_bucket:
                        continue
                    val = jnp.where(d <= bucket_hi[b], (tab_ref[b, hd] - far) * LOG2E, val)
                val = jnp.where(d >= 0, val, MASK_VALUE)
                bias_ref[delta, hd] = jnp.concatenate([val, val], axis=1)

    def query_tile(sub, carry):
        rows = pl.ds(pl.multiple_of(sub * t, t), t)
        _attn_query_tile(lambda_init, blk * ATT_TILES_PER_STEP + sub,
                         q_ref.at[rows], k_ref, vt_ref, lam_ref, sg_ref, o_ref.at[rows],
                         qq_ref, bias_ref, *scratch)
        return carry

    lax.fori_loop(0, ATT_TILES_PER_STEP, query_tile, 0)


def _attn_query_tile(lambda_init, i, q_ref, k_ref, vt_ref, lam_ref, sg_ref, o_ref,
                     qq_ref, bias_ref, m_ref, acc_ref, s_ref, p_ref, al_ref):
    t = ATT_TILE
    hd_dim = 2 * C_QK_DIM
    heads = range(C_HEADS)

    feat = lax.broadcasted_iota(jnp.int32, (hd_dim, t), 0)
    for hd in heads:
        qt = q_ref[:, hd * hd_dim:(hd + 1) * hd_dim].astype(_F32).T
        qq_ref[hd] = jnp.concatenate(
            [jnp.where(feat < C_QK_DIM, qt, 0.0), jnp.where(feat >= C_QK_DIM, qt, 0.0)],
            axis=1).astype(_BF16)
    m_ref[...] = jnp.full_like(m_ref, MASK_VALUE)
    acc_ref[...] = jnp.zeros_like(acc_ref)
    p_ref[C_HEADS - 1] = jnp.zeros((t, 2 * t), _BF16)
    al_ref[C_HEADS - 1] = jnp.ones((1, 2 * t), _F32)

    def scores(jk, hd):
        start = pl.multiple_of(jk * t, t)
        k_t = k_ref[pl.ds(start, t), hd * hd_dim:(hd + 1) * hd_dim]
        s_ref[hd] = jnp.dot(k_t, qq_ref[hd], preferred_element_type=_F32)

    def values(jk, hd):
        start = pl.multiple_of(jk * t, t)
        vt = vt_ref[hd * C_V_DIM:(hd + 1) * C_V_DIM, pl.ds(start, t)]
        vt_ones = jnp.concatenate([vt, jnp.ones((SUM_ROWS, t), _BF16)], axis=0)
        acc_ref[hd] = al_ref[hd] * acc_ref[hd] + jnp.dot(
            vt_ones, p_ref[hd], preferred_element_type=_F32)

    def step(jk, delta, jk_next):
        for hd in heads:
            ahead = hd + SCORE_LOOKAHEAD
            if ahead < C_HEADS:
                scores(jk, ahead)
            elif jk_next is not None:
                scores(jk_next, ahead - C_HEADS)
            s = s_ref[hd]
            if delta is not None:
                s = s + bias_ref[delta, hd]
            m_prev = m_ref[hd]
            m_next = jnp.maximum(m_prev, jnp.max(s, axis=0, keepdims=True))
            alpha = jnp.exp2(m_prev - m_next)
            p = jnp.exp2(s - m_next)
            m_ref[hd] = m_next
            if hd == 0:
                values(jnp.maximum(jk - 1, 0), C_HEADS - 1)
            else:
                values(jk, hd - 1)
            p_ref[hd] = p.astype(_BF16)
            al_ref[hd] = alpha

    for hd in range(SCORE_LOOKAHEAD):
        scores(0, hd)

    n_far = jnp.maximum(i - 1, 0)

    def far_run(first, count):
        for n in range(count):
            step(first + n, None, first + n + 1)

    def far_trip(r, carry):
        far_run(FAR_UNROLL * r, FAR_UNROLL)
        return carry

    trips = n_far // FAR_UNROLL
    lax.fori_loop(0, trips, far_trip, 0)
    done = FAR_UNROLL * trips
    run = FAR_UNROLL // 2
    while run >= 2:
        take = (n_far - done) >= run
        pl.when(take)(functools.partial(far_run, done, run))
        done = done + jnp.where(take, run, 0)
        run //= 2

    def finish(odd_far_tile, sub_diagonal):
        if odd_far_tile:
            step(n_far - 1, None, n_far)
        if sub_diagonal:
            step(i - 1, 1, i)
        step(i, 0, None)
        values(i, C_HEADS - 1)
        lp = lam_ref[...]
        lam = (jnp.exp(jnp.sum(lp[0:1] * lp[1:2], axis=-1, keepdims=True))
               - jnp.exp(jnp.sum(lp[2:3] * lp[3:4], axis=-1, keepdims=True)) + lambda_init)
        for hd in heads:
            inv_l = 1.0 / acc_ref[hd, C_V_DIM:C_V_DIM + 1, :]
            a = (acc_ref[hd, :C_V_DIM, :t] * inv_l[:, :t]
                 - acc_ref[hd, :C_V_DIM, t:] * (lam * inv_l[:, t:]))
            ms = jnp.mean(a * a, axis=0, keepdims=True)
            y = a * (lax.rsqrt(ms + EPS) * (1.0 - lambda_init)) * sg_ref[...]
            o_ref[:, hd * C_V_DIM:(hd + 1) * C_V_DIM] = y.T.astype(_BF16)

    odd_far = n_far % 2 == 1
    pl.when(i == 0)(lambda: finish(False, False))
    pl.when(jnp.logical_and(i >= 1, jnp.logical_not(odd_far)))(lambda: finish(False, True))
    pl.when(odd_far)(lambda: finish(True, True))


def _diff_attention(q, k, vt, rel_table, lam_params, subln_g, lambda_init):
    bsz, seq, _ = q.shape
    t = ATT_TILE
    bucket, bucket_hi = _t5_bucket_upper_bounds(seq)
    assert np.all(bucket[t + 1:] == REL_BUCKETS - 1)
    rows = t * ATT_TILES_PER_STEP
    grid = (bsz, seq // rows)
    kernel = functools.partial(_attn_kernel, bucket_hi, lambda_init)
    return pl.pallas_call(
        kernel,
        grid=grid,
        in_specs=[
            pl.BlockSpec(memory_space=pltpu.SMEM),
            pl.BlockSpec((None, rows, C_WIDTH), lambda b, i: (b, i, 0)),
            pl.BlockSpec((None, seq, C_WIDTH), lambda b, i: (b, 0, 0)),
            pl.BlockSpec((None, C_WIDTH, seq), lambda b, i: (b, 0, 0)),
            pl.BlockSpec((4, C_QK_DIM), lambda b, i: (0, 0)),
            pl.BlockSpec((C_V_DIM, 1), lambda b, i: (0, 0)),
        ],
        out_specs=pl.BlockSpec((None, rows, C_WIDTH), lambda b, i: (b, i, 0)),
        out_shape=jax.ShapeDtypeStruct((bsz, seq, C_WIDTH), _BF16),
        scratch_shapes=[
            pltpu.VMEM((C_HEADS, 2 * C_QK_DIM, 2 * t), _BF16),
            pltpu.VMEM((2, C_HEADS, t, 2 * t), _F32),
            pltpu.VMEM((C_HEADS, 1, 2 * t), _F32),
            pltpu.VMEM((C_HEADS, C_V_DIM + SUM_ROWS, 2 * t), _F32),
            pltpu.VMEM((C_HEADS, t, 2 * t), _F32),
            pltpu.VMEM((C_HEADS, t, 2 * t), _BF16),
            pltpu.VMEM((C_HEADS, 1, 2 * t), _F32),
        ],
        compiler_params=pltpu.CompilerParams(
            dimension_semantics=("arbitrary", "arbitrary"),
            vmem_limit_bytes=VMEM_LIMIT_BYTES),
        name="diff_attention",
    )(rel_table, q, k, vt, lam_params, subln_g)


def _out_mlp_kernel(n_cast, h_ref, ya_ref, yb_ref, g_ref, w_out_ref, w1_ref, w2_ref, *rest):
    cast_in, o_ref, cast_out = rest[:n_cast], rest[n_cast], rest[n_cast + 1:]
    _cast_slabs(cast_in + cast_out)
    half = D_MODEL // 2
    sub = MLP_ROW_TILE // MLP_SUBTILES
    rows = [slice(r * sub, (r + 1) * sub) for r in range(MLP_SUBTILES)]
    ys = [jnp.dot(ya_ref[rs, :], w_out_ref[:half, :], preferred_element_type=_F32)
          + jnp.dot(yb_ref[rs, :], w_out_ref[half:, :], preferred_element_type=_F32)
          for rs in rows]
    h1s = [h_ref[rs, :] + _rms(y, g_ref[1:2, :]) for rs, y in zip(rows, ys)]
    hns = [_rms(h1, g_ref[2:3, :]).astype(_BF16) for h1 in h1s]
    for rs, h1, hn in zip(rows, h1s, hns):
        acc = jnp.zeros((sub, D_MODEL), _F32)
        for c in range(D_FF // FF_CHUNK):
            cs = slice(c * FF_CHUNK, (c + 1) * FF_CHUNK)
            a = jnp.dot(hn, w1_ref[:, cs], preferred_element_type=_F32)
            a = jnp.square(jnp.maximum(a, 0.0)).astype(_BF16)
            acc = acc + jnp.dot(a, w2_ref[cs, :], preferred_element_type=_F32)
        o_ref[rs, :] = h1 + _rms(acc, g_ref[3:4, :])


def _out_mlp(h, ya, yb, g, w_out, w1, w2, casts=()):
    bsz, seq, _ = h.shape
    grid = (bsz, seq // MLP_ROW_TILE)
    row_spec = functools.partial(_row_spec, tile=MLP_ROW_TILE)
    cast_in, cast_out, cast_shapes = _cast_plan(casts, grid)
    return pl.pallas_call(
        functools.partial(_out_mlp_kernel, len(casts)),
        grid=grid,
        in_specs=[
            row_spec(D_MODEL),
            row_spec(D_MODEL // 2),
            row_spec(D_MODEL // 2),
            _const_spec((4, D_MODEL)),
            _const_spec((D_MODEL, D_MODEL)),
            _const_spec((D_MODEL, D_FF)),
            _const_spec((D_FF, D_MODEL)),
        ] + cast_in,
        out_specs=[row_spec(D_MODEL)] + cast_out,
        out_shape=[jax.ShapeDtypeStruct(h.shape, h.dtype)] + cast_shapes,
        compiler_params=pltpu.CompilerParams(
            dimension_semantics=("arbitrary", "arbitrary"),
            vmem_limit_bytes=VMEM_LIMIT_BYTES),
        name="out_mlp",
    )(h, ya, yb, g, w_out, w1, w2, *[w for w, _ in casts])


def kernel(x, rel_bias_table, norm_g, even_w_in, even_ln_g, even_ln_b, even_spatial_w,
           even_spatial_b, even_pool_w, even_pool_scale, even_w_out, odd_w_in, odd_lambda,
           odd_subln_g, odd_conv_w, odd_w_out, ffn_w1, ffn_w2):
    depth = norm_g.shape[0]
    bf = lambda w: w.astype(_BF16)

    def layer_weights(layer):
        mixer = (even_w_in, even_w_out) if layer % 2 == 0 else (odd_w_in, odd_w_out)
        return [(mixer[0], layer // 2), (mixer[1], layer // 2),
                (ffn_w1, layer), (ffn_w2, layer)]

    first = layer_weights(0)
    w_in = bf(first[0][0][first[0][1]])
    w_out = w1 = w2 = None
    h = x
    for layer in range(depth):
        g = norm_g[layer]
        own_casts = [] if w_out is not None else layer_weights(layer)[1:]
        if layer % 2 == 0:
            e = layer // 2
            ya, yb, *cast = _even_mixer(
                h, g[0:1], w_in, even_ln_g[e][None], even_ln_b[e][None],
                even_spatial_w[e], even_spatial_b[e].T, bf(even_pool_w[e]),
                even_pool_scale[e][None], own_casts)
        else:
            o = layer // 2
            lambda_init = 0.8 - 0.6 * math.exp(-0.3 * layer)
            q, k, vt, yb = _odd_proj(h, g[0:1], w_in, odd_conv_w[o])
            ya = _diff_attention(q, k, vt, rel_bias_table, odd_lambda[o],
                                 odd_subln_g[o][:, None], lambda_init)
            cast = [bf(w[i]) for w, i in own_casts]
        if own_casts:
            w_out, w1, w2 = cast
        next_casts = layer_weights(layer + 1) if layer + 1 < depth else []
        h, *nxt = _out_mlp(h, ya, yb, g, w_out, w1, w2, next_casts)
        w_in, w_out, w1, w2 = nxt if nxt else (None,) * 4
    return h
```

```python
import functools
import math

import numpy as np
import jax
import jax.numpy as jnp
from jax import lax
from jax.experimental import pallas as pl
from jax.experimental.pallas import tpu as pltpu

D_MODEL = 1024
A_WIDTH = 512
A_GROUPS = 4
A_GROUP_DIM = 128
CHUNK = 128
B_WIDTH = 512
POOL_WINDOWS = (2, 4, 8, 16)
B_GROUP_DIM = 128
C_HEADS = 4
C_QK_DIM = 64
C_V_DIM = 128
C_WIDTH = 512
D_WIDTH = 512
CONV_WIDTH = 3
REL_BUCKETS = 32
REL_MAX_DIST = 128
D_FF = 4096
EPS = 1e-6
EVEN_IN = 2 * A_WIDTH + B_WIDTH
ODD_IN = 2 * C_WIDTH + C_WIDTH + 3 * D_WIDTH

ROW_TILE = 1024
ATT_TILE = 256
ATT_TILES_PER_STEP = 4
FF_CHUNK = 1024
PROJ_SUBTILES = 4
PROJ_SLOTS = 3
MLP_ROW_TILE = 1024
MLP_SUBTILES = 4
POOL_HALO = 16
CONV_HALO = 8
LOG2E = math.log2(math.e)
FAR_UNROLL = 4
SCORE_LOOKAHEAD = 1
SUM_ROWS = 16
MASK_VALUE = -1e30
BF16_SUBLANES = 16
VMEM_LIMIT_BYTES = 52 * 1024 * 1024

_F32 = jnp.float32
_BF16 = jnp.bfloat16


def _rms(x, g):
    return x * lax.rsqrt(jnp.mean(x * x, axis=-1, keepdims=True) + EPS) * g


def _gelu_tanh(x):
    c = math.sqrt(2.0 / math.pi)
    hx = 0.5 * x
    return hx + hx * jnp.tanh(x * (c + (c * 0.044715) * (x * x)))


def _norm_project(x_ref, g_ref, w_ref, proj_ref):
    rows = x_ref.shape[0] // PROJ_SUBTILES
    slots = proj_ref.shape[0]
    ahead = slots - 1

    def project(r):
        hn = _rms(x_ref[r * rows:(r + 1) * rows, :], g_ref[...]).astype(_BF16)
        proj_ref[r % slots] = jnp.dot(hn, w_ref[...], preferred_element_type=_F32)

    for r in range(min(ahead, PROJ_SUBTILES)):
        project(r)
    for r in range(PROJ_SUBTILES):
        if r + ahead < PROJ_SUBTILES:
            project(r + ahead)
        yield r, proj_ref.at[r % slots]


def _const_spec(shape):
    nd = len(shape)
    return pl.BlockSpec(shape, lambda *_: (0,) * nd, pipeline_mode=pl.Buffered(1))


def _row_spec(width, col=0, tile=None):
    return pl.BlockSpec((None, tile or ROW_TILE, width), lambda b, j: (b, j, col))


def _cast_plan(weights, grid):
    steps = grid[0] * grid[1]
    in_specs, out_specs, out_shapes = [], [], []
    for w, layer in weights:
        _, rows, cols = w.shape
        slab = rows // steps
        assert slab * steps == rows and slab % BF16_SUBLANES == 0
        in_specs.append(pl.BlockSpec(
            (None, slab, cols), lambda b, j, layer=layer: (layer, b * grid[1] + j, 0)))
        out_specs.append(pl.BlockSpec((slab, cols), lambda b, j: (b * grid[1] + j, 0)))
        out_shapes.append(jax.ShapeDtypeStruct((rows, cols), _BF16))
    return in_specs, out_specs, out_shapes


def _cast_slabs(refs):
    n = len(refs) // 2
    for src, dst in zip(refs[:n], refs[n:]):
        dst[...] = src[...].astype(_BF16)


def _even_mixer_kernel(n_cast, x_ref, g_ref, w_in_ref, ln_g_ref, ln_b_ref, ws_ref, bs_ref,
                       pw_ref, ps_ref, *rest):
    cast_in, (ya_ref, yb_ref), rest = rest[:n_cast], rest[n_cast:n_cast + 2], rest[n_cast + 2:]
    cast_out, (carry_ref, proj_ref) = rest[:n_cast], rest[n_cast:]
    _cast_slabs(cast_in + cast_out)
    j = pl.program_id(1)
    rows = ROW_TILE // PROJ_SUBTILES

    @pl.when(j == 0)
    def _():
        carry_ref[...] = jnp.zeros_like(carry_ref)

    row = lax.broadcasted_iota(jnp.int32, (CHUNK, CHUNK), 0)
    col = lax.broadcasted_iota(jnp.int32, (CHUNK, CHUNK), 1)
    tril_w = [jnp.where(row >= col, ws_ref[g], 0.0).astype(_BF16) for g in range(A_GROUPS)]
    halo = carry_ref[...]
    for sub, proj in _norm_project(x_ref, g_ref, w_in_ref, proj_ref):
        base = sub * rows

        z = _gelu_tanh(proj[:, :2 * A_WIDTH])
        u = z[:, :A_WIDTH]
        v = z[:, A_WIDTH:]
        mu = jnp.mean(v, axis=-1, keepdims=True)
        vc = v - mu
        vn = vc * lax.rsqrt(jnp.mean(vc * vc, axis=-1, keepdims=True) + EPS)
        vn = (vn * ln_g_ref[...] + ln_b_ref[...]).astype(_BF16)
        for g in range(A_GROUPS):
            cs = slice(g * A_GROUP_DIM, (g + 1) * A_GROUP_DIM)
            bias = bs_ref[:, g:g + 1]
            for c in range(0, rows // CHUNK, 2):
                r0 = slice(c * CHUNK, (c + 1) * CHUNK)
                r1 = slice((c + 1) * CHUNK, (c + 2) * CHUNK)
                rhs = jnp.concatenate([vn[r0, cs], vn[r1, cs]], axis=1)
                mixed = jnp.dot(tril_w[g], rhs, preferred_element_type=_F32) + bias
                ya_ref[base + r0.start:base + r0.stop, cs] = (
                    u[r0, cs] * mixed[:, :A_GROUP_DIM]).astype(_BF16)
                ya_ref[base + r1.start:base + r1.stop, cs] = (
                    u[r1, cs] * mixed[:, A_GROUP_DIM:]).astype(_BF16)

        p = proj[:, 2 * A_WIDTH:]
        cur = jnp.concatenate([halo, p], axis=0)
        halo = p[rows - POOL_HALO:, :]
        cur_win = 1
        pos = j * ROW_TILE + base + lax.broadcasted_iota(jnp.int32, (rows, B_GROUP_DIM), 0)
        tokens_so_far = (pos + 1).astype(_F32)
        for g, win in enumerate(POOL_WINDOWS):
            cs = slice(g * B_GROUP_DIM, (g + 1) * B_GROUP_DIM)
            while cur_win < win:
                cur = cur + pltpu.roll(cur, cur_win, axis=0)
                cur_win *= 2
            assert cur_win == win and win <= POOL_HALO
            count = jnp.minimum(tokens_so_far, float(win))
            pooled = (cur[POOL_HALO:, :B_GROUP_DIM] / count - p[:, cs]).astype(_BF16)
            y = jnp.dot(pooled, pw_ref[g], preferred_element_type=_F32)
            yb_ref[base:base + rows, cs] = (y * ps_ref[:, cs]).astype(_BF16)
            cur = cur[:, B_GROUP_DIM:]
    carry_ref[...] = halo


def _even_mixer(h, g0, w_in, ln_g, ln_b, w_s, b_s_t, pool_w, pool_scale, casts):
    bsz, seq, _ = h.shape
    grid = (bsz, seq // ROW_TILE)
    out = jax.ShapeDtypeStruct((bsz, seq, A_WIDTH), _BF16)
    cast_in, cast_out, cast_shapes = _cast_plan(casts, grid)
    return pl.pallas_call(
        functools.partial(_even_mixer_kernel, len(casts)),
        grid=grid,
        in_specs=[
            _row_spec(D_MODEL),
            _const_spec((1, D_MODEL)),
            _const_spec((D_MODEL, EVEN_IN)),
            _const_spec((1, A_WIDTH)),
            _const_spec((1, A_WIDTH)),
            _const_spec((A_GROUPS, CHUNK, CHUNK)),
            _const_spec((CHUNK, A_GROUPS)),
            _const_spec((len(POOL_WINDOWS), B_GROUP_DIM, B_GROUP_DIM)),
            _const_spec((1, B_WIDTH)),
        ] + cast_in,
        out_specs=[_row_spec(A_WIDTH), _row_spec(B_WIDTH)] + cast_out,
        out_shape=[out, out] + cast_shapes,
        scratch_shapes=[
            pltpu.VMEM((POOL_HALO, B_WIDTH), _F32),
            pltpu.VMEM((PROJ_SLOTS, ROW_TILE // PROJ_SUBTILES, EVEN_IN), _F32),
        ],
        compiler_params=pltpu.CompilerParams(
            dimension_semantics=("arbitrary", "arbitrary"),
            vmem_limit_bytes=VMEM_LIMIT_BYTES),
        name="even_mixer",
    )(h, g0, w_in, ln_g, ln_b, w_s, b_s_t, pool_w, pool_scale, *[w for w, _ in casts])


def _odd_proj_kernel(x_ref, g_ref, w_in_ref, cw_ref, q_ref, k_ref, vt_ref, yd_ref,
                     carry_ref, proj_ref):
    j = pl.program_id(1)
    rows = ROW_TILE // PROJ_SUBTILES

    @pl.when(j == 0)
    def _():
        carry_ref[...] = jnp.zeros_like(carry_ref)

    halo = carry_ref[...]
    for sub, proj in _norm_project(x_ref, g_ref, w_in_ref, proj_ref):
        rs = slice(sub * rows, (sub + 1) * rows)
        q_ref[rs, :] = (proj[:, :C_WIDTH] * (C_QK_DIM ** -0.5 * LOG2E)).astype(_BF16)
        k_ref[rs, :] = proj[:, C_WIDTH:2 * C_WIDTH].astype(_BF16)
        for kt in range(rows // ATT_TILE):
            v_tile = proj[kt * ATT_TILE:(kt + 1) * ATT_TILE, 2 * C_WIDTH:3 * C_WIDTH]
            vt_ref[sub * (rows // ATT_TILE) + kt] = v_tile.T.astype(_BF16)
        o = 3 * C_WIDTH
        bg = proj[:, o:o + D_WIDTH]
        z = proj[:, o + D_WIDTH:o + 2 * D_WIDTH] * proj[:, o + 2 * D_WIDTH:]
        ext = jnp.concatenate([halo, z], axis=0)
        halo = z[rows - CONV_HALO:, :]
        y = cw_ref[CONV_WIDTH - 1:CONV_WIDTH, :] * z
        for t in range(CONV_WIDTH - 1):
            shift = CONV_WIDTH - 1 - t
            y = y + cw_ref[t:t + 1, :] * pltpu.roll(ext, shift, axis=0)[CONV_HALO:, :]
        yd_ref[rs, :] = (bg * y).astype(_BF16)
    carry_ref[...] = halo


def _odd_proj(h, g0, w_in, conv_w):
    bsz, seq, _ = h.shape
    grid = (bsz, seq // ROW_TILE)
    out = jax.ShapeDtypeStruct((bsz, seq, C_WIDTH), _BF16)
    assert (ROW_TILE // PROJ_SUBTILES) % ATT_TILE == 0
    out_t = jax.ShapeDtypeStruct((bsz, seq // ATT_TILE, C_WIDTH, ATT_TILE), _BF16)
    spec_t = pl.BlockSpec((None, ROW_TILE // ATT_TILE, C_WIDTH, ATT_TILE),
                          lambda b, j: (b, j, 0, 0))
    return pl.pallas_call(
        _odd_proj_kernel,
        grid=grid,
        in_specs=[
            _row_spec(D_MODEL),
            _const_spec((1, D_MODEL)),
            _const_spec((D_MODEL, ODD_IN)),
            _const_spec((CONV_WIDTH, D_WIDTH)),
        ],
        out_specs=[_row_spec(C_WIDTH), _row_spec(C_WIDTH), spec_t, _row_spec(D_WIDTH)],
        out_shape=[out, out, out_t, out],
        scratch_shapes=[
            pltpu.VMEM((CONV_HALO, D_WIDTH), _F32),
            pltpu.VMEM((PROJ_SLOTS, ROW_TILE // PROJ_SUBTILES, ODD_IN), _F32),
        ],
        compiler_params=pltpu.CompilerParams(
            dimension_semantics=("arbitrary", "arbitrary"),
            vmem_limit_bytes=VMEM_LIMIT_BYTES),
        name="odd_proj",
    )(h, g0, w_in, conv_w)


def _t5_bucket_upper_bounds(max_dist):
    d = np.arange(max_dist, dtype=np.int32)
    max_exact = REL_BUCKETS // 2
    nf = np.maximum(d, 1).astype(np.float32)
    large = max_exact + (np.log(nf / np.float32(max_exact))
                         / np.float32(math.log(REL_MAX_DIST / max_exact))
                         * np.float32(REL_BUCKETS - max_exact)).astype(np.int32)
    large = np.minimum(large, REL_BUCKETS - 1)
    bucket = np.where(d < max_exact, d, large)
    assert np.all(np.diff(bucket) >= 0)
    return bucket, {int(b): int(d[bucket == b].max()) for b in np.unique(bucket)}


def _attn_kernel(bucket_hi, lambda_init,
                 tab_ref, q_ref, k_ref, vt_ref, lam_ref, sg_ref, o_ref,
                 qq_ref, bias_ref, *scratch):
    t = ATT_TILE
    blk = pl.program_id(1)
    last_bucket = REL_BUCKETS - 1

    @pl.when(blk == 0)
    def _():
        kpos = lax.broadcasted_iota(jnp.int32, (t, t), 0)
        qpos = lax.broadcasted_iota(jnp.int32, (t, t), 1)
        for delta in range(2):
            d = qpos - kpos + delta * t
            for hd in range(C_HEADS):
                far = tab_ref[last_bucket, hd]
                val = jnp.zeros((t, t), _F32)
                for b in sorted(bucket_hi, reverse=True):
                    if b == last_bucket:
                        continue
                    val = jnp.where(d <= bucket_hi[b], (tab_ref[b, hd] - far) * LOG2E, val)
                val = jnp.where(d >= 0, val, MASK_VALUE)
                bias_ref[delta, hd] = jnp.concatenate([val, val], axis=1)

    def query_tile(sub, carry):
        rows = pl.ds(pl.multiple_of(sub * t, t), t)
        _attn_query_tile(lambda_init, blk * ATT_TILES_PER_STEP + sub,
                         q_ref.at[rows], k_ref, vt_ref, lam_ref, sg_ref, o_ref.at[rows],
                         qq_ref, bias_ref, *scratch)
        return carry

    lax.fori_loop(0, ATT_TILES_PER_STEP, query_tile, 0)


def _attn_query_tile(lambda_init, i, q_ref, k_ref, vt_ref, lam_ref, sg_ref, o_ref,
                     qq_ref, bias_ref, m_ref, acc_ref, s_ref, p_ref, al_ref):
    t = ATT_TILE
    hd_dim = 2 * C_QK_DIM
    heads = range(C_HEADS)

    feat = lax.broadcasted_iota(jnp.int32, (hd_dim, t), 0)
    for hd in heads:
        qt = q_ref[:, hd * hd_dim:(hd + 1) * hd_dim].astype(_F32).T
        qq_ref[hd] = jnp.concatenate(
            [jnp.where(feat < C_QK_DIM, qt, 0.0), jnp.where(feat >= C_QK_DIM, qt, 0.0)],
            axis=1).astype(_BF16)
    m_ref[...] = jnp.full_like(m_ref, MASK_VALUE)
    acc_ref[...] = jnp.zeros_like(acc_ref)
    p_ref[C_HEADS - 1] = jnp.zeros((t, 2 * t), _BF16)
    al_ref[C_HEADS - 1] = jnp.ones((1, 2 * t), _F32)

    def scores(jk, hd):
        start = pl.multiple_of(jk * t, t)
        k_t = k_ref[pl.ds(start, t), hd * hd_dim:(hd + 1) * hd_dim]
        s_ref[hd] = jnp.dot(k_t, qq_ref[hd], preferred_element_type=_F32)

    def values(jk, hd):
        vt = vt_ref[jk, hd * C_V_DIM:(hd + 1) * C_V_DIM, :]
        vt_ones = jnp.concatenate([vt, jnp.ones((SUM_ROWS, t), _BF16)], axis=0)
        acc_ref[hd] = al_ref[hd] * acc_ref[hd] + jnp.dot(
            vt_ones, p_ref[hd], preferred_element_type=_F32)

    def step(jk, delta, jk_next):
        for hd in heads:
            ahead = hd + SCORE_LOOKAHEAD
            if ahead < C_HEADS:
                scores(jk, ahead)
            elif jk_next is not None:
                scores(jk_next, ahead - C_HEADS)
            s = s_ref[hd]
            if delta is not None:
                s = s + bias_ref[delta, hd]
            m_prev = m_ref[hd]
            m_next = jnp.maximum(m_prev, jnp.max(s, axis=0, keepdims=True))
            alpha = jnp.exp2(m_prev - m_next)
            p = jnp.exp2(s - m_next)
            m_ref[hd] = m_next
            if hd == 0:
                values(jnp.maximum(jk - 1, 0), C_HEADS - 1)
            else:
                values(jk, hd - 1)
            p_ref[hd] = p.astype(_BF16)
            al_ref[hd] = alpha

    for hd in range(SCORE_LOOKAHEAD):
        scores(0, hd)

    n_far = jnp.maximum(i - 1, 0)

    def far_run(first, count):
        for n in range(count):
            step(first + n, None, first + n + 1)

    def far_trip(r, carry):
        far_run(FAR_UNROLL * r, FAR_UNROLL)
        return carry

    trips = n_far // FAR_UNROLL
    lax.fori_loop(0, trips, far_trip, 0)
    done = FAR_UNROLL * trips
    run = FAR_UNROLL // 2
    while run >= 2:
        take = (n_far - done) >= run
        pl.when(take)(functools.partial(far_run, done, run))
        done = done + jnp.where(take, run, 0)
        run //= 2

    def finish(odd_far_tile, sub_diagonal):
        if odd_far_tile:
            step(n_far - 1, None, n_far)
        if sub_diagonal:
            step(i - 1, 1, i)
        step(i, 0, None)
        values(i, C_HEADS - 1)
        lp = lam_ref[...]
        lam = (jnp.exp(jnp.sum(lp[0:1] * lp[1:2], axis=-1, keepdims=True))
               - jnp.exp(jnp.sum(lp[2:3] * lp[3:4], axis=-1, keepdims=True)) + lambda_init)
        for hd in heads:
            inv_l = 1.0 / acc_ref[hd, C_V_DIM:C_V_DIM + 1, :]
            a = (acc_ref[hd, :C_V_DIM, :t] * inv_l[:, :t]
                 - acc_ref[hd, :C_V_DIM, t:] * (lam * inv_l[:, t:]))
            ms = jnp.mean(a * a, axis=0, keepdims=True)
            y = a * (lax.rsqrt(ms + EPS) * (1.0 - lambda_init)) * sg_ref[...]
            o_ref[:, hd * C_V_DIM:(hd + 1) * C_V_DIM] = y.T.astype(_BF16)

    odd_far = n_far % 2 == 1
    pl.when(i == 0)(lambda: finish(False, False))
    pl.when(jnp.logical_and(i >= 1, jnp.logical_not(odd_far)))(lambda: finish(False, True))
    pl.when(odd_far)(lambda: finish(True, True))


def _diff_attention(q, k, vt, rel_table, lam_params, subln_g, lambda_init):
    bsz, seq, _ = q.shape
    t = ATT_TILE
    bucket, bucket_hi = _t5_bucket_upper_bounds(seq)
    assert np.all(bucket[t + 1:] == REL_BUCKETS - 1)
    rows = t * ATT_TILES_PER_STEP
    grid = (bsz, seq // rows)
    kernel = functools.partial(_attn_kernel, bucket_hi, lambda_init)
    return pl.pallas_call(
        kernel,
        grid=grid,
        in_specs=[
            pl.BlockSpec(memory_space=pltpu.SMEM),
            pl.BlockSpec((None, rows, C_WIDTH), lambda b, i: (b, i, 0)),
            pl.BlockSpec((None, seq, C_WIDTH), lambda b, i: (b, 0, 0)),
            pl.BlockSpec((None, seq // t, C_WIDTH, t), lambda b, i: (b, 0, 0, 0)),
            pl.BlockSpec((4, C_QK_DIM), lambda b, i: (0, 0)),
            pl.BlockSpec((C_V_DIM, 1), lambda b, i: (0, 0)),
        ],
        out_specs=pl.BlockSpec((None, rows, C_WIDTH), lambda b, i: (b, i, 0)),
        out_shape=jax.ShapeDtypeStruct((bsz, seq, C_WIDTH), _BF16),
        scratch_shapes=[
            pltpu.VMEM((C_HEADS, 2 * C_QK_DIM, 2 * t), _BF16),
            pltpu.VMEM((2, C_HEADS, t, 2 * t), _F32),
            pltpu.VMEM((C_HEADS, 1, 2 * t), _F32),
            pltpu.VMEM((C_HEADS, C_V_DIM + SUM_ROWS, 2 * t), _F32),
            pltpu.VMEM((C_HEADS, t, 2 * t), _F32),
            pltpu.VMEM((C_HEADS, t, 2 * t), _BF16),
            pltpu.VMEM((C_HEADS, 1, 2 * t), _F32),
        ],
        compiler_params=pltpu.CompilerParams(
            dimension_semantics=("arbitrary", "arbitrary"),
            vmem_limit_bytes=VMEM_LIMIT_BYTES),
        name="diff_attention",
    )(rel_table, q, k, vt, lam_params, subln_g)


def _out_mlp_kernel(n_cast, h_ref, ya_ref, yb_ref, g_ref, w_out_ref, w1_ref, w2_ref, *rest):
    cast_in, o_ref, cast_out = rest[:n_cast], rest[n_cast], rest[n_cast + 1:]
    _cast_slabs(cast_in + cast_out)
    half = D_MODEL // 2
    sub = MLP_ROW_TILE // MLP_SUBTILES
    rows = [slice(r * sub, (r + 1) * sub) for r in range(MLP_SUBTILES)]
    ys = [jnp.dot(ya_ref[rs, :], w_out_ref[:half, :], preferred_element_type=_F32)
          + jnp.dot(yb_ref[rs, :], w_out_ref[half:, :], preferred_element_type=_F32)
          for rs in rows]
    h1s = [h_ref[rs, :] + _rms(y, g_ref[1:2, :]) for rs, y in zip(rows, ys)]
    hns = [_rms(h1, g_ref[2:3, :]).astype(_BF16) for h1 in h1s]
    for rs, h1, hn in zip(rows, h1s, hns):
        acc = jnp.zeros((sub, D_MODEL), _F32)
        for c in range(D_FF // FF_CHUNK):
            cs = slice(c * FF_CHUNK, (c + 1) * FF_CHUNK)
            a = jnp.dot(hn, w1_ref[:, cs], preferred_element_type=_F32)
            a = jnp.square(jnp.maximum(a, 0.0)).astype(_BF16)
            acc = acc + jnp.dot(a, w2_ref[cs, :], preferred_element_type=_F32)
        o_ref[rs, :] = h1 + _rms(acc, g_ref[3:4, :])


def _out_mlp(h, ya, yb, g, w_out, w1, w2, casts=()):
    bsz, seq, _ = h.shape
    grid = (bsz, seq // MLP_ROW_TILE)
    row_spec = functools.partial(_row_spec, tile=MLP_ROW_TILE)
    cast_in, cast_out, cast_shapes = _cast_plan(casts, grid)
    return pl.pallas_call(
        functools.partial(_out_mlp_kernel, len(casts)),
        grid=grid,
        in_specs=[
            row_spec(D_MODEL),
            row_spec(D_MODEL // 2),
            row_spec(D_MODEL // 2),
            _const_spec((4, D_MODEL)),
            _const_spec((D_MODEL, D_MODEL)),
            _const_spec((D_MODEL, D_FF)),
            _const_spec((D_FF, D_MODEL)),
        ] + cast_in,
        out_specs=[row_spec(D_MODEL)] + cast_out,
        out_shape=[jax.ShapeDtypeStruct(h.shape, h.dtype)] + cast_shapes,
        compiler_params=pltpu.CompilerParams(
            dimension_semantics=("arbitrary", "arbitrary"),
            vmem_limit_bytes=VMEM_LIMIT_BYTES),
        name="out_mlp",
    )(h, ya, yb, g, w_out, w1, w2, *[w for w, _ in casts])


def kernel(x, rel_bias_table, norm_g, even_w_in, even_ln_g, even_ln_b, even_spatial_w,
           even_spatial_b, even_pool_w, even_pool_scale, even_w_out, odd_w_in, odd_lambda,
           odd_subln_g, odd_conv_w, odd_w_out, ffn_w1, ffn_w2):
    depth = norm_g.shape[0]
    bf = lambda w: w.astype(_BF16)

    def layer_weights(layer):
        mixer = (even_w_in, even_w_out) if layer % 2 == 0 else (odd_w_in, odd_w_out)
        return [(mixer[0], layer // 2), (mixer[1], layer // 2),
                (ffn_w1, layer), (ffn_w2, layer)]

    first = layer_weights(0)
    w_in = bf(first[0][0][first[0][1]])
    w_out = w1 = w2 = None
    h = x
    for layer in range(depth):
        g = norm_g[layer]
        own_casts = [] if w_out is not None else layer_weights(layer)[1:]
        if layer % 2 == 0:
            e = layer // 2
            ya, yb, *cast = _even_mixer(
                h, g[0:1], w_in, even_ln_g[e][None], even_ln_b[e][None],
                even_spatial_w[e], even_spatial_b[e].T, bf(even_pool_w[e]),
                even_pool_scale[e][None], own_casts)
        else:
            o = layer // 2
            lambda_init = 0.8 - 0.6 * math.exp(-0.3 * layer)
            q, k, vt, yb = _odd_proj(h, g[0:1], w_in, odd_conv_w[o])
            ya = _diff_attention(q, k, vt, rel_bias_table, odd_lambda[o],
                                 odd_subln_g[o][:, None], lambda_init)
            cast = [bf(w[i]) for w, i in own_casts]
        if own_casts:
            w_out, w1, w2 = cast
        next_casts = layer_weights(layer + 1) if layer + 1 < depth else []
        h, *nxt = _out_mlp(h, ya, yb, g, w_out, w1, w2, next_casts)
        w_in, w_out, w1, w2 = nxt if nxt else (None,) * 4
    return h
```

```python
import functools
import math

import numpy as np
import jax
import jax.numpy as jnp
from jax import lax
from jax.experimental import pallas as pl
from jax.experimental.pallas import tpu as pltpu

D_MODEL = 1024
A_WIDTH = 512
A_GROUPS = 4
A_GROUP_DIM = 128
CHUNK = 128
B_WIDTH = 512
POOL_WINDOWS = (2, 4, 8, 16)
B_GROUP_DIM = 128
C_HEADS = 4
C_QK_DIM = 64
C_V_DIM = 128
C_WIDTH = 512
D_WIDTH = 512
CONV_WIDTH = 3
REL_BUCKETS = 32
REL_MAX_DIST = 128
D_FF = 4096
EPS = 1e-6
EVEN_IN = 2 * A_WIDTH + B_WIDTH
ODD_IN = 2 * C_WIDTH + C_WIDTH + 3 * D_WIDTH

ROW_TILE = 1024
PROJ_SUBTILES = 4
PROJ_SLOTS = 3
MLP_ROW_TILE = 1024
MLP_SUBTILES = 4
FF_CHUNK = 1024
ATT_TILE = 256
ATT_TILES_PER_STEP = 4
FAR_UNROLL = 4
SCORE_LOOKAHEAD = 1
SUM_ROWS = 16
POOL_HALO = 16
CONV_HALO = 8
LOG2E = math.log2(math.e)
MASK_VALUE = -1e30
BF16_SUBLANES = 16
VMEM_LIMIT_BYTES = 52 * 1024 * 1024

_F32 = jnp.float32
_BF16 = jnp.bfloat16


def _rms(x, g):
    return x * lax.rsqrt(jnp.mean(x * x, axis=-1, keepdims=True) + EPS) * g


def _gelu_tanh(x):
    c = math.sqrt(2.0 / math.pi)
    hx = 0.5 * x
    return hx + hx * jnp.tanh(x * (c + (c * 0.044715) * (x * x)))


def _norm_project(x_ref, g_ref, w_ref, proj_ref):
    rows = x_ref.shape[0] // PROJ_SUBTILES
    slots = proj_ref.shape[0]
    ahead = slots - 1

    def project(r):
        hn = _rms(x_ref[r * rows:(r + 1) * rows, :], g_ref[...]).astype(_BF16)
        proj_ref[r % slots] = jnp.dot(hn, w_ref[...], preferred_element_type=_F32)

    for r in range(min(ahead, PROJ_SUBTILES)):
        project(r)
    for r in range(PROJ_SUBTILES):
        if r + ahead < PROJ_SUBTILES:
            project(r + ahead)
        yield r, proj_ref.at[r % slots]


def _const_spec(shape):
    nd = len(shape)
    return pl.BlockSpec(shape, lambda *_: (0,) * nd, pipeline_mode=pl.Buffered(1))


def _row_spec(width, col=0, tile=None):
    return pl.BlockSpec((None, tile or ROW_TILE, width), lambda b, j: (b, j, col))


def _cast_plan(weights, grid):
    steps = grid[0] * grid[1]
    in_specs, out_specs, out_shapes = [], [], []
    for w, layer in weights:
        _, rows, cols = w.shape
        slab = rows // steps
        assert slab * steps == rows and slab % BF16_SUBLANES == 0
        in_specs.append(pl.BlockSpec(
            (None, slab, cols), lambda b, j, layer=layer: (layer, b * grid[1] + j, 0)))
        out_specs.append(pl.BlockSpec((slab, cols), lambda b, j: (b * grid[1] + j, 0)))
        out_shapes.append(jax.ShapeDtypeStruct((rows, cols), _BF16))
    return in_specs, out_specs, out_shapes


def _cast_slabs(refs):
    n = len(refs) // 2
    for src, dst in zip(refs[:n], refs[n:]):
        dst[...] = src[...].astype(_BF16)


def _even_mixer_kernel(n_cast, x_ref, g_ref, w_in_ref, ln_g_ref, ln_b_ref, ws_ref, bs_ref,
                       pw_ref, ps_ref, *rest):
    cast_in, (ya_ref, yb_ref), rest = rest[:n_cast], rest[n_cast:n_cast + 2], rest[n_cast + 2:]
    cast_out, (carry_ref, proj_ref) = rest[:n_cast], rest[n_cast:]
    _cast_slabs(cast_in + cast_out)
    j = pl.program_id(1)
    rows = ROW_TILE // PROJ_SUBTILES

    @pl.when(j == 0)
    def _():
        carry_ref[...] = jnp.zeros_like(carry_ref)

    row = lax.broadcasted_iota(jnp.int32, (CHUNK, CHUNK), 0)
    col = lax.broadcasted_iota(jnp.int32, (CHUNK, CHUNK), 1)
    tril_w = [jnp.where(row >= col, ws_ref[g], 0.0).astype(_BF16) for g in range(A_GROUPS)]
    halo = carry_ref[...]
    for sub, proj in _norm_project(x_ref, g_ref, w_in_ref, proj_ref):
        base = sub * rows

        z = _gelu_tanh(proj[:, :2 * A_WIDTH])
        u = z[:, :A_WIDTH]
        v = z[:, A_WIDTH:]
        mu = jnp.mean(v, axis=-1, keepdims=True)
        vc = v - mu
        vn = vc * lax.rsqrt(jnp.mean(vc * vc, axis=-1, keepdims=True) + EPS)
        vn = (vn * ln_g_ref[...] + ln_b_ref[...]).astype(_BF16)
        for g in range(A_GROUPS):
            cs = slice(g * A_GROUP_DIM, (g + 1) * A_GROUP_DIM)
            bias = bs_ref[:, g:g + 1]
            for c in range(0, rows // CHUNK, 2):
                r0 = slice(c * CHUNK, (c + 1) * CHUNK)
                r1 = slice((c + 1) * CHUNK, (c + 2) * CHUNK)
                rhs = jnp.concatenate([vn[r0, cs], vn[r1, cs]], axis=1)
                mixed = jnp.dot(tril_w[g], rhs, preferred_element_type=_F32) + bias
                ya_ref[base + r0.start:base + r0.stop, cs] = (
                    u[r0, cs] * mixed[:, :A_GROUP_DIM]).astype(_BF16)
                ya_ref[base + r1.start:base + r1.stop, cs] = (
                    u[r1, cs] * mixed[:, A_GROUP_DIM:]).astype(_BF16)

        p = proj[:, 2 * A_WIDTH:]
        cur = jnp.concatenate([halo, p], axis=0)
        halo = p[rows - POOL_HALO:, :]
        cur_win = 1
        pos = j * ROW_TILE + base + lax.broadcasted_iota(jnp.int32, (rows, B_GROUP_DIM), 0)
        tokens_so_far = (pos + 1).astype(_F32)
        for g, win in enumerate(POOL_WINDOWS):
            cs = slice(g * B_GROUP_DIM, (g + 1) * B_GROUP_DIM)
            while cur_win < win:
                cur = cur + pltpu.roll(cur, cur_win, axis=0)
                cur_win *= 2
            assert cur_win == win and win <= POOL_HALO
            count = jnp.minimum(tokens_so_far, float(win))
            pooled = (cur[POOL_HALO:, :B_GROUP_DIM] / count - p[:, cs]).astype(_BF16)
            y = jnp.dot(pooled, pw_ref[g], preferred_element_type=_F32)
            yb_ref[base:base + rows, cs] = (y * ps_ref[:, cs]).astype(_BF16)
            cur = cur[:, B_GROUP_DIM:]
    carry_ref[...] = halo


def _even_mixer(h, g0, w_in, ln_g, ln_b, w_s, b_s_t, pool_w, pool_scale, casts):
    bsz, seq, _ = h.shape
    grid = (bsz, seq // ROW_TILE)
    out = jax.ShapeDtypeStruct((bsz, seq, A_WIDTH), _BF16)
    cast_in, cast_out, cast_shapes = _cast_plan(casts, grid)
    return pl.pallas_call(
        functools.partial(_even_mixer_kernel, len(casts)),
        grid=grid,
        in_specs=[
            _row_spec(D_MODEL),
            _const_spec((1, D_MODEL)),
            _const_spec((D_MODEL, EVEN_IN)),
            _const_spec((1, A_WIDTH)),
            _const_spec((1, A_WIDTH)),
            _const_spec((A_GROUPS, CHUNK, CHUNK)),
            _const_spec((CHUNK, A_GROUPS)),
            _const_spec((len(POOL_WINDOWS), B_GROUP_DIM, B_GROUP_DIM)),
            _const_spec((1, B_WIDTH)),
        ] + cast_in,
        out_specs=[_row_spec(A_WIDTH), _row_spec(B_WIDTH)] + cast_out,
        out_shape=[out, out] + cast_shapes,
        scratch_shapes=[
            pltpu.VMEM((POOL_HALO, B_WIDTH), _F32),
            pltpu.VMEM((PROJ_SLOTS, ROW_TILE // PROJ_SUBTILES, EVEN_IN), _F32),
        ],
        compiler_params=pltpu.CompilerParams(
            dimension_semantics=("arbitrary", "arbitrary"),
            vmem_limit_bytes=VMEM_LIMIT_BYTES),
        name="even_mixer",
    )(h, g0, w_in, ln_g, ln_b, w_s, b_s_t, pool_w, pool_scale, *[w for w, _ in casts])


def _odd_proj_kernel(x_ref, g_ref, w_in_ref, cw_ref, qt_ref, k_ref, vt_ref, yd_ref,
                     carry_ref, proj_ref):
    j = pl.program_id(1)
    rows = ROW_TILE // PROJ_SUBTILES

    @pl.when(j == 0)
    def _():
        carry_ref[...] = jnp.zeros_like(carry_ref)

    halo = carry_ref[...]
    for sub, proj in _norm_project(x_ref, g_ref, w_in_ref, proj_ref):
        rs = slice(sub * rows, (sub + 1) * rows)
        k_ref[rs, :] = proj[:, C_WIDTH:2 * C_WIDTH].astype(_BF16)
        for kt in range(rows // ATT_TILE):
            tile = slice(kt * ATT_TILE, (kt + 1) * ATT_TILE)
            idx = sub * (rows // ATT_TILE) + kt
            q_tile = proj[tile, :C_WIDTH] * (C_QK_DIM ** -0.5 * LOG2E)
            qt_ref[idx] = q_tile.T.astype(_BF16)
            vt_ref[idx] = proj[tile, 2 * C_WIDTH:3 * C_WIDTH].T.astype(_BF16)
        o = 3 * C_WIDTH
        bg = proj[:, o:o + D_WIDTH]
        z = proj[:, o + D_WIDTH:o + 2 * D_WIDTH] * proj[:, o + 2 * D_WIDTH:]
        ext = jnp.concatenate([halo, z], axis=0)
        halo = z[rows - CONV_HALO:, :]
        y = cw_ref[CONV_WIDTH - 1:CONV_WIDTH, :] * z
        for t in range(CONV_WIDTH - 1):
            shift = CONV_WIDTH - 1 - t
            y = y + cw_ref[t:t + 1, :] * pltpu.roll(ext, shift, axis=0)[CONV_HALO:, :]
        yd_ref[rs, :] = (bg * y).astype(_BF16)
    carry_ref[...] = halo


def _odd_proj(h, g0, w_in, conv_w):
    bsz, seq, _ = h.shape
    grid = (bsz, seq // ROW_TILE)
    out = jax.ShapeDtypeStruct((bsz, seq, C_WIDTH), _BF16)
    assert (ROW_TILE // PROJ_SUBTILES) % ATT_TILE == 0
    out_t = jax.ShapeDtypeStruct((bsz, seq // ATT_TILE, C_WIDTH, ATT_TILE), _BF16)
    spec_t = pl.BlockSpec((None, ROW_TILE // ATT_TILE, C_WIDTH, ATT_TILE),
                          lambda b, j: (b, j, 0, 0))
    return pl.pallas_call(
        _odd_proj_kernel,
        grid=grid,
        in_specs=[
            _row_spec(D_MODEL),
            _const_spec((1, D_MODEL)),
            _const_spec((D_MODEL, ODD_IN)),
            _const_spec((CONV_WIDTH, D_WIDTH)),
        ],
        out_specs=[spec_t, _row_spec(C_WIDTH), spec_t, _row_spec(D_WIDTH)],
        out_shape=[out_t, out, out_t, out],
        scratch_shapes=[
            pltpu.VMEM((CONV_HALO, D_WIDTH), _F32),
            pltpu.VMEM((PROJ_SLOTS, ROW_TILE // PROJ_SUBTILES, ODD_IN), _F32),
        ],
        compiler_params=pltpu.CompilerParams(
            dimension_semantics=("arbitrary", "arbitrary"),
            vmem_limit_bytes=VMEM_LIMIT_BYTES),
        name="odd_proj",
    )(h, g0, w_in, conv_w)


def _t5_bucket_upper_bounds(max_dist):
    d = np.arange(max_dist, dtype=np.int32)
    max_exact = REL_BUCKETS // 2
    nf = np.maximum(d, 1).astype(np.float32)
    large = max_exact + (np.log(nf / np.float32(max_exact))
                         / np.float32(math.log(REL_MAX_DIST / max_exact))
                         * np.float32(REL_BUCKETS - max_exact)).astype(np.int32)
    large = np.minimum(large, REL_BUCKETS - 1)
    bucket = np.where(d < max_exact, d, large)
    assert np.all(np.diff(bucket) >= 0)
    return bucket, {int(b): int(d[bucket == b].max()) for b in np.unique(bucket)}


def _attn_kernel(bucket_hi, lambda_init,
                 tab_ref, qt_ref, k_ref, vt_ref, lam_ref, sg_ref, o_ref,
                 qq_ref, bias_ref, *scratch):
    t = ATT_TILE
    blk = pl.program_id(1)
    last_bucket = REL_BUCKETS - 1

    @pl.when(blk == 0)
    def _():
        kpos = lax.broadcasted_iota(jnp.int32, (t, t), 0)
        qpos = lax.broadcasted_iota(jnp.int32, (t, t), 1)
        for delta in range(2):
            d = qpos - kpos + delta * t
            for hd in range(C_HEADS):
                far = tab_ref[last_bucket, hd]
                val = jnp.zeros((t, t), _F32)
                for b in sorted(bucket_hi, reverse=True):
                    if b == last_bucket:
                        continue
                    val = jnp.where(d <= bucket_hi[b], (tab_ref[b, hd] - far) * LOG2E, val)
                val = jnp.where(d >= 0, val, MASK_VALUE)
                bias_ref[delta, hd] = jnp.concatenate([val, val], axis=1)
        qq_ref[...] = jnp.zeros_like(qq_ref)

    def query_tile(sub, carry):
        rows = pl.ds(pl.multiple_of(sub * t, t), t)
        _attn_query_tile(lambda_init, blk * ATT_TILES_PER_STEP + sub,
                         qt_ref.at[sub], k_ref, vt_ref, lam_ref, sg_ref, o_ref.at[rows],
                         qq_ref, bias_ref, *scratch)
        return carry

    lax.fori_loop(0, ATT_TILES_PER_STEP, query_tile, 0)


def _attn_query_tile(lambda_init, i, qt_ref, k_ref, vt_ref, lam_ref, sg_ref, o_ref,
                     qq_ref, bias_ref, m_ref, acc_ref, s_ref, p_ref, al_ref):
    t = ATT_TILE
    hd_dim = 2 * C_QK_DIM
    heads = range(C_HEADS)

    for hd in heads:
        top = hd * hd_dim
        qq_ref[hd, :C_QK_DIM, :t] = qt_ref[top:top + C_QK_DIM, :]
        qq_ref[hd, C_QK_DIM:, t:] = qt_ref[top + C_QK_DIM:top + hd_dim, :]
    m_ref[...] = jnp.full_like(m_ref, MASK_VALUE)
    acc_ref[...] = jnp.zeros_like(acc_ref)
    p_ref[C_HEADS - 1] = jnp.zeros((t, 2 * t), _BF16)
    al_ref[C_HEADS - 1] = jnp.ones((1, 2 * t), _F32)

    def scores(jk, hd):
        start = pl.multiple_of(jk * t, t)
        k_t = k_ref[pl.ds(start, t), hd * hd_dim:(hd + 1) * hd_dim]
        s_ref[hd] = jnp.dot(k_t, qq_ref[hd], preferred_element_type=_F32)

    def values(jk, hd):
        vt = vt_ref[jk, hd * C_V_DIM:(hd + 1) * C_V_DIM, :]
        vt_ones = jnp.concatenate([vt, jnp.ones((SUM_ROWS, t), _BF16)], axis=0)
        acc_ref[hd] = al_ref[hd] * acc_ref[hd] + jnp.dot(
            vt_ones, p_ref[hd], preferred_element_type=_F32)

    def step(jk, delta, jk_next):
        for hd in heads:
            ahead = hd + SCORE_LOOKAHEAD
            if ahead < C_HEADS:
                scores(jk, ahead)
            elif jk_next is not None:
                scores(jk_next, ahead - C_HEADS)
            s = s_ref[hd]
            if delta is not None:
                s = s + bias_ref[delta, hd]
            m_prev = m_ref[hd]
            m_next = jnp.maximum(m_prev, jnp.max(s, axis=0, keepdims=True))
            alpha = jnp.exp2(m_prev - m_next)
            p = jnp.exp2(s - m_next)
            m_ref[hd] = m_next
            if hd == 0:
                values(jnp.maximum(jk - 1, 0), C_HEADS - 1)
            else:
                values(jk, hd - 1)
            p_ref[hd] = p.astype(_BF16)
            al_ref[hd] = alpha

    for hd in range(SCORE_LOOKAHEAD):
        scores(0, hd)

    n_far = jnp.maximum(i - 1, 0)

    def far_run(first, count):
        for n in range(count):
            step(first + n, None, first + n + 1)

    def far_trip(r, carry):
        far_run(FAR_UNROLL * r, FAR_UNROLL)
        return carry

    trips = n_far // FAR_UNROLL
    lax.fori_loop(0, trips, far_trip, 0)
    done = FAR_UNROLL * trips
    run = FAR_UNROLL // 2
    while run >= 2:
        take = (n_far - done) >= run
        pl.when(take)(functools.partial(far_run, done, run))
        done = done + jnp.where(take, run, 0)
        run //= 2

    def finish(odd_far_tile, sub_diagonal):
        if odd_far_tile:
            step(n_far - 1, None, n_far)
        if sub_diagonal:
            step(i - 1, 1, i)
        step(i, 0, None)
        values(i, C_HEADS - 1)
        lp = lam_ref[...]
        lam = (jnp.exp(jnp.sum(lp[0:1] * lp[1:2], axis=-1, keepdims=True))
               - jnp.exp(jnp.sum(lp[2:3] * lp[3:4], axis=-1, keepdims=True)) + lambda_init)
        for hd in heads:
            inv_l = 1.0 / acc_ref[hd, C_V_DIM:C_V_DIM + 1, :]
            a = (acc_ref[hd, :C_V_DIM, :t] * inv_l[:, :t]
                 - acc_ref[hd, :C_V_DIM, t:] * (lam * inv_l[:, t:]))
            ms = jnp.mean(a * a, axis=0, keepdims=True)
            y = a * (lax.rsqrt(ms + EPS) * (1.0 - lambda_init)) * sg_ref[...]
            o_ref[:, hd * C_V_DIM:(hd + 1) * C_V_DIM] = y.T.astype(_BF16)

    odd_far = n_far % 2 == 1
    pl.when(i == 0)(lambda: finish(False, False))
    pl.when(jnp.logical_and(i >= 1, jnp.logical_not(odd_far)))(lambda: finish(False, True))
    pl.when(odd_far)(lambda: finish(True, True))


def _diff_attention(qt, k, vt, rel_table, lam_params, subln_g, lambda_init):
    bsz, seq, _ = k.shape
    t = ATT_TILE
    bucket, bucket_hi = _t5_bucket_upper_bounds(seq)
    assert np.all(bucket[t + 1:] == REL_BUCKETS - 1)
    rows = t * ATT_TILES_PER_STEP
    grid = (bsz, seq // rows)
    kernel = functools.partial(_attn_kernel, bucket_hi, lambda_init)
    return pl.pallas_call(
        kernel,
        grid=grid,
        in_specs=[
            pl.BlockSpec(memory_space=pltpu.SMEM),
            pl.BlockSpec((None, ATT_TILES_PER_STEP, C_WIDTH, t), lambda b, i: (b, i, 0, 0)),
            pl.BlockSpec((None, seq, C_WIDTH), lambda b, i: (b, 0, 0)),
            pl.BlockSpec((None, seq // t, C_WIDTH, t), lambda b, i: (b, 0, 0, 0)),
            pl.BlockSpec((4, C_QK_DIM), lambda b, i: (0, 0)),
            pl.BlockSpec((C_V_DIM, 1), lambda b, i: (0, 0)),
        ],
        out_specs=pl.BlockSpec((None, rows, C_WIDTH), lambda b, i: (b, i, 0)),
        out_shape=jax.ShapeDtypeStruct((bsz, seq, C_WIDTH), _BF16),
        scratch_shapes=[
            pltpu.VMEM((C_HEADS, 2 * C_QK_DIM, 2 * t), _BF16),
            pltpu.VMEM((2, C_HEADS, t, 2 * t), _F32),
            pltpu.VMEM((C_HEADS, 1, 2 * t), _F32),
            pltpu.VMEM((C_HEADS, C_V_DIM + SUM_ROWS, 2 * t), _F32),
            pltpu.VMEM((C_HEADS, t, 2 * t), _F32),
            pltpu.VMEM((C_HEADS, t, 2 * t), _BF16),
            pltpu.VMEM((C_HEADS, 1, 2 * t), _F32),
        ],
        compiler_params=pltpu.CompilerParams(
            dimension_semantics=("arbitrary", "arbitrary"),
            vmem_limit_bytes=VMEM_LIMIT_BYTES),
        name="diff_attention",
    )(rel_table, qt, k, vt, lam_params, subln_g)


def _out_mlp_kernel(n_cast, h_ref, ya_ref, yb_ref, g_ref, w_out_ref, w1_ref, w2_ref, *rest):
    cast_in, o_ref, cast_out = rest[:n_cast], rest[n_cast], rest[n_cast + 1:]
    _cast_slabs(cast_in + cast_out)
    half = D_MODEL // 2
    sub = MLP_ROW_TILE // MLP_SUBTILES
    rows = [slice(r * sub, (r + 1) * sub) for r in range(MLP_SUBTILES)]
    ys = [jnp.dot(ya_ref[rs, :], w_out_ref[:half, :], preferred_element_type=_F32)
          + jnp.dot(yb_ref[rs, :], w_out_ref[half:, :], preferred_element_type=_F32)
          for rs in rows]
    h1s = [h_ref[rs, :] + _rms(y, g_ref[1:2, :]) for rs, y in zip(rows, ys)]
    hns = [_rms(h1, g_ref[2:3, :]).astype(_BF16) for h1 in h1s]
    for rs, h1, hn in zip(rows, h1s, hns):
        acc = jnp.zeros((sub, D_MODEL), _F32)
        for c in range(D_FF // FF_CHUNK):
            cs = slice(c * FF_CHUNK, (c + 1) * FF_CHUNK)
            a = jnp.dot(hn, w1_ref[:, cs], preferred_element_type=_F32)
            a = jnp.square(jnp.maximum(a, 0.0)).astype(_BF16)
            acc = acc + jnp.dot(a, w2_ref[cs, :], preferred_element_type=_F32)
        o_ref[rs, :] = h1 + _rms(acc, g_ref[3:4, :])


def _out_mlp(h, ya, yb, g, w_out, w1, w2, casts=()):
    bsz, seq, _ = h.shape
    grid = (bsz, seq // MLP_ROW_TILE)
    row_spec = functools.partial(_row_spec, tile=MLP_ROW_TILE)
    cast_in, cast_out, cast_shapes = _cast_plan(casts, grid)
    return pl.pallas_call(
        functools.partial(_out_mlp_kernel, len(casts)),
        grid=grid,
        in_specs=[
            row_spec(D_MODEL),
            row_spec(D_MODEL // 2),
            row_spec(D_MODEL // 2),
            _const_spec((4, D_MODEL)),
            _const_spec((D_MODEL, D_MODEL)),
            _const_spec((D_MODEL, D_FF)),
            _const_spec((D_FF, D_MODEL)),
        ] + cast_in,
        out_specs=[row_spec(D_MODEL)] + cast_out,
        out_shape=[jax.ShapeDtypeStruct(h.shape, h.dtype)] + cast_shapes,
        compiler_params=pltpu.CompilerParams(
            dimension_semantics=("arbitrary", "arbitrary"),
            vmem_limit_bytes=VMEM_LIMIT_BYTES),
        name="out_mlp",
    )(h, ya, yb, g, w_out, w1, w2, *[w for w, _ in casts])


def kernel(x, rel_bias_table, norm_g, even_w_in, even_ln_g, even_ln_b, even_spatial_w,
           even_spatial_b, even_pool_w, even_pool_scale, even_w_out, odd_w_in, odd_lambda,
           odd_subln_g, odd_conv_w, odd_w_out, ffn_w1, ffn_w2):
    depth = norm_g.shape[0]
    bf = lambda w: w.astype(_BF16)

    def layer_weights(layer):
        mixer = (even_w_in, even_w_out) if layer % 2 == 0 else (odd_w_in, odd_w_out)
        return [(mixer[0], layer // 2), (mixer[1], layer // 2),
                (ffn_w1, layer), (ffn_w2, layer)]

    first = layer_weights(0)
    w_in = bf(first[0][0][first[0][1]])
    w_out = w1 = w2 = None
    h = x
    for layer in range(depth):
        g = norm_g[layer]
        own_casts = [] if w_out is not None else layer_weights(layer)[1:]
        if layer % 2 == 0:
            e = layer // 2
            ya, yb, *cast = _even_mixer(
                h, g[0:1], w_in, even_ln_g[e][None], even_ln_b[e][None],
                even_spatial_w[e], even_spatial_b[e].T, bf(even_pool_w[e]),
                even_pool_scale[e][None], own_casts)
        else:
            o = layer // 2
            lambda_init = 0.8 - 0.6 * math.exp(-0.3 * layer)
            q, k, vt, yb = _odd_proj(h, g[0:1], w_in, odd_conv_w[o])
            ya = _diff_attention(q, k, vt, rel_bias_table, odd_lambda[o],
                                 odd_subln_g[o][:, None], lambda_init)
            cast = [bf(w[i]) for w, i in own_casts]
        if own_casts:
            w_out, w1, w2 = cast
        next_casts = layer_weights(layer + 1) if layer + 1 < depth else []
        h, *nxt = _out_mlp(h, ya, yb, g, w_out, w1, w2, next_casts)
        w_in, w_out, w1, w2 = nxt if nxt else (None,) * 4
    return h
```

```python
import functools
import math

import numpy as np
import jax
import jax.numpy as jnp
from jax import lax
from jax.experimental import pallas as pl
from jax.experimental.pallas import tpu as pltpu

D_MODEL = 1024
A_WIDTH = 512
A_GROUPS = 4
A_GROUP_DIM = 128
CHUNK = 128
B_WIDTH = 512
POOL_WINDOWS = (2, 4, 8, 16)
B_GROUP_DIM = 128
C_HEADS = 4
C_QK_DIM = 64
C_V_DIM = 128
C_WIDTH = 512
D_WIDTH = 512
CONV_WIDTH = 3
REL_BUCKETS = 32
REL_MAX_DIST = 128
D_FF = 4096
EPS = 1e-6
EVEN_IN = 2 * A_WIDTH + B_WIDTH
ODD_IN = 2 * C_WIDTH + C_WIDTH + 3 * D_WIDTH

ROW_TILE = 1024
ATT_TILE = 256
ATT_TILES_PER_STEP = 4
FF_CHUNK = 1024
PROJ_SUBTILES = 4
PROJ_SLOTS = 3
MLP_ROW_TILE = 1024
MLP_SUBTILES = 4
POOL_HALO = 16
CONV_HALO = 8
LOG2E = math.log2(math.e)
FAR_UNROLL = 4
SCORE_LOOKAHEAD = 1
SUM_ROWS = 16
MASK_VALUE = -1e30
BF16_SUBLANES = 16
VMEM_LIMIT_BYTES = 52 * 1024 * 1024

_F32 = jnp.float32
_BF16 = jnp.bfloat16


def _rms(x, g):
    return x * lax.rsqrt(jnp.mean(x * x, axis=-1, keepdims=True) + EPS) * g


def _gelu_tanh(x):
    c = math.sqrt(2.0 / math.pi)
    hx = 0.5 * x
    return hx + hx * jnp.tanh(x * (c + (c * 0.044715) * (x * x)))


def _norm_project(x_ref, g_ref, w_ref, proj_ref):
    rows = x_ref.shape[0] // PROJ_SUBTILES
    slots = proj_ref.shape[0]
    ahead = slots - 1

    def project(r):
        hn = _rms(x_ref[r * rows:(r + 1) * rows, :], g_ref[...]).astype(_BF16)
        proj_ref[r % slots] = jnp.dot(hn, w_ref[...], preferred_element_type=_F32)

    for r in range(min(ahead, PROJ_SUBTILES)):
        project(r)
    for r in range(PROJ_SUBTILES):
        if r + ahead < PROJ_SUBTILES:
            project(r + ahead)
        yield r, proj_ref.at[r % slots]


def _const_spec(shape):
    nd = len(shape)
    return pl.BlockSpec(shape, lambda *_: (0,) * nd, pipeline_mode=pl.Buffered(1))


def _row_spec(width, col=0, tile=None):
    return pl.BlockSpec((None, tile or ROW_TILE, width), lambda b, j: (b, j, col))


def _cast_plan(weights, grid):
    steps = grid[0] * grid[1]
    in_specs, out_specs, out_shapes = [], [], []
    for w, layer in weights:
        _, rows, cols = w.shape
        slab = rows // steps
        assert slab * steps == rows and slab % BF16_SUBLANES == 0
        in_specs.append(pl.BlockSpec(
            (None, slab, cols), lambda b, j, layer=layer: (layer, b * grid[1] + j, 0)))
        out_specs.append(pl.BlockSpec((slab, cols), lambda b, j: (b * grid[1] + j, 0)))
        out_shapes.append(jax.ShapeDtypeStruct((rows, cols), _BF16))
    return in_specs, out_specs, out_shapes


def _cast_slabs(refs):
    n = len(refs) // 2
    for src, dst in zip(refs[:n], refs[n:]):
        dst[...] = src[...].astype(_BF16)


def _even_mixer_kernel(n_cast, x_ref, g_ref, w_in_ref, ln_g_ref, ln_b_ref, ws_ref, bs_ref,
                       pw_ref, ps_ref, *rest):
    cast_in, (ya_ref, yb_ref), rest = rest[:n_cast], rest[n_cast:n_cast + 2], rest[n_cast + 2:]
    cast_out, (carry_ref, proj_ref) = rest[:n_cast], rest[n_cast:]
    _cast_slabs(cast_in + cast_out)
    j = pl.program_id(1)
    rows = ROW_TILE // PROJ_SUBTILES

    @pl.when(j == 0)
    def _():
        carry_ref[...] = jnp.zeros_like(carry_ref)

    row = lax.broadcasted_iota(jnp.int32, (CHUNK, CHUNK), 0)
    col = lax.broadcasted_iota(jnp.int32, (CHUNK, CHUNK), 1)
    tril_w = [jnp.where(row >= col, ws_ref[g], 0.0).astype(_BF16) for g in range(A_GROUPS)]
    halo = carry_ref[...]
    for sub, proj in _norm_project(x_ref, g_ref, w_in_ref, proj_ref):
        base = sub * rows

        z = _gelu_tanh(proj[:, :2 * A_WIDTH])
        u = z[:, :A_WIDTH]
        v = z[:, A_WIDTH:]
        mu = jnp.mean(v, axis=-1, keepdims=True)
        vc = v - mu
        vn = vc * lax.rsqrt(jnp.mean(vc * vc, axis=-1, keepdims=True) + EPS)
        vn = (vn * ln_g_ref[...] + ln_b_ref[...]).astype(_BF16)
        for g in range(A_GROUPS):
            cs = slice(g * A_GROUP_DIM, (g + 1) * A_GROUP_DIM)
            bias = bs_ref[:, g:g + 1]
            for c in range(0, rows // CHUNK, 2):
                r0 = slice(c * CHUNK, (c + 1) * CHUNK)
                r1 = slice((c + 1) * CHUNK, (c + 2) * CHUNK)
                rhs = jnp.concatenate([vn[r0, cs], vn[r1, cs]], axis=1)
                mixed = jnp.dot(tril_w[g], rhs, preferred_element_type=_F32) + bias
                ya_ref[base + r0.start:base + r0.stop, cs] = (
                    u[r0, cs] * mixed[:, :A_GROUP_DIM]).astype(_BF16)
                ya_ref[base + r1.start:base + r1.stop, cs] = (
                    u[r1, cs] * mixed[:, A_GROUP_DIM:]).astype(_BF16)

        p = proj[:, 2 * A_WIDTH:]
        cur = jnp.concatenate([halo, p], axis=0)
        halo = p[rows - POOL_HALO:, :]
        cur_win = 1
        pos = j * ROW_TILE + base + lax.broadcasted_iota(jnp.int32, (rows, B_GROUP_DIM), 0)
        tokens_so_far = (pos + 1).astype(_F32)
        for g, win in enumerate(POOL_WINDOWS):
            cs = slice(g * B_GROUP_DIM, (g + 1) * B_GROUP_DIM)
            while cur_win < win:
                cur = cur + pltpu.roll(cur, cur_win, axis=0)
                cur_win *= 2
            assert cur_win == win and win <= POOL_HALO
            count = jnp.minimum(tokens_so_far, float(win))
            pooled = (cur[POOL_HALO:, :B_GROUP_DIM] / count - p[:, cs]).astype(_BF16)
            y = jnp.dot(pooled, pw_ref[g], preferred_element_type=_F32)
            yb_ref[base:base + rows, cs] = (y * ps_ref[:, cs]).astype(_BF16)
            cur = cur[:, B_GROUP_DIM:]
    carry_ref[...] = halo


def _even_mixer(h, g0, w_in, ln_g, ln_b, w_s, b_s_t, pool_w, pool_scale, casts):
    bsz, seq, _ = h.shape
    grid = (bsz, seq // ROW_TILE)
    out = jax.ShapeDtypeStruct((bsz, seq, A_WIDTH), _BF16)
    cast_in, cast_out, cast_shapes = _cast_plan(casts, grid)
    return pl.pallas_call(
        functools.partial(_even_mixer_kernel, len(casts)),
        grid=grid,
        in_specs=[
            _row_spec(D_MODEL),
            _const_spec((1, D_MODEL)),
            _const_spec((D_MODEL, EVEN_IN)),
            _const_spec((1, A_WIDTH)),
            _const_spec((1, A_WIDTH)),
            _const_spec((A_GROUPS, CHUNK, CHUNK)),
            _const_spec((CHUNK, A_GROUPS)),
            _const_spec((len(POOL_WINDOWS), B_GROUP_DIM, B_GROUP_DIM)),
            _const_spec((1, B_WIDTH)),
        ] + cast_in,
        out_specs=[_row_spec(A_WIDTH), _row_spec(B_WIDTH)] + cast_out,
        out_shape=[out, out] + cast_shapes,
        scratch_shapes=[
            pltpu.VMEM((POOL_HALO, B_WIDTH), _F32),
            pltpu.VMEM((PROJ_SLOTS, ROW_TILE // PROJ_SUBTILES, EVEN_IN), _F32),
        ],
        compiler_params=pltpu.CompilerParams(
            dimension_semantics=("arbitrary", "arbitrary"),
            vmem_limit_bytes=VMEM_LIMIT_BYTES),
        name="even_mixer",
    )(h, g0, w_in, ln_g, ln_b, w_s, b_s_t, pool_w, pool_scale, *[w for w, _ in casts])


def _odd_proj_kernel(x_ref, g_ref, w_in_ref, cw_ref, q_ref, k_ref, vt_ref, yd_ref,
                     carry_ref, proj_ref):
    j = pl.program_id(1)
    rows = ROW_TILE // PROJ_SUBTILES

    @pl.when(j == 0)
    def _():
        carry_ref[...] = jnp.zeros_like(carry_ref)

    halo = carry_ref[...]
    for sub, proj in _norm_project(x_ref, g_ref, w_in_ref, proj_ref):
        rs = slice(sub * rows, (sub + 1) * rows)
        q_ref[rs, :] = (proj[:, :C_WIDTH] * (C_QK_DIM ** -0.5 * LOG2E)).astype(_BF16)
        k_ref[rs, :] = proj[:, C_WIDTH:2 * C_WIDTH].astype(_BF16)
        for kt in range(rows // ATT_TILE):
            v_tile = proj[kt * ATT_TILE:(kt + 1) * ATT_TILE, 2 * C_WIDTH:3 * C_WIDTH]
            vt_ref[sub * (rows // ATT_TILE) + kt] = v_tile.T.astype(_BF16)
        o = 3 * C_WIDTH
        bg = proj[:, o:o + D_WIDTH]
        z = proj[:, o + D_WIDTH:o + 2 * D_WIDTH] * proj[:, o + 2 * D_WIDTH:]
        ext = jnp.concatenate([halo, z], axis=0)
        halo = z[rows - CONV_HALO:, :]
        y = cw_ref[CONV_WIDTH - 1:CONV_WIDTH, :] * z
        for t in range(CONV_WIDTH - 1):
            shift = CONV_WIDTH - 1 - t
            y = y + cw_ref[t:t + 1, :] * pltpu.roll(ext, shift, axis=0)[CONV_HALO:, :]
        yd_ref[rs, :] = (bg * y).astype(_BF16)
    carry_ref[...] = halo


def _odd_proj(h, g0, w_in, conv_w):
    bsz, seq, _ = h.shape
    grid = (bsz, seq // ROW_TILE)
    out = jax.ShapeDtypeStruct((bsz, seq, C_WIDTH), _BF16)
    assert (ROW_TILE // PROJ_SUBTILES) % ATT_TILE == 0
    out_t = jax.ShapeDtypeStruct((bsz, seq // ATT_TILE, C_WIDTH, ATT_TILE), _BF16)
    spec_t = pl.BlockSpec((None, ROW_TILE // ATT_TILE, C_WIDTH, ATT_TILE),
                          lambda b, j: (b, j, 0, 0))
    return pl.pallas_call(
        _odd_proj_kernel,
        grid=grid,
        in_specs=[
            _row_spec(D_MODEL),
            _const_spec((1, D_MODEL)),
            _const_spec((D_MODEL, ODD_IN)),
            _const_spec((CONV_WIDTH, D_WIDTH)),
        ],
        out_specs=[_row_spec(C_WIDTH), _row_spec(C_WIDTH), spec_t, _row_spec(D_WIDTH)],
        out_shape=[out, out, out_t, out],
        scratch_shapes=[
            pltpu.VMEM((CONV_HALO, D_WIDTH), _F32),
            pltpu.VMEM((PROJ_SLOTS, ROW_TILE // PROJ_SUBTILES, ODD_IN), _F32),
        ],
        compiler_params=pltpu.CompilerParams(
            dimension_semantics=("arbitrary", "arbitrary"),
            vmem_limit_bytes=VMEM_LIMIT_BYTES),
        name="odd_proj",
    )(h, g0, w_in, conv_w)


def _t5_bucket_upper_bounds(max_dist):
    d = np.arange(max_dist, dtype=np.int32)
    max_exact = REL_BUCKETS // 2
    nf = np.maximum(d, 1).astype(np.float32)
    large = max_exact + (np.log(nf / np.float32(max_exact))
                         / np.float32(math.log(REL_MAX_DIST / max_exact))
                         * np.float32(REL_BUCKETS - max_exact)).astype(np.int32)
    large = np.minimum(large, REL_BUCKETS - 1)
    bucket = np.where(d < max_exact, d, large)
    assert np.all(np.diff(bucket) >= 0)
    return bucket, {int(b): int(d[bucket == b].max()) for b in np.unique(bucket)}


def _attn_kernel(bucket_hi, lambda_init,
                 tab_ref, q_ref, k_ref, vt_ref, lam_ref, sg_ref, o_ref,
                 qq_ref, bias_ref, *scratch):
    t = ATT_TILE
    blk = pl.program_id(1)
    last_bucket = REL_BUCKETS - 1

    @pl.when(blk == 0)
    def _():
        kpos = lax.broadcasted_iota(jnp.int32, (t, t), 0)
        qpos = lax.broadcasted_iota(jnp.int32, (t, t), 1)
        for delta in range(2):
            d = qpos - kpos + delta * t
            for hd in range(C_HEADS):
                far = tab_ref[last_bucket, hd]
                val = jnp.zeros((t, t), _F32)
                for b in sorted(bucket_hi, reverse=True):
                    if b == last_bucket:
                        continue
                    val = jnp.where(d <= bucket_hi[b], (tab_ref[b, hd] - far) * LOG2E, val)
                val = jnp.where(d >= 0, val, MASK_VALUE)
                bias_ref[delta, hd] = jnp.concatenate([val, val], axis=1)

    def query_tile(sub, carry):
        rows = pl.ds(pl.multiple_of(sub * t, t), t)
        _attn_query_tile(lambda_init, blk * ATT_TILES_PER_STEP + sub,
                         q_ref.at[rows], k_ref, vt_ref, lam_ref, sg_ref, o_ref.at[sub],
                         qq_ref, bias_ref, *scratch)
        return carry

    lax.fori_loop(0, ATT_TILES_PER_STEP, query_tile, 0)


def _attn_query_tile(lambda_init, i, q_ref, k_ref, vt_ref, lam_ref, sg_ref, o_ref,
                     qq_ref, bias_ref, m_ref, acc_ref, s_ref, p_ref, al_ref):
    t = ATT_TILE
    hd_dim = 2 * C_QK_DIM
    heads = range(C_HEADS)

    feat = lax.broadcasted_iota(jnp.int32, (hd_dim, t), 0)
    for hd in heads:
        qt = q_ref[:, hd * hd_dim:(hd + 1) * hd_dim].astype(_F32).T
        qq_ref[hd] = jnp.concatenate(
            [jnp.where(feat < C_QK_DIM, qt, 0.0), jnp.where(feat >= C_QK_DIM, qt, 0.0)],
            axis=1).astype(_BF16)
    m_ref[...] = jnp.full_like(m_ref, MASK_VALUE)
    acc_ref[...] = jnp.zeros_like(acc_ref)
    p_ref[C_HEADS - 1] = jnp.zeros((t, 2 * t), _BF16)
    al_ref[C_HEADS - 1] = jnp.ones((1, 2 * t), _F32)

    def scores(jk, hd):
        start = pl.multiple_of(jk * t, t)
        k_t = k_ref[pl.ds(start, t), hd * hd_dim:(hd + 1) * hd_dim]
        s_ref[hd] = jnp.dot(k_t, qq_ref[hd], preferred_element_type=_F32)

    def values(jk, hd):
        vt = vt_ref[jk, hd * C_V_DIM:(hd + 1) * C_V_DIM, :]
        vt_ones = jnp.concatenate([vt, jnp.ones((SUM_ROWS, t), _BF16)], axis=0)
        acc_ref[hd] = al_ref[hd] * acc_ref[hd] + jnp.dot(
            vt_ones, p_ref[hd], preferred_element_type=_F32)

    def step(jk, delta, jk_next):
        for hd in heads:
            ahead = hd + SCORE_LOOKAHEAD
            if ahead < C_HEADS:
                scores(jk, ahead)
            elif jk_next is not None:
                scores(jk_next, ahead - C_HEADS)
            s = s_ref[hd]
            if delta is not None:
                s = s + bias_ref[delta, hd]
            m_prev = m_ref[hd]
            m_next = jnp.maximum(m_prev, jnp.max(s, axis=0, keepdims=True))
            alpha = jnp.exp2(m_prev - m_next)
            p = jnp.exp2(s - m_next)
            m_ref[hd] = m_next
            if hd == 0:
                values(jnp.maximum(jk - 1, 0), C_HEADS - 1)
            else:
                values(jk, hd - 1)
            p_ref[hd] = p.astype(_BF16)
            al_ref[hd] = alpha

    for hd in range(SCORE_LOOKAHEAD):
        scores(0, hd)

    n_far = jnp.maximum(i - 1, 0)

    def far_run(first, count):
        for n in range(count):
            step(first + n, None, first + n + 1)

    def far_trip(r, carry):
        far_run(FAR_UNROLL * r, FAR_UNROLL)
        return carry

    trips = n_far // FAR_UNROLL
    lax.fori_loop(0, trips, far_trip, 0)
    done = FAR_UNROLL * trips
    run = FAR_UNROLL // 2
    while run >= 2:
        take = (n_far - done) >= run
        pl.when(take)(functools.partial(far_run, done, run))
        done = done + jnp.where(take, run, 0)
        run //= 2

    def finish(odd_far_tile, sub_diagonal):
        if odd_far_tile:
            step(n_far - 1, None, n_far)
        if sub_diagonal:
            step(i - 1, 1, i)
        step(i, 0, None)
        values(i, C_HEADS - 1)
        lp = lam_ref[...]
        lam = (jnp.exp(jnp.sum(lp[0:1] * lp[1:2], axis=-1, keepdims=True))
               - jnp.exp(jnp.sum(lp[2:3] * lp[3:4], axis=-1, keepdims=True)) + lambda_init)
        for hd in heads:
            inv_l = 1.0 / acc_ref[hd, C_V_DIM:C_V_DIM + 1, :]
            a = (acc_ref[hd, :C_V_DIM, :t] * inv_l[:, :t]
                 - acc_ref[hd, :C_V_DIM, t:] * (lam * inv_l[:, t:]))
            ms = jnp.mean(a * a, axis=0, keepdims=True)
            y = a * (lax.rsqrt(ms + EPS) * (1.0 - lambda_init)) * sg_ref[...]
            o_ref[hd * C_V_DIM:(hd + 1) * C_V_DIM, :] = y.astype(_BF16)

    odd_far = n_far % 2 == 1
    pl.when(i == 0)(lambda: finish(False, False))
    pl.when(jnp.logical_and(i >= 1, jnp.logical_not(odd_far)))(lambda: finish(False, True))
    pl.when(odd_far)(lambda: finish(True, True))


def _diff_attention(q, k, vt, rel_table, lam_params, subln_g, lambda_init):
    bsz, seq, _ = q.shape
    t = ATT_TILE
    bucket, bucket_hi = _t5_bucket_upper_bounds(seq)
    assert np.all(bucket[t + 1:] == REL_BUCKETS - 1)
    rows = t * ATT_TILES_PER_STEP
    grid = (bsz, seq // rows)
    kernel = functools.partial(_attn_kernel, bucket_hi, lambda_init)
    return pl.pallas_call(
        kernel,
        grid=grid,
        in_specs=[
            pl.BlockSpec(memory_space=pltpu.SMEM),
            pl.BlockSpec((None, rows, C_WIDTH), lambda b, i: (b, i, 0)),
            pl.BlockSpec((None, seq, C_WIDTH), lambda b, i: (b, 0, 0)),
            pl.BlockSpec((None, seq // t, C_WIDTH, t), lambda b, i: (b, 0, 0, 0)),
            pl.BlockSpec((4, C_QK_DIM), lambda b, i: (0, 0)),
            pl.BlockSpec((C_V_DIM, 1), lambda b, i: (0, 0)),
        ],
        out_specs=pl.BlockSpec((None, ATT_TILES_PER_STEP, C_WIDTH, t), lambda b, i: (b, i, 0, 0)),
        out_shape=jax.ShapeDtypeStruct((bsz, seq // t, C_WIDTH, t), _BF16),
        scratch_shapes=[
            pltpu.VMEM((C_HEADS, 2 * C_QK_DIM, 2 * t), _BF16),
            pltpu.VMEM((2, C_HEADS, t, 2 * t), _F32),
            pltpu.VMEM((C_HEADS, 1, 2 * t), _F32),
            pltpu.VMEM((C_HEADS, C_V_DIM + SUM_ROWS, 2 * t), _F32),
            pltpu.VMEM((C_HEADS, t, 2 * t), _F32),
            pltpu.VMEM((C_HEADS, t, 2 * t), _BF16),
            pltpu.VMEM((C_HEADS, 1, 2 * t), _F32),
        ],
        compiler_params=pltpu.CompilerParams(
            dimension_semantics=("arbitrary", "arbitrary"),
            vmem_limit_bytes=VMEM_LIMIT_BYTES),
        name="diff_attention",
    )(rel_table, q, k, vt, lam_params, subln_g)


def _out_mlp_kernel(n_cast, ya_tiled, h_ref, ya_ref, yb_ref, g_ref, w_out_ref, w1_ref, w2_ref,
                    *rest):
    cast_in, o_ref, cast_out = rest[:n_cast], rest[n_cast], rest[n_cast + 1:]
    _cast_slabs(cast_in + cast_out)
    half = D_MODEL // 2
    sub = MLP_ROW_TILE // MLP_SUBTILES
    rows = [slice(r * sub, (r + 1) * sub) for r in range(MLP_SUBTILES)]

    def out_proj_a(r):
        if not ya_tiled:
            return jnp.dot(ya_ref[rows[r], :], w_out_ref[:half, :], preferred_element_type=_F32)
        per = sub // ATT_TILE
        parts = [lax.dot_general(ya_ref[r * per + kt], w_out_ref[:half, :],
                                 (((0,), (0,)), ((), ())), preferred_element_type=_F32)
                 for kt in range(per)]
        return parts[0] if per == 1 else jnp.concatenate(parts, axis=0)

    ys = [out_proj_a(r)
          + jnp.dot(yb_ref[rs, :], w_out_ref[half:, :], preferred_element_type=_F32)
          for r, rs in enumerate(rows)]
    h1s = [h_ref[rs, :] + _rms(y, g_ref[1:2, :]) for rs, y in zip(rows, ys)]
    hns = [_rms(h1, g_ref[2:3, :]).astype(_BF16) for h1 in h1s]
    for rs, h1, hn in zip(rows, h1s, hns):
        acc = jnp.zeros((sub, D_MODEL), _F32)
        for c in range(D_FF // FF_CHUNK):
            cs = slice(c * FF_CHUNK, (c + 1) * FF_CHUNK)
            a = jnp.dot(hn, w1_ref[:, cs], preferred_element_type=_F32)
            a = jnp.square(jnp.maximum(a, 0.0)).astype(_BF16)
            acc = acc + jnp.dot(a, w2_ref[cs, :], preferred_element_type=_F32)
        o_ref[rs, :] = h1 + _rms(acc, g_ref[3:4, :])


def _out_mlp(h, ya, yb, g, w_out, w1, w2, casts=()):
    bsz, seq, _ = h.shape
    grid = (bsz, seq // MLP_ROW_TILE)
    row_spec = functools.partial(_row_spec, tile=MLP_ROW_TILE)
    cast_in, cast_out, cast_shapes = _cast_plan(casts, grid)
    ya_tiled = ya.ndim == 4
    if ya_tiled:
        assert (MLP_ROW_TILE // MLP_SUBTILES) % ATT_TILE == 0
        ya_spec = pl.BlockSpec((None, MLP_ROW_TILE // ATT_TILE, D_MODEL // 2, ATT_TILE),
                               lambda b, j: (b, j, 0, 0))
    else:
        ya_spec = row_spec(D_MODEL // 2)
    return pl.pallas_call(
        functools.partial(_out_mlp_kernel, len(casts), ya_tiled),
        grid=grid,
        in_specs=[
            row_spec(D_MODEL),
            ya_spec,
            row_spec(D_MODEL // 2),
            _const_spec((4, D_MODEL)),
            _const_spec((D_MODEL, D_MODEL)),
            _const_spec((D_MODEL, D_FF)),
            _const_spec((D_FF, D_MODEL)),
        ] + cast_in,
        out_specs=[row_spec(D_MODEL)] + cast_out,
        out_shape=[jax.ShapeDtypeStruct(h.shape, h.dtype)] + cast_shapes,
        compiler_params=pltpu.CompilerParams(
            dimension_semantics=("arbitrary", "arbitrary"),
            vmem_limit_bytes=VMEM_LIMIT_BYTES),
        name="out_mlp",
    )(h, ya, yb, g, w_out, w1, w2, *[w for w, _ in casts])


def kernel(x, rel_bias_table, norm_g, even_w_in, even_ln_g, even_ln_b, even_spatial_w,
           even_spatial_b, even_pool_w, even_pool_scale, even_w_out, odd_w_in, odd_lambda,
           odd_subln_g, odd_conv_w, odd_w_out, ffn_w1, ffn_w2):
    depth = norm_g.shape[0]
    bf = lambda w: w.astype(_BF16)

    def layer_weights(layer):
        mixer = (even_w_in, even_w_out) if layer % 2 == 0 else (odd_w_in, odd_w_out)
        return [(mixer[0], layer // 2), (mixer[1], layer // 2),
                (ffn_w1, layer), (ffn_w2, layer)]

    first = layer_weights(0)
    w_in = bf(first[0][0][first[0][1]])
    w_out = w1 = w2 = None
    h = x
    for layer in range(depth):
        g = norm_g[layer]
        own_casts = [] if w_out is not None else layer_weights(layer)[1:]
        if layer % 2 == 0:
            e = layer // 2
            ya, yb, *cast = _even_mixer(
                h, g[0:1], w_in, even_ln_g[e][None], even_ln_b[e][None],
                even_spatial_w[e], even_spatial_b[e].T, bf(even_pool_w[e]),
                even_pool_scale[e][None], own_casts)
        else:
            o = layer // 2
            lambda_init = 0.8 - 0.6 * math.exp(-0.3 * layer)
            q, k, vt, yb = _odd_proj(h, g[0:1], w_in, odd_conv_w[o])
            ya = _diff_attention(q, k, vt, rel_bias_table, odd_lambda[o],
                                 odd_subln_g[o][:, None], lambda_init)
            cast = [bf(w[i]) for w, i in own_casts]
        if own_casts:
            w_out, w1, w2 = cast
        next_casts = layer_weights(layer + 1) if layer + 1 < depth else []
        h, *nxt = _out_mlp(h, ya, yb, g, w_out, w1, w2, next_casts)
        w_in, w_out, w1, w2 = nxt if nxt else (None,) * 4
    return h
```

```python
import functools
import math

import numpy as np
import jax
import jax.numpy as jnp
from jax import lax
from jax.experimental import pallas as pl
from jax.experimental.pallas import tpu as pltpu

D_MODEL = 1024
A_WIDTH = 512
A_GROUPS = 4
A_GROUP_DIM = 128
CHUNK = 128
B_WIDTH = 512
POOL_WINDOWS = (2, 4, 8, 16)
B_GROUP_DIM = 128
C_HEADS = 4
C_QK_DIM = 64
C_V_DIM = 128
C_WIDTH = 512
D_WIDTH = 512
CONV_WIDTH = 3
REL_BUCKETS = 32
REL_MAX_DIST = 128
D_FF = 4096
EPS = 1e-6
EVEN_IN = 2 * A_WIDTH + B_WIDTH
ODD_IN = 2 * C_WIDTH + C_WIDTH + 3 * D_WIDTH

ROW_TILE = 1024
ATT_TILE = 256
ATT_TILES_PER_STEP = 4
FF_CHUNK = 1024
PROJ_SUBTILES = 4
PROJ_SLOTS = 3
MLP_ROW_TILE = 1024
MLP_SUBTILES = 4
POOL_HALO = 16
CONV_HALO = 8
LOG2E = math.log2(math.e)
FAR_UNROLL = 8
SCORE_LOOKAHEAD = 1
SUM_ROWS = 16
MASK_VALUE = -1e30
BF16_SUBLANES = 16
VMEM_LIMIT_BYTES = 52 * 1024 * 1024

_F32 = jnp.float32
_BF16 = jnp.bfloat16


def _rms(x, g):
    return x * lax.rsqrt(jnp.mean(x * x, axis=-1, keepdims=True) + EPS) * g


def _gelu_tanh(x):
    c = math.sqrt(2.0 / math.pi)
    hx = 0.5 * x
    return hx + hx * jnp.tanh(x * (c + (c * 0.044715) * (x * x)))


def _norm_project(x_ref, g_ref, w_ref, proj_ref):
    rows = x_ref.shape[0] // PROJ_SUBTILES
    slots = proj_ref.shape[0]
    ahead = slots - 1

    def project(r):
        hn = _rms(x_ref[r * rows:(r + 1) * rows, :], g_ref[...]).astype(_BF16)
        proj_ref[r % slots] = jnp.dot(hn, w_ref[...], preferred_element_type=_F32)

    for r in range(min(ahead, PROJ_SUBTILES)):
        project(r)
    for r in range(PROJ_SUBTILES):
        if r + ahead < PROJ_SUBTILES:
            project(r + ahead)
        yield r, proj_ref.at[r % slots]


def _const_spec(shape):
    nd = len(shape)
    return pl.BlockSpec(shape, lambda *_: (0,) * nd, pipeline_mode=pl.Buffered(1))


def _row_spec(width, col=0, tile=None):
    return pl.BlockSpec((None, tile or ROW_TILE, width), lambda b, j: (b, j, col))


def _cast_plan(weights, grid):
    steps = grid[0] * grid[1]
    in_specs, out_specs, out_shapes = [], [], []
    for w, layer in weights:
        _, rows, cols = w.shape
        slab = rows // steps
        assert slab * steps == rows and slab % BF16_SUBLANES == 0
        in_specs.append(pl.BlockSpec(
            (None, slab, cols), lambda b, j, layer=layer: (layer, b * grid[1] + j, 0)))
        out_specs.append(pl.BlockSpec((slab, cols), lambda b, j: (b * grid[1] + j, 0)))
        out_shapes.append(jax.ShapeDtypeStruct((rows, cols), _BF16))
    return in_specs, out_specs, out_shapes


def _cast_slabs(refs):
    n = len(refs) // 2
    for src, dst in zip(refs[:n], refs[n:]):
        dst[...] = src[...].astype(_BF16)


def _even_mixer_kernel(n_cast, x_ref, g_ref, w_in_ref, ln_g_ref, ln_b_ref, ws_ref, bs_ref,
                       pw_ref, ps_ref, *rest):
    cast_in, (ya_ref, yb_ref), rest = rest[:n_cast], rest[n_cast:n_cast + 2], rest[n_cast + 2:]
    cast_out, (carry_ref, proj_ref) = rest[:n_cast], rest[n_cast:]
    _cast_slabs(cast_in + cast_out)
    j = pl.program_id(1)
    rows = ROW_TILE // PROJ_SUBTILES

    @pl.when(j == 0)
    def _():
        carry_ref[...] = jnp.zeros_like(carry_ref)

    row = lax.broadcasted_iota(jnp.int32, (CHUNK, CHUNK), 0)
    col = lax.broadcasted_iota(jnp.int32, (CHUNK, CHUNK), 1)
    tril_w = [jnp.where(row >= col, ws_ref[g], 0.0).astype(_BF16) for g in range(A_GROUPS)]
    halo = carry_ref[...]
    for sub, proj in _norm_project(x_ref, g_ref, w_in_ref, proj_ref):
        base = sub * rows

        z = _gelu_tanh(proj[:, :2 * A_WIDTH])
        u = z[:, :A_WIDTH]
        v = z[:, A_WIDTH:]
        mu = jnp.mean(v, axis=-1, keepdims=True)
        vc = v - mu
        vn = vc * lax.rsqrt(jnp.mean(vc * vc, axis=-1, keepdims=True) + EPS)
        vn = (vn * ln_g_ref[...] + ln_b_ref[...]).astype(_BF16)
        for g in range(A_GROUPS):
            cs = slice(g * A_GROUP_DIM, (g + 1) * A_GROUP_DIM)
            bias = bs_ref[:, g:g + 1]
            for c in range(0, rows // CHUNK, 2):
                r0 = slice(c * CHUNK, (c + 1) * CHUNK)
                r1 = slice((c + 1) * CHUNK, (c + 2) * CHUNK)
                rhs = jnp.concatenate([vn[r0, cs], vn[r1, cs]], axis=1)
                mixed = jnp.dot(tril_w[g], rhs, preferred_element_type=_F32) + bias
                ya_ref[base + r0.start:base + r0.stop, cs] = (
                    u[r0, cs] * mixed[:, :A_GROUP_DIM]).astype(_BF16)
                ya_ref[base + r1.start:base + r1.stop, cs] = (
                    u[r1, cs] * mixed[:, A_GROUP_DIM:]).astype(_BF16)

        p = proj[:, 2 * A_WIDTH:]
        cur = jnp.concatenate([halo, p], axis=0)
        halo = p[rows - POOL_HALO:, :]
        cur_win = 1
        pos = j * ROW_TILE + base + lax.broadcasted_iota(jnp.int32, (rows, B_GROUP_DIM), 0)
        tokens_so_far = (pos + 1).astype(_F32)
        for g, win in enumerate(POOL_WINDOWS):
            cs = slice(g * B_GROUP_DIM, (g + 1) * B_GROUP_DIM)
            while cur_win < win:
                cur = cur + pltpu.roll(cur, cur_win, axis=0)
                cur_win *= 2
            assert cur_win == win and win <= POOL_HALO
            count = jnp.minimum(tokens_so_far, float(win))
            pooled = (cur[POOL_HALO:, :B_GROUP_DIM] / count - p[:, cs]).astype(_BF16)
            y = jnp.dot(pooled, pw_ref[g], preferred_element_type=_F32)
            yb_ref[base:base + rows, cs] = (y * ps_ref[:, cs]).astype(_BF16)
            cur = cur[:, B_GROUP_DIM:]
    carry_ref[...] = halo


def _even_mixer(h, g0, w_in, ln_g, ln_b, w_s, b_s_t, pool_w, pool_scale, casts):
    bsz, seq, _ = h.shape
    grid = (bsz, seq // ROW_TILE)
    out = jax.ShapeDtypeStruct((bsz, seq, A_WIDTH), _BF16)
    cast_in, cast_out, cast_shapes = _cast_plan(casts, grid)
    return pl.pallas_call(
        functools.partial(_even_mixer_kernel, len(casts)),
        grid=grid,
        in_specs=[
            _row_spec(D_MODEL),
            _const_spec((1, D_MODEL)),
            _const_spec((D_MODEL, EVEN_IN)),
            _const_spec((1, A_WIDTH)),
            _const_spec((1, A_WIDTH)),
            _const_spec((A_GROUPS, CHUNK, CHUNK)),
            _const_spec((CHUNK, A_GROUPS)),
            _const_spec((len(POOL_WINDOWS), B_GROUP_DIM, B_GROUP_DIM)),
            _const_spec((1, B_WIDTH)),
        ] + cast_in,
        out_specs=[_row_spec(A_WIDTH), _row_spec(B_WIDTH)] + cast_out,
        out_shape=[out, out] + cast_shapes,
        scratch_shapes=[
            pltpu.VMEM((POOL_HALO, B_WIDTH), _F32),
            pltpu.VMEM((PROJ_SLOTS, ROW_TILE // PROJ_SUBTILES, EVEN_IN), _F32),
        ],
        compiler_params=pltpu.CompilerParams(
            dimension_semantics=("arbitrary", "arbitrary"),
            vmem_limit_bytes=VMEM_LIMIT_BYTES),
        name="even_mixer",
    )(h, g0, w_in, ln_g, ln_b, w_s, b_s_t, pool_w, pool_scale, *[w for w, _ in casts])


def _odd_proj_kernel(x_ref, g_ref, w_in_ref, cw_ref, q_ref, k_ref, vt_ref, yd_ref,
                     carry_ref, proj_ref):
    j = pl.program_id(1)
    rows = ROW_TILE // PROJ_SUBTILES

    @pl.when(j == 0)
    def _():
        carry_ref[...] = jnp.zeros_like(carry_ref)

    halo = carry_ref[...]
    for sub, proj in _norm_project(x_ref, g_ref, w_in_ref, proj_ref):
        rs = slice(sub * rows, (sub + 1) * rows)
        q_ref[rs, :] = (proj[:, :C_WIDTH] * (C_QK_DIM ** -0.5 * LOG2E)).astype(_BF16)
        k_ref[rs, :] = proj[:, C_WIDTH:2 * C_WIDTH].astype(_BF16)
        for kt in range(rows // ATT_TILE):
            v_tile = proj[kt * ATT_TILE:(kt + 1) * ATT_TILE, 2 * C_WIDTH:3 * C_WIDTH]
            vt_ref[sub * (rows // ATT_TILE) + kt] = v_tile.T.astype(_BF16)
        o = 3 * C_WIDTH
        bg = proj[:, o:o + D_WIDTH]
        z = proj[:, o + D_WIDTH:o + 2 * D_WIDTH] * proj[:, o + 2 * D_WIDTH:]
        ext = jnp.concatenate([halo, z], axis=0)
        halo = z[rows - CONV_HALO:, :]
        y = cw_ref[CONV_WIDTH - 1:CONV_WIDTH, :] * z
        for t in range(CONV_WIDTH - 1):
            shift = CONV_WIDTH - 1 - t
            y = y + cw_ref[t:t + 1, :] * pltpu.roll(ext, shift, axis=0)[CONV_HALO:, :]
        yd_ref[rs, :] = (bg * y).astype(_BF16)
    carry_ref[...] = halo


def _odd_proj(h, g0, w_in, conv_w):
    bsz, seq, _ = h.shape
    grid = (bsz, seq // ROW_TILE)
    out = jax.ShapeDtypeStruct((bsz, seq, C_WIDTH), _BF16)
    assert (ROW_TILE // PROJ_SUBTILES) % ATT_TILE == 0
    out_t = jax.ShapeDtypeStruct((bsz, seq // ATT_TILE, C_WIDTH, ATT_TILE), _BF16)
    spec_t = pl.BlockSpec((None, ROW_TILE // ATT_TILE, C_WIDTH, ATT_TILE),
                          lambda b, j: (b, j, 0, 0))
    return pl.pallas_call(
        _odd_proj_kernel,
        grid=grid,
        in_specs=[
            _row_spec(D_MODEL),
            _const_spec((1, D_MODEL)),
            _const_spec((D_MODEL, ODD_IN)),
            _const_spec((CONV_WIDTH, D_WIDTH)),
        ],
        out_specs=[_row_spec(C_WIDTH), _row_spec(C_WIDTH), spec_t, _row_spec(D_WIDTH)],
        out_shape=[out, out, out_t, out],
        scratch_shapes=[
            pltpu.VMEM((CONV_HALO, D_WIDTH), _F32),
            pltpu.VMEM((PROJ_SLOTS, ROW_TILE // PROJ_SUBTILES, ODD_IN), _F32),
        ],
        compiler_params=pltpu.CompilerParams(
            dimension_semantics=("arbitrary", "arbitrary"),
            vmem_limit_bytes=VMEM_LIMIT_BYTES),
        name="odd_proj",
    )(h, g0, w_in, conv_w)


def _t5_bucket_upper_bounds(max_dist):
    d = np.arange(max_dist, dtype=np.int32)
    max_exact = REL_BUCKETS // 2
    nf = np.maximum(d, 1).astype(np.float32)
    large = max_exact + (np.log(nf / np.float32(max_exact))
                         / np.float32(math.log(REL_MAX_DIST / max_exact))
                         * np.float32(REL_BUCKETS - max_exact)).astype(np.int32)
    large = np.minimum(large, REL_BUCKETS - 1)
    bucket = np.where(d < max_exact, d, large)
    assert np.all(np.diff(bucket) >= 0)
    return bucket, {int(b): int(d[bucket == b].max()) for b in np.unique(bucket)}


def _attn_kernel(bucket_hi, lambda_init,
                 tab_ref, q_ref, k_ref, vt_ref, lam_ref, sg_ref, o_ref,
                 qq_ref, bias_ref, *scratch):
    t = ATT_TILE
    blk = pl.program_id(1)
    last_bucket = REL_BUCKETS - 1

    @pl.when(blk == 0)
    def _():
        kpos = lax.broadcasted_iota(jnp.int32, (t, t), 0)
        qpos = lax.broadcasted_iota(jnp.int32, (t, t), 1)
        for delta in range(2):
            d = qpos - kpos + delta * t
            for hd in range(C_HEADS):
                far = tab_ref[last_bucket, hd]
                val = jnp.zeros((t, t), _F32)
                for b in sorted(bucket_hi, reverse=True):
                    if b == last_bucket:
                        continue
                    val = jnp.where(d <= bucket_hi[b], (tab_ref[b, hd] - far) * LOG2E, val)
                val = jnp.where(d >= 0, val, MASK_VALUE)
                bias_ref[delta, hd] = jnp.concatenate([val, val], axis=1)

    def query_tile(sub, carry):
        rows = pl.ds(pl.multiple_of(sub * t, t), t)
        _attn_query_tile(lambda_init, blk * ATT_TILES_PER_STEP + sub,
                         q_ref.at[rows], k_ref, vt_ref, lam_ref, sg_ref, o_ref.at[sub],
                         qq_ref, bias_ref, *scratch)
        return carry

    lax.fori_loop(0, ATT_TILES_PER_STEP, query_tile, 0)


def _attn_query_tile(lambda_init, i, q_ref, k_ref, vt_ref, lam_ref, sg_ref, o_ref,
                     qq_ref, bias_ref, m_ref, acc_ref, s_ref, p_ref, al_ref):
    t = ATT_TILE
    hd_dim = 2 * C_QK_DIM
    heads = range(C_HEADS)

    feat = lax.broadcasted_iota(jnp.int32, (hd_dim, t), 0)
    for hd in heads:
        qt = q_ref[:, hd * hd_dim:(hd + 1) * hd_dim].astype(_F32).T
        qq_ref[hd] = jnp.concatenate(
            [jnp.where(feat < C_QK_DIM, qt, 0.0), jnp.where(feat >= C_QK_DIM, qt, 0.0)],
            axis=1).astype(_BF16)
    m_ref[...] = jnp.full_like(m_ref, MASK_VALUE)
    acc_ref[...] = jnp.zeros_like(acc_ref)
    p_ref[C_HEADS - 1] = jnp.zeros((t, 2 * t), _BF16)
    al_ref[C_HEADS - 1] = jnp.ones((1, 2 * t), _F32)

    def scores(jk, hd):
        start = pl.multiple_of(jk * t, t)
        k_t = k_ref[pl.ds(start, t), hd * hd_dim:(hd + 1) * hd_dim]
        s_ref[hd] = jnp.dot(k_t, qq_ref[hd], preferred_element_type=_F32)

    def values(jk, hd):
        vt = vt_ref[jk, hd * C_V_DIM:(hd + 1) * C_V_DIM, :]
        vt_ones = jnp.concatenate([vt, jnp.ones((SUM_ROWS, t), _BF16)], axis=0)
        acc_ref[hd] = al_ref[hd] * acc_ref[hd] + jnp.dot(
            vt_ones, p_ref[hd], preferred_element_type=_F32)

    def step(jk, delta, jk_next):
        for hd in heads:
            ahead = hd + SCORE_LOOKAHEAD
            if ahead < C_HEADS:
                scores(jk, ahead)
            elif jk_next is not None:
                scores(jk_next, ahead - C_HEADS)
            s = s_ref[hd]
            if delta is not None:
                s = s + bias_ref[delta, hd]
            m_prev = m_ref[hd]
            m_next = jnp.maximum(m_prev, jnp.max(s, axis=0, keepdims=True))
            alpha = jnp.exp2(m_prev - m_next)
            p = jnp.exp2(s - m_next)
            m_ref[hd] = m_next
            if hd == 0:
                values(jnp.maximum(jk - 1, 0), C_HEADS - 1)
            else:
                values(jk, hd - 1)
            p_ref[hd] = p.astype(_BF16)
            al_ref[hd] = alpha

    for hd in range(SCORE_LOOKAHEAD):
        scores(0, hd)

    n_far = jnp.maximum(i - 1, 0)

    def far_run(first, count):
        for n in range(count):
            step(first + n, None, first + n + 1)

    def far_trip(r, carry):
        far_run(FAR_UNROLL * r, FAR_UNROLL)
        return carry

    trips = n_far // FAR_UNROLL
    lax.fori_loop(0, trips, far_trip, 0)
    done = FAR_UNROLL * trips
    run = FAR_UNROLL // 2
    while run >= 2:
        take = (n_far - done) >= run
        pl.when(take)(functools.partial(far_run, done, run))
        done = done + jnp.where(take, run, 0)
        run //= 2

    def finish(odd_far_tile, sub_diagonal):
        if odd_far_tile:
            step(n_far - 1, None, n_far)
        if sub_diagonal:
            step(i - 1, 1, i)
        step(i, 0, None)
        values(i, C_HEADS - 1)
        lp = lam_ref[...]
        lam = (jnp.exp(jnp.sum(lp[0:1] * lp[1:2], axis=-1, keepdims=True))
               - jnp.exp(jnp.sum(lp[2:3] * lp[3:4], axis=-1, keepdims=True)) + lambda_init)
        for hd in heads:
            inv_l = 1.0 / acc_ref[hd, C_V_DIM:C_V_DIM + 1, :]
            a = (acc_ref[hd, :C_V_DIM, :t] * inv_l[:, :t]
                 - acc_ref[hd, :C_V_DIM, t:] * (lam * inv_l[:, t:]))
            ms = jnp.mean(a * a, axis=0, keepdims=True)
            y = a * (lax.rsqrt(ms + EPS) * (1.0 - lambda_init)) * sg_ref[...]
            o_ref[hd * C_V_DIM:(hd + 1) * C_V_DIM, :] = y.astype(_BF16)

    odd_far = n_far % 2 == 1
    pl.when(i == 0)(lambda: finish(False, False))
    pl.when(jnp.logical_and(i >= 1, jnp.logical_not(odd_far)))(lambda: finish(False, True))
    pl.when(odd_far)(lambda: finish(True, True))


def _diff_attention(q, k, vt, rel_table, lam_params, subln_g, lambda_init):
    bsz, seq, _ = q.shape
    t = ATT_TILE
    bucket, bucket_hi = _t5_bucket_upper_bounds(seq)
    assert np.all(bucket[t + 1:] == REL_BUCKETS - 1)
    rows = t * ATT_TILES_PER_STEP
    grid = (bsz, seq // rows)
    kernel = functools.partial(_attn_kernel, bucket_hi, lambda_init)
    return pl.pallas_call(
        kernel,
        grid=grid,
        in_specs=[
            pl.BlockSpec(memory_space=pltpu.SMEM),
            pl.BlockSpec((None, rows, C_WIDTH), lambda b, i: (b, i, 0)),
            pl.BlockSpec((None, seq, C_WIDTH), lambda b, i: (b, 0, 0)),
            pl.BlockSpec((None, seq // t, C_WIDTH, t), lambda b, i: (b, 0, 0, 0)),
            pl.BlockSpec((4, C_QK_DIM), lambda b, i: (0, 0)),
            pl.BlockSpec((C_V_DIM, 1), lambda b, i: (0, 0)),
        ],
        out_specs=pl.BlockSpec((None, ATT_TILES_PER_STEP, C_WIDTH, t), lambda b, i: (b, i, 0, 0)),
        out_shape=jax.ShapeDtypeStruct((bsz, seq // t, C_WIDTH, t), _BF16),
        scratch_shapes=[
            pltpu.VMEM((C_HEADS, 2 * C_QK_DIM, 2 * t), _BF16),
            pltpu.VMEM((2, C_HEADS, t, 2 * t), _F32),
            pltpu.VMEM((C_HEADS, 1, 2 * t), _F32),
            pltpu.VMEM((C_HEADS, C_V_DIM + SUM_ROWS, 2 * t), _F32),
            pltpu.VMEM((C_HEADS, t, 2 * t), _F32),
            pltpu.VMEM((C_HEADS, t, 2 * t), _BF16),
            pltpu.VMEM((C_HEADS, 1, 2 * t), _F32),
        ],
        compiler_params=pltpu.CompilerParams(
            dimension_semantics=("arbitrary", "arbitrary"),
            vmem_limit_bytes=VMEM_LIMIT_BYTES),
        name="diff_attention",
    )(rel_table, q, k, vt, lam_params, subln_g)


def _out_mlp_kernel(n_cast, ya_tiled, h_ref, ya_ref, yb_ref, g_ref, w_out_ref, w1_ref, w2_ref,
                    *rest):
    cast_in, o_ref, cast_out = rest[:n_cast], rest[n_cast], rest[n_cast + 1:]
    _cast_slabs(cast_in + cast_out)
    half = D_MODEL // 2
    sub = MLP_ROW_TILE // MLP_SUBTILES
    rows = [slice(r * sub, (r + 1) * sub) for r in range(MLP_SUBTILES)]

    def out_proj_a(r):
        if not ya_tiled:
            return jnp.dot(ya_ref[rows[r], :], w_out_ref[:half, :], preferred_element_type=_F32)
        per = sub // ATT_TILE
        parts = [lax.dot_general(ya_ref[r * per + kt], w_out_ref[:half, :],
                                 (((0,), (0,)), ((), ())), preferred_element_type=_F32)
                 for kt in range(per)]
        return parts[0] if per == 1 else jnp.concatenate(parts, axis=0)

    ys = [out_proj_a(r)
          + jnp.dot(yb_ref[rs, :], w_out_ref[half:, :], preferred_element_type=_F32)
          for r, rs in enumerate(rows)]
    h1s = [h_ref[rs, :] + _rms(y, g_ref[1:2, :]) for rs, y in zip(rows, ys)]
    hns = [_rms(h1, g_ref[2:3, :]).astype(_BF16) for h1 in h1s]
    for rs, h1, hn in zip(rows, h1s, hns):
        acc = jnp.zeros((sub, D_MODEL), _F32)
        for c in range(D_FF // FF_CHUNK):
            cs = slice(c * FF_CHUNK, (c + 1) * FF_CHUNK)
            a = jnp.dot(hn, w1_ref[:, cs], preferred_element_type=_F32)
            a = jnp.square(jnp.maximum(a, 0.0)).astype(_BF16)
            acc = acc + jnp.dot(a, w2_ref[cs, :], preferred_element_type=_F32)
        o_ref[rs, :] = h1 + _rms(acc, g_ref[3:4, :])


def _out_mlp(h, ya, yb, g, w_out, w1, w2, casts=()):
    bsz, seq, _ = h.shape
    grid = (bsz, seq // MLP_ROW_TILE)
    row_spec = functools.partial(_row_spec, tile=MLP_ROW_TILE)
    cast_in, cast_out, cast_shapes = _cast_plan(casts, grid)
    ya_tiled = ya.ndim == 4
    if ya_tiled:
        assert (MLP_ROW_TILE // MLP_SUBTILES) % ATT_TILE == 0
        ya_spec = pl.BlockSpec((None, MLP_ROW_TILE // ATT_TILE, D_MODEL // 2, ATT_TILE),
                               lambda b, j: (b, j, 0, 0))
    else:
        ya_spec = row_spec(D_MODEL // 2)
    return pl.pallas_call(
        functools.partial(_out_mlp_kernel, len(casts), ya_tiled),
        grid=grid,
        in_specs=[
            row_spec(D_MODEL),
            ya_spec,
            row_spec(D_MODEL // 2),
            _const_spec((4, D_MODEL)),
            _const_spec((D_MODEL, D_MODEL)),
            _const_spec((D_MODEL, D_FF)),
            _const_spec((D_FF, D_MODEL)),
        ] + cast_in,
        out_specs=[row_spec(D_MODEL)] + cast_out,
        out_shape=[jax.ShapeDtypeStruct(h.shape, h.dtype)] + cast_shapes,
        compiler_params=pltpu.CompilerParams(
            dimension_semantics=("arbitrary", "arbitrary"),
            vmem_limit_bytes=VMEM_LIMIT_BYTES),
        name="out_mlp",
    )(h, ya, yb, g, w_out, w1, w2, *[w for w, _ in casts])


def kernel(x, rel_bias_table, norm_g, even_w_in, even_ln_g, even_ln_b, even_spatial_w,
           even_spatial_b, even_pool_w, even_pool_scale, even_w_out, odd_w_in, odd_lambda,
           odd_subln_g, odd_conv_w, odd_w_out, ffn_w1, ffn_w2):
    depth = norm_g.shape[0]
    bf = lambda w: w.astype(_BF16)

    def layer_weights(layer):
        mixer = (even_w_in, even_w_out) if layer % 2 == 0 else (odd_w_in, odd_w_out)
        return [(mixer[0], layer // 2), (mixer[1], layer // 2),
                (ffn_w1, layer), (ffn_w2, layer)]

    first = layer_weights(0)
    w_in = bf(first[0][0][first[0][1]])
    w_out = w1 = w2 = None
    h = x
    for layer in range(depth):
        g = norm_g[layer]
        own_casts = [] if w_out is not None else layer_weights(layer)[1:]
        if layer % 2 == 0:
            e = layer // 2
            ya, yb, *cast = _even_mixer(
                h, g[0:1], w_in, even_ln_g[e][None], even_ln_b[e][None],
                even_spatial_w[e], even_spatial_b[e].T, bf(even_pool_w[e]),
                even_pool_scale[e][None], own_casts)
        else:
            o = layer // 2
            lambda_init = 0.8 - 0.6 * math.exp(-0.3 * layer)
            q, k, vt, yb = _odd_proj(h, g[0:1], w_in, odd_conv_w[o])
            ya = _diff_attention(q, k, vt, rel_bias_table, odd_lambda[o],
                                 odd_subln_g[o][:, None], lambda_init)
            cast = [bf(w[i]) for w, i in own_casts]
        if own_casts:
            w_out, w1, w2 = cast
        next_casts = layer_weights(layer + 1) if layer + 1 < depth else []
        h, *nxt = _out_mlp(h, ya, yb, g, w_out, w1, w2, next_casts)
        w_in, w_out, w1, w2 = nxt if nxt else (None,) * 4
    return h
```

```python
import functools
import math

import numpy as np
import jax
import jax.numpy as jnp
from jax import lax
from jax.experimental import pallas as pl
from jax.experimental.pallas import tpu as pltpu

D_MODEL = 1024
A_WIDTH = 512
A_GROUPS = 4
A_GROUP_DIM = 128
CHUNK = 128
B_WIDTH = 512
POOL_WINDOWS = (2, 4, 8, 16)
B_GROUP_DIM = 128
C_HEADS = 4
C_QK_DIM = 64
C_V_DIM = 128
C_WIDTH = 512
D_WIDTH = 512
CONV_WIDTH = 3
REL_BUCKETS = 32
REL_MAX_DIST = 128
D_FF = 4096
EPS = 1e-6
EVEN_IN = 2 * A_WIDTH + B_WIDTH
ODD_IN = 2 * C_WIDTH + C_WIDTH + 3 * D_WIDTH

ROW_TILE = 1024
ATT_TILE = 256
ATT_TILES_PER_STEP = 4
FF_CHUNK = 1024
PROJ_SUBTILES = 4
PROJ_SLOTS = 3
MLP_ROW_TILE = 1024
MLP_SUBTILES = 4
POOL_HALO = 16
CONV_HALO = 8
LOG2E = math.log2(math.e)
FAR_UNROLL = 8
SCORE_LOOKAHEAD = 1
SUM_ROWS = 16
MASK_VALUE = -1e30
BF16_SUBLANES = 16
VMEM_LIMIT_BYTES = 52 * 1024 * 1024

_F32 = jnp.float32
_BF16 = jnp.bfloat16


def _rms(x, g):
    return x * lax.rsqrt(jnp.mean(x * x, axis=-1, keepdims=True) + EPS) * g


def _gelu_tanh(x):
    c = math.sqrt(2.0 / math.pi)
    hx = 0.5 * x
    return hx + hx * jnp.tanh(x * (c + (c * 0.044715) * (x * x)))


def _norm_project(x_ref, g_ref, w_ref, proj_ref):
    rows = x_ref.shape[0] // PROJ_SUBTILES
    slots = proj_ref.shape[0]
    ahead = slots - 1

    def project(r):
        hn = _rms(x_ref[r * rows:(r + 1) * rows, :], g_ref[...]).astype(_BF16)
        proj_ref[r % slots] = jnp.dot(hn, w_ref[...], preferred_element_type=_F32)

    for r in range(min(ahead, PROJ_SUBTILES)):
        project(r)
    for r in range(PROJ_SUBTILES):
        if r + ahead < PROJ_SUBTILES:
            project(r + ahead)
        yield r, proj_ref.at[r % slots]


def _const_spec(shape):
    nd = len(shape)
    return pl.BlockSpec(shape, lambda *_: (0,) * nd, pipeline_mode=pl.Buffered(1))


def _row_spec(width, col=0, tile=None):
    return pl.BlockSpec((None, tile or ROW_TILE, width), lambda b, j: (b, j, col))


def _cast_plan(weights, grid):
    steps = grid[0] * grid[1]
    in_specs, out_specs, out_shapes = [], [], []
    for w, layer in weights:
        _, rows, cols = w.shape
        slab = rows // steps
        assert slab * steps == rows and slab % BF16_SUBLANES == 0
        in_specs.append(pl.BlockSpec(
            (None, slab, cols), lambda b, j, layer=layer: (layer, b * grid[1] + j, 0)))
        out_specs.append(pl.BlockSpec((slab, cols), lambda b, j: (b * grid[1] + j, 0)))
        out_shapes.append(jax.ShapeDtypeStruct((rows, cols), _BF16))
    return in_specs, out_specs, out_shapes


def _cast_slabs(refs):
    n = len(refs) // 2
    for src, dst in zip(refs[:n], refs[n:]):
        dst[...] = src[...].astype(_BF16)


def _even_mixer_kernel(n_cast, x_ref, g_ref, w_in_ref, ln_g_ref, ln_b_ref, ws_ref, bs_ref,
                       pw_ref, ps_ref, *rest):
    cast_in, (ya_ref, yb_ref), rest = rest[:n_cast], rest[n_cast:n_cast + 2], rest[n_cast + 2:]
    cast_out, (carry_ref, proj_ref) = rest[:n_cast], rest[n_cast:]
    _cast_slabs(cast_in + cast_out)
    j = pl.program_id(1)
    rows = ROW_TILE // PROJ_SUBTILES

    @pl.when(j == 0)
    def _():
        carry_ref[...] = jnp.zeros_like(carry_ref)

    row = lax.broadcasted_iota(jnp.int32, (CHUNK, CHUNK), 0)
    col = lax.broadcasted_iota(jnp.int32, (CHUNK, CHUNK), 1)
    tril_w = [jnp.where(row >= col, ws_ref[g], 0.0).astype(_BF16) for g in range(A_GROUPS)]
    halo = carry_ref[...]
    for sub, proj in _norm_project(x_ref, g_ref, w_in_ref, proj_ref):
        base = sub * rows

        z = _gelu_tanh(proj[:, :2 * A_WIDTH])
        u = z[:, :A_WIDTH]
        v = z[:, A_WIDTH:]
        mu = jnp.mean(v, axis=-1, keepdims=True)
        vc = v - mu
        vn = vc * lax.rsqrt(jnp.mean(vc * vc, axis=-1, keepdims=True) + EPS)
        vn = (vn * ln_g_ref[...] + ln_b_ref[...]).astype(_BF16)
        for g in range(A_GROUPS):
            cs = slice(g * A_GROUP_DIM, (g + 1) * A_GROUP_DIM)
            bias = bs_ref[:, g:g + 1]
            for c in range(0, rows // CHUNK, 2):
                r0 = slice(c * CHUNK, (c + 1) * CHUNK)
                r1 = slice((c + 1) * CHUNK, (c + 2) * CHUNK)
                rhs = jnp.concatenate([vn[r0, cs], vn[r1, cs]], axis=1)
                mixed = jnp.dot(tril_w[g], rhs, preferred_element_type=_F32) + bias
                ya_ref[base + r0.start:base + r0.stop, cs] = (
                    u[r0, cs] * mixed[:, :A_GROUP_DIM]).astype(_BF16)
                ya_ref[base + r1.start:base + r1.stop, cs] = (
                    u[r1, cs] * mixed[:, A_GROUP_DIM:]).astype(_BF16)

        p = proj[:, 2 * A_WIDTH:]
        cur = jnp.concatenate([halo, p], axis=0)
        halo = p[rows - POOL_HALO:, :]
        cur_win = 1
        pos = j * ROW_TILE + base + lax.broadcasted_iota(jnp.int32, (rows, B_GROUP_DIM), 0)
        tokens_so_far = (pos + 1).astype(_F32)
        for g, win in enumerate(POOL_WINDOWS):
            cs = slice(g * B_GROUP_DIM, (g + 1) * B_GROUP_DIM)
            while cur_win < win:
                cur = cur + pltpu.roll(cur, cur_win, axis=0)
                cur_win *= 2
            assert cur_win == win and win <= POOL_HALO
            count = jnp.minimum(tokens_so_far, float(win))
            pooled = (cur[POOL_HALO:, :B_GROUP_DIM] / count - p[:, cs]).astype(_BF16)
            y = jnp.dot(pooled, pw_ref[g], preferred_element_type=_F32)
            yb_ref[base:base + rows, cs] = (y * ps_ref[:, cs]).astype(_BF16)
            cur = cur[:, B_GROUP_DIM:]
    carry_ref[...] = halo


def _even_mixer(h, g0, w_in, ln_g, ln_b, w_s, b_s_t, pool_w, pool_scale, casts):
    bsz, seq, _ = h.shape
    grid = (bsz, seq // ROW_TILE)
    out = jax.ShapeDtypeStruct((bsz, seq, A_WIDTH), _BF16)
    cast_in, cast_out, cast_shapes = _cast_plan(casts, grid)
    return pl.pallas_call(
        functools.partial(_even_mixer_kernel, len(casts)),
        grid=grid,
        in_specs=[
            _row_spec(D_MODEL),
            _const_spec((1, D_MODEL)),
            _const_spec((D_MODEL, EVEN_IN)),
            _const_spec((1, A_WIDTH)),
            _const_spec((1, A_WIDTH)),
            _const_spec((A_GROUPS, CHUNK, CHUNK)),
            _const_spec((CHUNK, A_GROUPS)),
            _const_spec((len(POOL_WINDOWS), B_GROUP_DIM, B_GROUP_DIM)),
            _const_spec((1, B_WIDTH)),
        ] + cast_in,
        out_specs=[_row_spec(A_WIDTH), _row_spec(B_WIDTH)] + cast_out,
        out_shape=[out, out] + cast_shapes,
        scratch_shapes=[
            pltpu.VMEM((POOL_HALO, B_WIDTH), _F32),
            pltpu.VMEM((PROJ_SLOTS, ROW_TILE // PROJ_SUBTILES, EVEN_IN), _F32),
        ],
        compiler_params=pltpu.CompilerParams(
            dimension_semantics=("arbitrary", "arbitrary"),
            vmem_limit_bytes=VMEM_LIMIT_BYTES),
        name="even_mixer",
    )(h, g0, w_in, ln_g, ln_b, w_s, b_s_t, pool_w, pool_scale, *[w for w, _ in casts])


def _odd_proj_kernel(x_ref, g_ref, w_in_ref, cw_ref, q_ref, k_ref, vt_ref, yd_ref,
                     carry_ref, proj_ref):
    j = pl.program_id(1)
    rows = ROW_TILE // PROJ_SUBTILES

    @pl.when(j == 0)
    def _():
        carry_ref[...] = jnp.zeros_like(carry_ref)

    halo = carry_ref[...]
    for sub, proj in _norm_project(x_ref, g_ref, w_in_ref, proj_ref):
        rs = slice(sub * rows, (sub + 1) * rows)
        q_ref[rs, :] = (proj[:, :C_WIDTH] * (C_QK_DIM ** -0.5 * LOG2E)).astype(_BF16)
        k_ref[rs, :] = proj[:, C_WIDTH:2 * C_WIDTH].astype(_BF16)
        for kt in range(rows // ATT_TILE):
            v_tile = proj[kt * ATT_TILE:(kt + 1) * ATT_TILE, 2 * C_WIDTH:3 * C_WIDTH]
            vt_ref[sub * (rows // ATT_TILE) + kt] = v_tile.T.astype(_BF16)
        o = 3 * C_WIDTH
        bg = proj[:, o:o + D_WIDTH]
        z = proj[:, o + D_WIDTH:o + 2 * D_WIDTH] * proj[:, o + 2 * D_WIDTH:]
        ext = jnp.concatenate([halo, z], axis=0)
        halo = z[rows - CONV_HALO:, :]
        y = cw_ref[CONV_WIDTH - 1:CONV_WIDTH, :] * z
        for t in range(CONV_WIDTH - 1):
            shift = CONV_WIDTH - 1 - t
            y = y + cw_ref[t:t + 1, :] * pltpu.roll(ext, shift, axis=0)[CONV_HALO:, :]
        yd_ref[rs, :] = (bg * y).astype(_BF16)
    carry_ref[...] = halo


def _odd_proj(h, g0, w_in, conv_w):
    bsz, seq, _ = h.shape
    grid = (bsz, seq // ROW_TILE)
    out = jax.ShapeDtypeStruct((bsz, seq, C_WIDTH), _BF16)
    assert (ROW_TILE // PROJ_SUBTILES) % ATT_TILE == 0
    out_t = jax.ShapeDtypeStruct((bsz, seq // ATT_TILE, C_WIDTH, ATT_TILE), _BF16)
    spec_t = pl.BlockSpec((None, ROW_TILE // ATT_TILE, C_WIDTH, ATT_TILE),
                          lambda b, j: (b, j, 0, 0))
    return pl.pallas_call(
        _odd_proj_kernel,
        grid=grid,
        in_specs=[
            _row_spec(D_MODEL),
            _const_spec((1, D_MODEL)),
            _const_spec((D_MODEL, ODD_IN)),
            _const_spec((CONV_WIDTH, D_WIDTH)),
        ],
        out_specs=[_row_spec(C_WIDTH), _row_spec(C_WIDTH), spec_t, _row_spec(D_WIDTH)],
        out_shape=[out, out, out_t, out],
        scratch_shapes=[
            pltpu.VMEM((CONV_HALO, D_WIDTH), _F32),
            pltpu.VMEM((PROJ_SLOTS, ROW_TILE // PROJ_SUBTILES, ODD_IN), _F32),
        ],
        compiler_params=pltpu.CompilerParams(
            dimension_semantics=("arbitrary", "arbitrary"),
            vmem_limit_bytes=VMEM_LIMIT_BYTES),
        name="odd_proj",
    )(h, g0, w_in, conv_w)


def _t5_bucket_upper_bounds(max_dist):
    d = np.arange(max_dist, dtype=np.int32)
    max_exact = REL_BUCKETS // 2
    nf = np.maximum(d, 1).astype(np.float32)
    large = max_exact + (np.log(nf / np.float32(max_exact))
                         / np.float32(math.log(REL_MAX_DIST / max_exact))
                         * np.float32(REL_BUCKETS - max_exact)).astype(np.int32)
    large = np.minimum(large, REL_BUCKETS - 1)
    bucket = np.where(d < max_exact, d, large)
    assert np.all(np.diff(bucket) >= 0)
    return bucket, {int(b): int(d[bucket == b].max()) for b in np.unique(bucket)}


def _attn_kernel(bucket_hi, lambda_init,
                 tab_ref, q_ref, k_ref, vt_ref, lam_ref, sg_ref, o_ref,
                 qq_ref, bias_ref, *scratch):
    t = ATT_TILE
    blk = pl.program_id(1)
    last_bucket = REL_BUCKETS - 1

    @pl.when(jnp.logical_and(pl.program_id(0) == 0, blk == 0))
    def _():
        kpos = lax.broadcasted_iota(jnp.int32, (t, t), 0)
        qpos = lax.broadcasted_iota(jnp.int32, (t, t), 1)
        for delta in range(2):
            d = qpos - kpos + delta * t
            for hd in range(C_HEADS):
                far = tab_ref[last_bucket, hd]
                val = jnp.zeros((t, t), _F32)
                for b in sorted(bucket_hi, reverse=True):
                    if b == last_bucket:
                        continue
                    val = jnp.where(d <= bucket_hi[b], (tab_ref[b, hd] - far) * LOG2E, val)
                val = jnp.where(d >= 0, val, MASK_VALUE)
                bias_ref[delta, hd] = jnp.concatenate([val, val], axis=1)

    def query_tile(sub, carry):
        rows = pl.ds(pl.multiple_of(sub * t, t), t)
        _attn_query_tile(lambda_init, blk * ATT_TILES_PER_STEP + sub,
                         q_ref.at[rows], k_ref, vt_ref, lam_ref, sg_ref, o_ref.at[sub],
                         qq_ref, bias_ref, *scratch)
        return carry

    lax.fori_loop(0, ATT_TILES_PER_STEP, query_tile, 0)


def _attn_query_tile(lambda_init, i, q_ref, k_ref, vt_ref, lam_ref, sg_ref, o_ref,
                     qq_ref, bias_ref, m_ref, acc_ref, s_ref, p_ref, al_ref):
    t = ATT_TILE
    hd_dim = 2 * C_QK_DIM
    heads = range(C_HEADS)

    feat = lax.broadcasted_iota(jnp.int32, (hd_dim, t), 0)
    for hd in heads:
        qt = q_ref[:, hd * hd_dim:(hd + 1) * hd_dim].astype(_F32).T
        qq_ref[hd] = jnp.concatenate(
            [jnp.where(feat < C_QK_DIM, qt, 0.0), jnp.where(feat >= C_QK_DIM, qt, 0.0)],
            axis=1).astype(_BF16)
    m_ref[...] = jnp.full_like(m_ref, MASK_VALUE)
    acc_ref[...] = jnp.zeros_like(acc_ref)
    p_ref[C_HEADS - 1] = jnp.zeros((t, 2 * t), _BF16)
    al_ref[C_HEADS - 1] = jnp.ones((1, 2 * t), _F32)

    def scores(jk, hd):
        start = pl.multiple_of(jk * t, t)
        k_t = k_ref[pl.ds(start, t), hd * hd_dim:(hd + 1) * hd_dim]
        s_ref[hd] = jnp.dot(k_t, qq_ref[hd], preferred_element_type=_F32)

    def values(jk, hd):
        vt = vt_ref[jk, hd * C_V_DIM:(hd + 1) * C_V_DIM, :]
        vt_ones = jnp.concatenate([vt, jnp.ones((SUM_ROWS, t), _BF16)], axis=0)
        acc_ref[hd] = al_ref[hd] * acc_ref[hd] + jnp.dot(
            vt_ones, p_ref[hd], preferred_element_type=_F32)

    def step(jk, delta, jk_next):
        for hd in heads:
            ahead = hd + SCORE_LOOKAHEAD
            if ahead < C_HEADS:
                scores(jk, ahead)
            elif jk_next is not None:
                scores(jk_next, ahead - C_HEADS)
            s = s_ref[hd]
            if delta is not None:
                s = s + bias_ref[delta, hd]
            m_prev = m_ref[hd]
            m_next = jnp.maximum(m_prev, jnp.max(s, axis=0, keepdims=True))
            alpha = jnp.exp2(m_prev - m_next)
            p = jnp.exp2(s - m_next)
            m_ref[hd] = m_next
            if hd == 0:
                values(jnp.maximum(jk - 1, 0), C_HEADS - 1)
            else:
                values(jk, hd - 1)
            p_ref[hd] = p.astype(_BF16)
            al_ref[hd] = alpha

    for hd in range(SCORE_LOOKAHEAD):
        scores(0, hd)

    n_far = jnp.maximum(i - 1, 0)

    def far_run(first, count):
        for n in range(count):
            step(first + n, None, first + n + 1)

    def far_trip(r, carry):
        far_run(FAR_UNROLL * r, FAR_UNROLL)
        return carry

    trips = n_far // FAR_UNROLL
    lax.fori_loop(0, trips, far_trip, 0)
    done = FAR_UNROLL * trips
    run = FAR_UNROLL // 2
    while run >= 2:
        take = (n_far - done) >= run
        pl.when(take)(functools.partial(far_run, done, run))
        done = done + jnp.where(take, run, 0)
        run //= 2

    def finish(odd_far_tile, sub_diagonal):
        if odd_far_tile:
            step(n_far - 1, None, n_far)
        if sub_diagonal:
            step(i - 1, 1, i)
        step(i, 0, None)
        values(i, C_HEADS - 1)
        lp = lam_ref[...]
        lam = (jnp.exp(jnp.sum(lp[0:1] * lp[1:2], axis=-1, keepdims=True))
               - jnp.exp(jnp.sum(lp[2:3] * lp[3:4], axis=-1, keepdims=True)) + lambda_init)
        for hd in heads:
            inv_l = 1.0 / acc_ref[hd, C_V_DIM:C_V_DIM + 1, :]
            a = (acc_ref[hd, :C_V_DIM, :t] * inv_l[:, :t]
                 - acc_ref[hd, :C_V_DIM, t:] * (lam * inv_l[:, t:]))
            ms = jnp.mean(a * a, axis=0, keepdims=True)
            y = a * (lax.rsqrt(ms + EPS) * (1.0 - lambda_init)) * sg_ref[...]
            o_ref[hd * C_V_DIM:(hd + 1) * C_V_DIM, :] = y.astype(_BF16)

    odd_far = n_far % 2 == 1
    pl.when(i == 0)(lambda: finish(False, False))
    pl.when(jnp.logical_and(i >= 1, jnp.logical_not(odd_far)))(lambda: finish(False, True))
    pl.when(odd_far)(lambda: finish(True, True))


def _diff_attention(q, k, vt, rel_table, lam_params, subln_g, lambda_init):
    bsz, seq, _ = q.shape
    t = ATT_TILE
    bucket, bucket_hi = _t5_bucket_upper_bounds(seq)
    assert np.all(bucket[t + 1:] == REL_BUCKETS - 1)
    rows = t * ATT_TILES_PER_STEP
    grid = (bsz, seq // rows)
    kernel = functools.partial(_attn_kernel, bucket_hi, lambda_init)
    return pl.pallas_call(
        kernel,
        grid=grid,
        in_specs=[
            pl.BlockSpec(memory_space=pltpu.SMEM),
            pl.BlockSpec((None, rows, C_WIDTH), lambda b, i: (b, i, 0)),
            pl.BlockSpec((None, seq, C_WIDTH), lambda b, i: (b, 0, 0)),
            pl.BlockSpec((None, seq // t, C_WIDTH, t), lambda b, i: (b, 0, 0, 0)),
            pl.BlockSpec((4, C_QK_DIM), lambda b, i: (0, 0)),
            pl.BlockSpec((C_V_DIM, 1), lambda b, i: (0, 0)),
        ],
        out_specs=pl.BlockSpec((None, ATT_TILES_PER_STEP, C_WIDTH, t), lambda b, i: (b, i, 0, 0)),
        out_shape=jax.ShapeDtypeStruct((bsz, seq // t, C_WIDTH, t), _BF16),
        scratch_shapes=[
            pltpu.VMEM((C_HEADS, 2 * C_QK_DIM, 2 * t), _BF16),
            pltpu.VMEM((2, C_HEADS, t, 2 * t), _F32),
            pltpu.VMEM((C_HEADS, 1, 2 * t), _F32),
            pltpu.VMEM((C_HEADS, C_V_DIM + SUM_ROWS, 2 * t), _F32),
            pltpu.VMEM((C_HEADS, t, 2 * t), _F32),
            pltpu.VMEM((C_HEADS, t, 2 * t), _BF16),
            pltpu.VMEM((C_HEADS, 1, 2 * t), _F32),
        ],
        compiler_params=pltpu.CompilerParams(
            dimension_semantics=("arbitrary", "arbitrary"),
            vmem_limit_bytes=VMEM_LIMIT_BYTES),
        name="diff_attention",
    )(rel_table, q, k, vt, lam_params, subln_g)


def _out_mlp_kernel(n_cast, ya_tiled, h_ref, ya_ref, yb_ref, g_ref, w_out_ref, w1_ref, w2_ref,
                    *rest):
    cast_in, o_ref, cast_out = rest[:n_cast], rest[n_cast], rest[n_cast + 1:]
    _cast_slabs(cast_in + cast_out)
    half = D_MODEL // 2
    sub = MLP_ROW_TILE // MLP_SUBTILES
    rows = [slice(r * sub, (r + 1) * sub) for r in range(MLP_SUBTILES)]

    def out_proj_a(r):
        if not ya_tiled:
            return jnp.dot(ya_ref[rows[r], :], w_out_ref[:half, :], preferred_element_type=_F32)
        per = sub // ATT_TILE
        parts = [lax.dot_general(ya_ref[r * per + kt], w_out_ref[:half, :],
                                 (((0,), (0,)), ((), ())), preferred_element_type=_F32)
                 for kt in range(per)]
        return parts[0] if per == 1 else jnp.concatenate(parts, axis=0)

    ys = [out_proj_a(r)
          + jnp.dot(yb_ref[rs, :], w_out_ref[half:, :], preferred_element_type=_F32)
          for r, rs in enumerate(rows)]
    h1s = [h_ref[rs, :] + _rms(y, g_ref[1:2, :]) for rs, y in zip(rows, ys)]
    hns = [_rms(h1, g_ref[2:3, :]).astype(_BF16) for h1 in h1s]
    for rs, h1, hn in zip(rows, h1s, hns):
        acc = jnp.zeros((sub, D_MODEL), _F32)
        for c in range(D_FF // FF_CHUNK):
            cs = slice(c * FF_CHUNK, (c + 1) * FF_CHUNK)
            a = jnp.dot(hn, w1_ref[:, cs], preferred_element_type=_F32)
            a = jnp.square(jnp.maximum(a, 0.0)).astype(_BF16)
            acc = acc + jnp.dot(a, w2_ref[cs, :], preferred_element_type=_F32)
        o_ref[rs, :] = h1 + _rms(acc, g_ref[3:4, :])


def _out_mlp(h, ya, yb, g, w_out, w1, w2, casts=()):
    bsz, seq, _ = h.shape
    grid = (bsz, seq // MLP_ROW_TILE)
    row_spec = functools.partial(_row_spec, tile=MLP_ROW_TILE)
    cast_in, cast_out, cast_shapes = _cast_plan(casts, grid)
    ya_tiled = ya.ndim == 4
    if ya_tiled:
        assert (MLP_ROW_TILE // MLP_SUBTILES) % ATT_TILE == 0
        ya_spec = pl.BlockSpec((None, MLP_ROW_TILE // ATT_TILE, D_MODEL // 2, ATT_TILE),
                               lambda b, j: (b, j, 0, 0))
    else:
        ya_spec = row_spec(D_MODEL // 2)
    return pl.pallas_call(
        functools.partial(_out_mlp_kernel, len(casts), ya_tiled),
        grid=grid,
        in_specs=[
            row_spec(D_MODEL),
            ya_spec,
            row_spec(D_MODEL // 2),
            _const_spec((4, D_MODEL)),
            _const_spec((D_MODEL, D_MODEL)),
            _const_spec((D_MODEL, D_FF)),
            _const_spec((D_FF, D_MODEL)),
        ] + cast_in,
        out_specs=[row_spec(D_MODEL)] + cast_out,
        out_shape=[jax.ShapeDtypeStruct(h.shape, h.dtype)] + cast_shapes,
        compiler_params=pltpu.CompilerParams(
            dimension_semantics=("arbitrary", "arbitrary"),
            vmem_limit_bytes=VMEM_LIMIT_BYTES),
        name="out_mlp",
    )(h, ya, yb, g, w_out, w1, w2, *[w for w, _ in casts])


def kernel(x, rel_bias_table, norm_g, even_w_in, even_ln_g, even_ln_b, even_spatial_w,
           even_spatial_b, even_pool_w, even_pool_scale, even_w_out, odd_w_in, odd_lambda,
           odd_subln_g, odd_conv_w, odd_w_out, ffn_w1, ffn_w2):
    depth = norm_g.shape[0]
    bf = lambda w: w.astype(_BF16)

    def layer_weights(layer):
        mixer = (even_w_in, even_w_out) if layer % 2 == 0 else (odd_w_in, odd_w_out)
        return [(mixer[0], layer // 2), (mixer[1], layer // 2),
                (ffn_w1, layer), (ffn_w2, layer)]

    first = layer_weights(0)
    w_in = bf(first[0][0][first[0][1]])
    w_out = w1 = w2 = None
    h = x
    for layer in range(depth):
        g = norm_g[layer]
        own_casts = [] if w_out is not None else layer_weights(layer)[1:]
        if layer % 2 == 0:
            e = layer // 2
            ya, yb, *cast = _even_mixer(
                h, g[0:1], w_in, even_ln_g[e][None], even_ln_b[e][None],
                even_spatial_w[e], even_spatial_b[e].T, bf(even_pool_w[e]),
                even_pool_scale[e][None], own_casts)
        else:
            o = layer // 2
            lambda_init = 0.8 - 0.6 * math.exp(-0.3 * layer)
            q, k, vt, yb = _odd_proj(h, g[0:1], w_in, odd_conv_w[o])
            ya = _diff_attention(q, k, vt, rel_bias_table, odd_lambda[o],
                                 odd_subln_g[o][:, None], lambda_init)
            cast = [bf(w[i]) for w, i in own_casts]
        if own_casts:
            w_out, w1, w2 = cast
        next_casts = layer_weights(layer + 1) if layer + 1 < depth else []
        h, *nxt = _out_mlp(h, ya, yb, g, w_out, w1, w2, next_casts)
        w_in, w_out, w1, w2 = nxt if nxt else (None,) * 4
    return h
```

```python
import functools
import math

import numpy as np
import jax
import jax.numpy as jnp
from jax import lax
from jax.experimental import pallas as pl
from jax.experimental.pallas import tpu as pltpu

D_MODEL = 1024
A_WIDTH = 512
A_GROUPS = 4
A_GROUP_DIM = 128
CHUNK = 128
B_WIDTH = 512
POOL_WINDOWS = (2, 4, 8, 16)
B_GROUP_DIM = 128
C_HEADS = 4
C_QK_DIM = 64
C_V_DIM = 128
C_WIDTH = 512
D_WIDTH = 512
CONV_WIDTH = 3
REL_BUCKETS = 32
REL_MAX_DIST = 128
D_FF = 4096
EPS = 1e-6
EVEN_IN = 2 * A_WIDTH + B_WIDTH
ODD_IN = 2 * C_WIDTH + C_WIDTH + 3 * D_WIDTH

ROW_TILE = 1024
PROJ_SUBTILES = 4
PROJ_SLOTS = 3
MLP_ROW_TILE = 1024
MLP_SUBTILES = 4
FF_CHUNK = 1024
ATT_TILE = 256
ATT_TILES_PER_STEP = 4
FAR_UNROLL = 8
SCORE_LOOKAHEAD = 1
SUM_ROWS = 16
POOL_HALO = 16
CONV_HALO = 8
LOG2E = math.log2(math.e)
MASK_VALUE = -1e30
BF16_SUBLANES = 16
VMEM_LIMIT_BYTES = 52 * 1024 * 1024

_F32 = jnp.float32
_BF16 = jnp.bfloat16


def _rms(x, g):
    return x * lax.rsqrt(jnp.mean(x * x, axis=-1, keepdims=True) + EPS) * g


def _gelu_tanh(x):
    c = math.sqrt(2.0 / math.pi)
    hx = 0.5 * x
    return hx + hx * jnp.tanh(x * (c + (c * 0.044715) * (x * x)))


def _norm_project(x_ref, g_ref, w_ref, proj_ref):
    rows = x_ref.shape[0] // PROJ_SUBTILES
    slots = proj_ref.shape[0]
    ahead = slots - 1

    def project(r):
        hn = _rms(x_ref[r * rows:(r + 1) * rows, :], g_ref[...]).astype(_BF16)
        proj_ref[r % slots] = jnp.dot(hn, w_ref[...], preferred_element_type=_F32)

    for r in range(min(ahead, PROJ_SUBTILES)):
        project(r)
    for r in range(PROJ_SUBTILES):
        if r + ahead < PROJ_SUBTILES:
            project(r + ahead)
        yield r, proj_ref.at[r % slots]


def _const_spec(shape):
    nd = len(shape)
    return pl.BlockSpec(shape, lambda *_: (0,) * nd, pipeline_mode=pl.Buffered(1))


def _row_spec(width, col=0, tile=None):
    return pl.BlockSpec((None, tile or ROW_TILE, width), lambda b, j: (b, j, col))


def _cast_plan(weights, grid):
    steps = grid[0] * grid[1]
    in_specs, out_specs, out_shapes = [], [], []
    for w, layer in weights:
        _, rows, cols = w.shape
        slab = rows // steps
        assert slab * steps == rows and slab % BF16_SUBLANES == 0
        in_specs.append(pl.BlockSpec(
            (None, slab, cols), lambda b, j, layer=layer: (layer, b * grid[1] + j, 0)))
        out_specs.append(pl.BlockSpec((slab, cols), lambda b, j: (b * grid[1] + j, 0)))
        out_shapes.append(jax.ShapeDtypeStruct((rows, cols), _BF16))
    return in_specs, out_specs, out_shapes


def _cast_slabs(refs):
    n = len(refs) // 2
    for src, dst in zip(refs[:n], refs[n:]):
        dst[...] = src[...].astype(_BF16)


def _even_mixer_kernel(n_cast, x_ref, g_ref, w_in_ref, ln_g_ref, ln_b_ref, ws_ref, bs_ref,
                       pw_ref, ps_ref, *rest):
    cast_in, (ya_ref, yb_ref), rest = rest[:n_cast], rest[n_cast:n_cast + 2], rest[n_cast + 2:]
    cast_out, (carry_ref, proj_ref) = rest[:n_cast], rest[n_cast:]
    _cast_slabs(cast_in + cast_out)
    j = pl.program_id(1)
    rows = ROW_TILE // PROJ_SUBTILES

    @pl.when(j == 0)
    def _():
        carry_ref[...] = jnp.zeros_like(carry_ref)

    row = lax.broadcasted_iota(jnp.int32, (CHUNK, CHUNK), 0)
    col = lax.broadcasted_iota(jnp.int32, (CHUNK, CHUNK), 1)
    tril_w = [jnp.where(row >= col, ws_ref[g], 0.0).astype(_BF16) for g in range(A_GROUPS)]
    halo = carry_ref[...]
    for sub, proj in _norm_project(x_ref, g_ref, w_in_ref, proj_ref):
        base = sub * rows

        z = _gelu_tanh(proj[:, :2 * A_WIDTH])
        u = z[:, :A_WIDTH]
        v = z[:, A_WIDTH:]
        mu = jnp.mean(v, axis=-1, keepdims=True)
        vc = v - mu
        vn = vc * lax.rsqrt(jnp.mean(vc * vc, axis=-1, keepdims=True) + EPS)
        vn = (vn * ln_g_ref[...] + ln_b_ref[...]).astype(_BF16)
        for g in range(A_GROUPS):
            cs = slice(g * A_GROUP_DIM, (g + 1) * A_GROUP_DIM)
            bias = bs_ref[:, g:g + 1]
            for c in range(0, rows // CHUNK, 2):
                r0 = slice(c * CHUNK, (c + 1) * CHUNK)
                r1 = slice((c + 1) * CHUNK, (c + 2) * CHUNK)
                rhs = jnp.concatenate([vn[r0, cs], vn[r1, cs]], axis=1)
                mixed = jnp.dot(tril_w[g], rhs, preferred_element_type=_F32) + bias
                ya_ref[base + r0.start:base + r0.stop, cs] = (
                    u[r0, cs] * mixed[:, :A_GROUP_DIM]).astype(_BF16)
                ya_ref[base + r1.start:base + r1.stop, cs] = (
                    u[r1, cs] * mixed[:, A_GROUP_DIM:]).astype(_BF16)

        p = proj[:, 2 * A_WIDTH:]
        cur = jnp.concatenate([halo, p], axis=0)
        halo = p[rows - POOL_HALO:, :]
        cur_win = 1
        pos = j * ROW_TILE + base + lax.broadcasted_iota(jnp.int32, (rows, B_GROUP_DIM), 0)
        tokens_so_far = (pos + 1).astype(_F32)
        for g, win in enumerate(POOL_WINDOWS):
            cs = slice(g * B_GROUP_DIM, (g + 1) * B_GROUP_DIM)
            while cur_win < win:
                cur = cur + pltpu.roll(cur, cur_win, axis=0)
                cur_win *= 2
            assert cur_win == win and win <= POOL_HALO
            count = jnp.minimum(tokens_so_far, float(win))
            pooled = (cur[POOL_HALO:, :B_GROUP_DIM] / count - p[:, cs]).astype(_BF16)
            y = jnp.dot(pooled, pw_ref[g], preferred_element_type=_F32)
            yb_ref[base:base + rows, cs] = (y * ps_ref[:, cs]).astype(_BF16)
            cur = cur[:, B_GROUP_DIM:]
    carry_ref[...] = halo


def _even_mixer(h, g0, w_in, ln_g, ln_b, w_s, b_s_t, pool_w, pool_scale, casts):
    bsz, seq, _ = h.shape
    grid = (bsz, seq // ROW_TILE)
    out = jax.ShapeDtypeStruct((bsz, seq, A_WIDTH), _BF16)
    cast_in, cast_out, cast_shapes = _cast_plan(casts, grid)
    return pl.pallas_call(
        functools.partial(_even_mixer_kernel, len(casts)),
        grid=grid,
        in_specs=[
            _row_spec(D_MODEL),
            _const_spec((1, D_MODEL)),
            _const_spec((D_MODEL, EVEN_IN)),
            _const_spec((1, A_WIDTH)),
            _const_spec((1, A_WIDTH)),
            _const_spec((A_GROUPS, CHUNK, CHUNK)),
            _const_spec((CHUNK, A_GROUPS)),
            _const_spec((len(POOL_WINDOWS), B_GROUP_DIM, B_GROUP_DIM)),
            _const_spec((1, B_WIDTH)),
        ] + cast_in,
        out_specs=[_row_spec(A_WIDTH), _row_spec(B_WIDTH)] + cast_out,
        out_shape=[out, out] + cast_shapes,
        scratch_shapes=[
            pltpu.VMEM((POOL_HALO, B_WIDTH), _F32),
            pltpu.VMEM((PROJ_SLOTS, ROW_TILE // PROJ_SUBTILES, EVEN_IN), _F32),
        ],
        compiler_params=pltpu.CompilerParams(
            dimension_semantics=("arbitrary", "arbitrary"),
            vmem_limit_bytes=VMEM_LIMIT_BYTES),
        name="even_mixer",
    )(h, g0, w_in, ln_g, ln_b, w_s, b_s_t, pool_w, pool_scale, *[w for w, _ in casts])


def _odd_proj_kernel(x_ref, g_ref, w_in_ref, cw_ref, q_ref, k_ref, vt_ref, yd_ref,
                     carry_ref, proj_ref):
    j = pl.program_id(1)
    rows = ROW_TILE // PROJ_SUBTILES

    @pl.when(j == 0)
    def _():
        carry_ref[...] = jnp.zeros_like(carry_ref)

    halo = carry_ref[...]
    for sub, proj in _norm_project(x_ref, g_ref, w_in_ref, proj_ref):
        rs = slice(sub * rows, (sub + 1) * rows)
        q_ref[rs, :] = (proj[:, :C_WIDTH] * (C_QK_DIM ** -0.5 * LOG2E)).astype(_BF16)
        k_ref[rs, :] = proj[:, C_WIDTH:2 * C_WIDTH].astype(_BF16)
        for kt in range(rows // ATT_TILE):
            v_tile = proj[kt * ATT_TILE:(kt + 1) * ATT_TILE, 2 * C_WIDTH:3 * C_WIDTH]
            vt_ref[sub * (rows // ATT_TILE) + kt] = v_tile.T.astype(_BF16)
        o = 3 * C_WIDTH
        bg = proj[:, o:o + D_WIDTH]
        z = proj[:, o + D_WIDTH:o + 2 * D_WIDTH] * proj[:, o + 2 * D_WIDTH:]
        ext = jnp.concatenate([halo, z], axis=0)
        halo = z[rows - CONV_HALO:, :]
        y = cw_ref[CONV_WIDTH - 1:CONV_WIDTH, :] * z
        for t in range(CONV_WIDTH - 1):
            shift = CONV_WIDTH - 1 - t
            y = y + cw_ref[t:t + 1, :] * pltpu.roll(ext, shift, axis=0)[CONV_HALO:, :]
        yd_ref[rs, :] = (bg * y).astype(_BF16)
    carry_ref[...] = halo


def _odd_proj(h, g0, w_in, conv_w):
    bsz, seq, _ = h.shape
    grid = (bsz, seq // ROW_TILE)
    out = jax.ShapeDtypeStruct((bsz, seq, C_WIDTH), _BF16)
    assert (ROW_TILE // PROJ_SUBTILES) % ATT_TILE == 0
    out_t = jax.ShapeDtypeStruct((bsz, seq // ATT_TILE, C_WIDTH, ATT_TILE), _BF16)
    spec_t = pl.BlockSpec((None, ROW_TILE // ATT_TILE, C_WIDTH, ATT_TILE),
                          lambda b, j: (b, j, 0, 0))
    return pl.pallas_call(
        _odd_proj_kernel,
        grid=grid,
        in_specs=[
            _row_spec(D_MODEL),
            _const_spec((1, D_MODEL)),
            _const_spec((D_MODEL, ODD_IN)),
            _const_spec((CONV_WIDTH, D_WIDTH)),
        ],
        out_specs=[_row_spec(C_WIDTH), _row_spec(C_WIDTH), spec_t, _row_spec(D_WIDTH)],
        out_shape=[out, out, out_t, out],
        scratch_shapes=[
            pltpu.VMEM((CONV_HALO, D_WIDTH), _F32),
            pltpu.VMEM((PROJ_SLOTS, ROW_TILE // PROJ_SUBTILES, ODD_IN), _F32),
        ],
        compiler_params=pltpu.CompilerParams(
            dimension_semantics=("arbitrary", "arbitrary"),
            vmem_limit_bytes=VMEM_LIMIT_BYTES),
        name="odd_proj",
    )(h, g0, w_in, conv_w)


def _t5_bucket_upper_bounds(max_dist):
    d = np.arange(max_dist, dtype=np.int32)
    max_exact = REL_BUCKETS // 2
    nf = np.maximum(d, 1).astype(np.float32)
    large = max_exact + (np.log(nf / np.float32(max_exact))
                         / np.float32(math.log(REL_MAX_DIST / max_exact))
                         * np.float32(REL_BUCKETS - max_exact)).astype(np.int32)
    large = np.minimum(large, REL_BUCKETS - 1)
    bucket = np.where(d < max_exact, d, large)
    assert np.all(np.diff(bucket) >= 0)
    return bucket, {int(b): int(d[bucket == b].max()) for b in np.unique(bucket)}


def _attn_kernel(bucket_hi, lambda_init,
                 tab_ref, q_ref, k_ref, vt_ref, lam_ref, sg_ref, o_ref,
                 qq_ref, bias_ref, *scratch):
    t = ATT_TILE
    blk = pl.program_id(1)
    last_bucket = REL_BUCKETS - 1

    @pl.when(jnp.logical_and(pl.program_id(0) == 0, blk == 0))
    def _():
        kpos = lax.broadcasted_iota(jnp.int32, (t, t), 0)
        qpos = lax.broadcasted_iota(jnp.int32, (t, t), 1)
        for delta in range(2):
            d = qpos - kpos + delta * t
            for hd in range(C_HEADS):
                far = tab_ref[last_bucket, hd]
                val = jnp.zeros((t, t), _F32)
                for b in sorted(bucket_hi, reverse=True):
                    if b == last_bucket:
                        continue
                    val = jnp.where(d <= bucket_hi[b], (tab_ref[b, hd] - far) * LOG2E, val)
                val = jnp.where(d >= 0, val, MASK_VALUE)
                bias_ref[delta, hd] = jnp.concatenate([val, val], axis=1)

    def query_tile(sub, carry):
        rows = pl.ds(pl.multiple_of(sub * t, t), t)
        _attn_query_tile(lambda_init, blk * ATT_TILES_PER_STEP + sub,
                         q_ref.at[rows], k_ref, vt_ref, lam_ref, sg_ref, o_ref.at[sub],
                         qq_ref, bias_ref, *scratch)
        return carry

    lax.fori_loop(0, ATT_TILES_PER_STEP, query_tile, 0)


def _attn_query_tile(lambda_init, i, q_ref, k_ref, vt_ref, lam_ref, sg_ref, o_ref,
                     qq_ref, bias_ref, m_ref, acc_ref, s_ref, p_ref, al_ref):
    t = ATT_TILE
    hd_dim = 2 * C_QK_DIM
    heads = range(C_HEADS)

    feat = lax.broadcasted_iota(jnp.int32, (hd_dim, t), 0)
    for hd in heads:
        qt = q_ref[:, hd * hd_dim:(hd + 1) * hd_dim].astype(_F32).T
        qq_ref[hd] = jnp.concatenate(
            [jnp.where(feat < C_QK_DIM, qt, 0.0), jnp.where(feat >= C_QK_DIM, qt, 0.0)],
            axis=1).astype(_BF16)
    m_ref[...] = jnp.full_like(m_ref, MASK_VALUE)
    acc_ref[...] = jnp.zeros_like(acc_ref)
    p_ref[C_HEADS - 1] = jnp.zeros((t, 2 * t), _BF16)
    al_ref[C_HEADS - 1] = jnp.ones((1, 2 * t), _F32)

    def scores(jk, hd):
        start = pl.multiple_of(jk * t, t)
        k_t = k_ref[pl.ds(start, t), hd * hd_dim:(hd + 1) * hd_dim]
        s_ref[hd] = jnp.dot(k_t, qq_ref[hd], preferred_element_type=_F32)

    def values(jk, hd):
        vt = vt_ref[jk, hd * C_V_DIM:(hd + 1) * C_V_DIM, :]
        vt_ones = jnp.concatenate([vt, jnp.ones((SUM_ROWS, t), _BF16)], axis=0)
        acc_ref[hd] = al_ref[hd] * acc_ref[hd] + jnp.dot(
            vt_ones, p_ref[hd], preferred_element_type=_F32)

    def step(jk, delta, jk_next):
        for hd in heads:
            ahead = hd + SCORE_LOOKAHEAD
            if ahead < C_HEADS:
                scores(jk, ahead)
            elif jk_next is not None:
                scores(jk_next, ahead - C_HEADS)
            s = s_ref[hd]
            if delta is not None:
                s = s + bias_ref[delta, hd]
            m_prev = m_ref[hd]
            m_next = jnp.maximum(m_prev, jnp.max(s, axis=0, keepdims=True))
            alpha = jnp.exp2(m_prev - m_next)
            p = jnp.exp2(s - m_next)
            m_ref[hd] = m_next
            if hd == 0:
                values(jnp.maximum(jk - 1, 0), C_HEADS - 1)
            else:
                values(jk, hd - 1)
            p_ref[hd] = p.astype(_BF16)
            al_ref[hd] = alpha

    for hd in range(SCORE_LOOKAHEAD):
        scores(0, hd)

    n_far = jnp.maximum(i - 1, 0)

    def far_run(first, count):
        for n in range(count):
            step(first + n, None, first + n + 1)

    def far_trip(r, carry):
        far_run(FAR_UNROLL * r, FAR_UNROLL)
        return carry

    trips = n_far // FAR_UNROLL
    lax.fori_loop(0, trips, far_trip, 0)
    done = FAR_UNROLL * trips
    run = FAR_UNROLL // 2
    while run >= 2:
        take = (n_far - done) >= run
        pl.when(take)(functools.partial(far_run, done, run))
        done = done + jnp.where(take, run, 0)
        run //= 2

    def finish(odd_far_tile, sub_diagonal):
        if odd_far_tile:
            step(n_far - 1, None, n_far)
        if sub_diagonal:
            step(i - 1, 1, i)
        step(i, 0, None)
        values(i, C_HEADS - 1)
        lp = lam_ref[...]
        lam = (jnp.exp(jnp.sum(lp[0:1] * lp[1:2], axis=-1, keepdims=True))
               - jnp.exp(jnp.sum(lp[2:3] * lp[3:4], axis=-1, keepdims=True)) + lambda_init)
        for hd in heads:
            inv_l = 1.0 / acc_ref[hd, C_V_DIM:C_V_DIM + 1, :]
            a = (acc_ref[hd, :C_V_DIM, :t] * inv_l[:, :t]
                 - acc_ref[hd, :C_V_DIM, t:] * (lam * inv_l[:, t:]))
            ms = jnp.mean(a * a, axis=0, keepdims=True)
            y = a * (lax.rsqrt(ms + EPS) * (1.0 - lambda_init)) * sg_ref[...]
            o_ref[hd * C_V_DIM:(hd + 1) * C_V_DIM, :] = y.astype(_BF16)

    odd_far = n_far % 2 == 1
    pl.when(i == 0)(lambda: finish(False, False))
    pl.when(jnp.logical_and(i >= 1, jnp.logical_not(odd_far)))(lambda: finish(False, True))
    pl.when(odd_far)(lambda: finish(True, True))


def _diff_attention(q, k, vt, rel_table, lam_params, subln_g, lambda_init):
    bsz, seq, _ = q.shape
    t = ATT_TILE
    bucket, bucket_hi = _t5_bucket_upper_bounds(seq)
    assert np.all(bucket[t + 1:] == REL_BUCKETS - 1)
    rows = t * ATT_TILES_PER_STEP
    grid = (bsz, seq // rows)
    kernel = functools.partial(_attn_kernel, bucket_hi, lambda_init)
    return pl.pallas_call(
        kernel,
        grid=grid,
        in_specs=[
            pl.BlockSpec(memory_space=pltpu.SMEM),
            pl.BlockSpec((None, rows, C_WIDTH), lambda b, i: (b, i, 0)),
            pl.BlockSpec((None, seq, C_WIDTH), lambda b, i: (b, 0, 0)),
            pl.BlockSpec((None, seq // t, C_WIDTH, t), lambda b, i: (b, 0, 0, 0)),
            pl.BlockSpec((4, C_QK_DIM), lambda b, i: (0, 0)),
            pl.BlockSpec((C_V_DIM, 1), lambda b, i: (0, 0)),
        ],
        out_specs=pl.BlockSpec((None, ATT_TILES_PER_STEP, C_WIDTH, t), lambda b, i: (b, i, 0, 0)),
        out_shape=jax.ShapeDtypeStruct((bsz, seq // t, C_WIDTH, t), _BF16),
        scratch_shapes=[
            pltpu.VMEM((C_HEADS, 2 * C_QK_DIM, 2 * t), _BF16),
            pltpu.VMEM((2, C_HEADS, t, 2 * t), _F32),
            pltpu.VMEM((C_HEADS, 1, 2 * t), _F32),
            pltpu.VMEM((C_HEADS, C_V_DIM + SUM_ROWS, 2 * t), _F32),
            pltpu.VMEM((C_HEADS, t, 2 * t), _F32),
            pltpu.VMEM((C_HEADS, t, 2 * t), _BF16),
            pltpu.VMEM((C_HEADS, 1, 2 * t), _F32),
        ],
        compiler_params=pltpu.CompilerParams(
            dimension_semantics=("arbitrary", "arbitrary"),
            vmem_limit_bytes=VMEM_LIMIT_BYTES),
        name="diff_attention",
    )(rel_table, q, k, vt, lam_params, subln_g)


def _out_mlp_kernel(n_cast, ya_tiled, h_ref, ya_ref, yb_ref, g_ref, w_out_ref, w1_ref, w2_ref,
                    *rest):
    cast_in, o_ref, cast_out = rest[:n_cast], rest[n_cast], rest[n_cast + 1:]
    _cast_slabs(cast_in + cast_out)
    half = D_MODEL // 2
    sub = MLP_ROW_TILE // MLP_SUBTILES
    rows = [slice(r * sub, (r + 1) * sub) for r in range(MLP_SUBTILES)]

    def out_proj_a(r):
        if not ya_tiled:
            return jnp.dot(ya_ref[rows[r], :], w_out_ref[:half, :], preferred_element_type=_F32)
        per = sub // ATT_TILE
        parts = [lax.dot_general(ya_ref[r * per + kt], w_out_ref[:half, :],
                                 (((0,), (0,)), ((), ())), preferred_element_type=_F32)
                 for kt in range(per)]
        return parts[0] if per == 1 else jnp.concatenate(parts, axis=0)

    ys = [out_proj_a(r)
          + jnp.dot(yb_ref[rs, :], w_out_ref[half:, :], preferred_element_type=_F32)
          for r, rs in enumerate(rows)]
    h1s = [h_ref[rs, :] + _rms(y, g_ref[1:2, :]) for rs, y in zip(rows, ys)]
    hns = [_rms(h1, g_ref[2:3, :]).astype(_BF16) for h1 in h1s]
    for rs, h1, hn in zip(rows, h1s, hns):
        acc = jnp.zeros((sub, D_MODEL), _F32)
        for c in range(D_FF // FF_CHUNK):
            cs = slice(c * FF_CHUNK, (c + 1) * FF_CHUNK)
            a = jnp.dot(hn, w1_ref[:, cs], preferred_element_type=_F32)
            a = jnp.square(jnp.maximum(a, 0.0)).astype(_BF16)
            acc = acc + jnp.dot(a, w2_ref[cs, :], preferred_element_type=_F32)
        o_ref[rs, :] = h1 + _rms(acc, g_ref[3:4, :])


def _out_mlp(h, ya, yb, g, w_out, w1, w2, casts=()):
    bsz, seq, _ = h.shape
    grid = (bsz, seq // MLP_ROW_TILE)
    row_spec = functools.partial(_row_spec, tile=MLP_ROW_TILE)
    cast_in, cast_out, cast_shapes = _cast_plan(casts, grid)
    ya_tiled = ya.ndim == 4
    if ya_tiled:
        assert (MLP_ROW_TILE // MLP_SUBTILES) % ATT_TILE == 0
        ya_spec = pl.BlockSpec((None, MLP_ROW_TILE // ATT_TILE, D_MODEL // 2, ATT_TILE),
                               lambda b, j: (b, j, 0, 0))
    else:
        ya_spec = row_spec(D_MODEL // 2)
    return pl.pallas_call(
        functools.partial(_out_mlp_kernel, len(casts), ya_tiled),
        grid=grid,
        in_specs=[
            row_spec(D_MODEL),
            ya_spec,
            row_spec(D_MODEL // 2),
            _const_spec((4, D_MODEL)),
            _const_spec((D_MODEL, D_MODEL)),
            _const_spec((D_MODEL, D_FF)),
            _const_spec((D_FF, D_MODEL)),
        ] + cast_in,
        out_specs=[row_spec(D_MODEL)] + cast_out,
        out_shape=[jax.ShapeDtypeStruct(h.shape, h.dtype)] + cast_shapes,
        compiler_params=pltpu.CompilerParams(
            dimension_semantics=("arbitrary", "arbitrary"),
            vmem_limit_bytes=VMEM_LIMIT_BYTES),
        name="out_mlp",
    )(h, ya, yb, g, w_out, w1, w2, *[w for w, _ in casts])


def kernel(x, rel_bias_table, norm_g, even_w_in, even_ln_g, even_ln_b, even_spatial_w,
           even_spatial_b, even_pool_w, even_pool_scale, even_w_out, odd_w_in, odd_lambda,
           odd_subln_g, odd_conv_w, odd_w_out, ffn_w1, ffn_w2):
    depth = norm_g.shape[0]
    bf = lambda w: w.astype(_BF16)

    def layer_weights(layer):
        mixer = (even_w_in, even_w_out) if layer % 2 == 0 else (odd_w_in, odd_w_out)
        return [(mixer[0], layer // 2), (mixer[1], layer // 2),
                (ffn_w1, layer), (ffn_w2, layer)]

    first = layer_weights(0)
    w_in = bf(first[0][0][first[0][1]])
    w_out = w1 = w2 = None
    h = x
    for layer in range(depth):
        g = norm_g[layer]
        own_casts = [] if w_out is not None else layer_weights(layer)[1:]
        if layer % 2 == 0:
            e = layer // 2
            ya, yb, *cast = _even_mixer(
                h, g[0:1], w_in, even_ln_g[e][None], even_ln_b[e][None],
                even_spatial_w[e], even_spatial_b[e].T, bf(even_pool_w[e]),
                even_pool_scale[e][None], own_casts)
        else:
            o = layer // 2
            lambda_init = 0.8 - 0.6 * math.exp(-0.3 * layer)
            q, k, vt, yb = _odd_proj(h, g[0:1], w_in, odd_conv_w[o])
            ya = _diff_attention(q, k, vt, rel_bias_table, odd_lambda[o],
                                 odd_subln_g[o][:, None], lambda_init)
            cast = [bf(w[i]) for w, i in own_casts]
        if own_casts:
            w_out, w1, w2 = cast
        next_casts = layer_weights(layer + 1) if layer + 1 < depth else []
        h, *nxt = _out_mlp(h, ya, yb, g, w_out, w1, w2, next_casts)
        w_in, w_out, w1, w2 = nxt if nxt else (None,) * 4
    return h
```

```python
import functools
import math

import numpy as np
import jax
import jax.numpy as jnp
from jax import lax
from jax.experimental import pallas as pl
from jax.experimental.pallas import tpu as pltpu

D_MODEL = 1024
A_WIDTH = 512
A_GROUPS = 4
A_GROUP_DIM = 128
CHUNK = 128
B_WIDTH = 512
POOL_WINDOWS = (2, 4, 8, 16)
B_GROUP_DIM = 128
C_HEADS = 4
C_QK_DIM = 64
C_V_DIM = 128
C_WIDTH = 512
D_WIDTH = 512
CONV_WIDTH = 3
REL_BUCKETS = 32
REL_MAX_DIST = 128
D_FF = 4096
EPS = 1e-6
EVEN_IN = 2 * A_WIDTH + B_WIDTH
ODD_IN = 2 * C_WIDTH + C_WIDTH + 3 * D_WIDTH

ROW_TILE = 1024
PROJ_SUBTILES = 4
PROJ_SLOTS = 3
MLP_ROW_TILE = 1024
MLP_SUBTILES = 4
FF_CHUNK = 1024
ATT_TILE = 256
ATT_TILES_PER_STEP = 4
FAR_UNROLL = 8
SCORE_LOOKAHEAD = 1
SUM_ROWS = 16
POOL_HALO = 16
CONV_HALO = 8
LOG2E = math.log2(math.e)
MASK_VALUE = -1e30
BF16_SUBLANES = 16
VMEM_LIMIT_BYTES = 52 * 1024 * 1024

_F32 = jnp.float32
_BF16 = jnp.bfloat16


def _rms(x, g):
    return x * lax.rsqrt(jnp.mean(x * x, axis=-1, keepdims=True) + EPS) * g


def _gelu_tanh(x):
    c = math.sqrt(2.0 / math.pi)
    hx = 0.5 * x
    return hx + hx * jnp.tanh(x * (c + (c * 0.044715) * (x * x)))


def _norm_project(x_ref, g_ref, w_ref, proj_ref):
    rows = x_ref.shape[0] // PROJ_SUBTILES
    slots = proj_ref.shape[0]
    ahead = slots - 1

    def project(r):
        hn = _rms(x_ref[r * rows:(r + 1) * rows, :], g_ref[...]).astype(_BF16)
        proj_ref[r % slots] = jnp.dot(hn, w_ref[...], preferred_element_type=_F32)

    for r in range(min(ahead, PROJ_SUBTILES)):
        project(r)
    for r in range(PROJ_SUBTILES):
        if r + ahead < PROJ_SUBTILES:
            project(r + ahead)
        yield r, proj_ref.at[r % slots]


def _const_spec(shape):
    nd = len(shape)
    return pl.BlockSpec(shape, lambda *_: (0,) * nd, pipeline_mode=pl.Buffered(1))


def _row_spec(width, col=0, tile=None):
    return pl.BlockSpec((None, tile or ROW_TILE, width), lambda b, j: (b, j, col))


def _cast_plan(weights, grid):
    steps = grid[0] * grid[1]
    in_specs, out_specs, out_shapes = [], [], []
    for w, layer in weights:
        _, rows, cols = w.shape
        slab = rows // steps
        assert slab * steps == rows and slab % BF16_SUBLANES == 0
        in_specs.append(pl.BlockSpec(
            (None, slab, cols), lambda b, j, layer=layer: (layer, b * grid[1] + j, 0)))
        out_specs.append(pl.BlockSpec((slab, cols), lambda b, j: (b * grid[1] + j, 0)))
        out_shapes.append(jax.ShapeDtypeStruct((rows, cols), _BF16))
    return in_specs, out_specs, out_shapes


def _cast_slabs(refs):
    n = len(refs) // 2
    for src, dst in zip(refs[:n], refs[n:]):
        dst[...] = src[...].astype(_BF16)


def _even_mixer_kernel(n_cast, x_ref, g_ref, w_in_ref, ln_g_ref, ln_b_ref, ws_ref, bs_ref,
                       pw_ref, ps_ref, *rest):
    cast_in, (ya_ref, yb_ref), rest = rest[:n_cast], rest[n_cast:n_cast + 2], rest[n_cast + 2:]
    cast_out, (carry_ref, proj_ref) = rest[:n_cast], rest[n_cast:]
    _cast_slabs(cast_in + cast_out)
    j = pl.program_id(1)
    rows = ROW_TILE // PROJ_SUBTILES

    @pl.when(j == 0)
    def _():
        carry_ref[...] = jnp.zeros_like(carry_ref)

    row = lax.broadcasted_iota(jnp.int32, (CHUNK, CHUNK), 0)
    col = lax.broadcasted_iota(jnp.int32, (CHUNK, CHUNK), 1)
    tril_w = [jnp.where(row >= col, ws_ref[g], 0.0).astype(_BF16) for g in range(A_GROUPS)]
    halo = carry_ref[...]
    for sub, proj in _norm_project(x_ref, g_ref, w_in_ref, proj_ref):
        base = sub * rows

        z = _gelu_tanh(proj[:, :2 * A_WIDTH])
        u = z[:, :A_WIDTH]
        v = z[:, A_WIDTH:]
        mu = jnp.mean(v, axis=-1, keepdims=True)
        vc = v - mu
        vn = vc * lax.rsqrt(jnp.mean(vc * vc, axis=-1, keepdims=True) + EPS)
        vn = (vn * ln_g_ref[...] + ln_b_ref[...]).astype(_BF16)
        for g in range(A_GROUPS):
            cs = slice(g * A_GROUP_DIM, (g + 1) * A_GROUP_DIM)
            bias = bs_ref[:, g:g + 1]
            for c in range(0, rows // CHUNK, 2):
                r0 = slice(c * CHUNK, (c + 1) * CHUNK)
                r1 = slice((c + 1) * CHUNK, (c + 2) * CHUNK)
                rhs = jnp.concatenate([vn[r0, cs], vn[r1, cs]], axis=1)
                mixed = jnp.dot(tril_w[g], rhs, preferred_element_type=_F32) + bias
                ya_ref[base + r0.start:base + r0.stop, cs] = (
                    u[r0, cs] * mixed[:, :A_GROUP_DIM]).astype(_BF16)
                ya_ref[base + r1.start:base + r1.stop, cs] = (
                    u[r1, cs] * mixed[:, A_GROUP_DIM:]).astype(_BF16)

        p = proj[:, 2 * A_WIDTH:]
        cur = jnp.concatenate([halo, p], axis=0)
        halo = p[rows - POOL_HALO:, :]
        cur_win = 1
        pos = j * ROW_TILE + base + lax.broadcasted_iota(jnp.int32, (rows, B_GROUP_DIM), 0)
        tokens_so_far = (pos + 1).astype(_F32)
        for g, win in enumerate(POOL_WINDOWS):
            cs = slice(g * B_GROUP_DIM, (g + 1) * B_GROUP_DIM)
            while cur_win < win:
                cur = cur + pltpu.roll(cur, cur_win, axis=0)
                cur_win *= 2
            assert cur_win == win and win <= POOL_HALO
            count = jnp.minimum(tokens_so_far, float(win))
            pooled = (cur[POOL_HALO:, :B_GROUP_DIM] / count - p[:, cs]).astype(_BF16)
            y = jnp.dot(pooled, pw_ref[g], preferred_element_type=_F32)
            yb_ref[base:base + rows, cs] = (y * ps_ref[:, cs]).astype(_BF16)
            cur = cur[:, B_GROUP_DIM:]
    carry_ref[...] = halo


def _even_mixer(h, g0, w_in, ln_g, ln_b, w_s, b_s_t, pool_w, pool_scale, casts):
    bsz, seq, _ = h.shape
    grid = (bsz, seq // ROW_TILE)
    out = jax.ShapeDtypeStruct((bsz, seq, A_WIDTH), _BF16)
    cast_in, cast_out, cast_shapes = _cast_plan(casts, grid)
    return pl.pallas_call(
        functools.partial(_even_mixer_kernel, len(casts)),
        grid=grid,
        in_specs=[
            _row_spec(D_MODEL),
            _const_spec((1, D_MODEL)),
            _const_spec((D_MODEL, EVEN_IN)),
            _const_spec((1, A_WIDTH)),
            _const_spec((1, A_WIDTH)),
            _const_spec((A_GROUPS, CHUNK, CHUNK)),
            _const_spec((CHUNK, A_GROUPS)),
            _const_spec((len(POOL_WINDOWS), B_GROUP_DIM, B_GROUP_DIM)),
            _const_spec((1, B_WIDTH)),
        ] + cast_in,
        out_specs=[_row_spec(A_WIDTH), _row_spec(B_WIDTH)] + cast_out,
        out_shape=[out, out] + cast_shapes,
        scratch_shapes=[
            pltpu.VMEM((POOL_HALO, B_WIDTH), _F32),
            pltpu.VMEM((PROJ_SLOTS, ROW_TILE // PROJ_SUBTILES, EVEN_IN), _F32),
        ],
        compiler_params=pltpu.CompilerParams(
            dimension_semantics=("arbitrary", "arbitrary"),
            vmem_limit_bytes=VMEM_LIMIT_BYTES),
        name="even_mixer",
    )(h, g0, w_in, ln_g, ln_b, w_s, b_s_t, pool_w, pool_scale, *[w for w, _ in casts])


def _odd_proj_kernel(x_ref, g_ref, w_in_ref, cw_ref, q_ref, k_ref, vt_ref, yd_ref,
                     carry_ref, proj_ref):
    j = pl.program_id(1)
    rows = ROW_TILE // PROJ_SUBTILES

    @pl.when(j == 0)
    def _():
        carry_ref[...] = jnp.zeros_like(carry_ref)

    halo = carry_ref[...]
    for sub, proj in _norm_project(x_ref, g_ref, w_in_ref, proj_ref):
        rs = slice(sub * rows, (sub + 1) * rows)
        q_ref[rs, :] = (proj[:, :C_WIDTH] * (C_QK_DIM ** -0.5 * LOG2E)).astype(_BF16)
        k_ref[rs, :] = proj[:, C_WIDTH:2 * C_WIDTH].astype(_BF16)
        for kt in range(rows // ATT_TILE):
            v_tile = proj[kt * ATT_TILE:(kt + 1) * ATT_TILE, 2 * C_WIDTH:3 * C_WIDTH]
            vt_ref[sub * (rows // ATT_TILE) + kt] = v_tile.T.astype(_BF16)
        o = 3 * C_WIDTH
        bg = proj[:, o:o + D_WIDTH]
        z = proj[:, o + D_WIDTH:o + 2 * D_WIDTH] * proj[:, o + 2 * D_WIDTH:]
        ext = jnp.concatenate([halo, z], axis=0)
        halo = z[rows - CONV_HALO:, :]
        y = cw_ref[CONV_WIDTH - 1:CONV_WIDTH, :] * z
        for t in range(CONV_WIDTH - 1):
            shift = CONV_WIDTH - 1 - t
            y = y + cw_ref[t:t + 1, :] * pltpu.roll(ext, shift, axis=0)[CONV_HALO:, :]
        yd_ref[rs, :] = (bg * y).astype(_BF16)
    carry_ref[...] = halo


def _odd_proj(h, g0, w_in, conv_w):
    bsz, seq, _ = h.shape
    grid = (bsz, seq // ROW_TILE)
    out = jax.ShapeDtypeStruct((bsz, seq, C_WIDTH), _BF16)
    assert (ROW_TILE // PROJ_SUBTILES) % ATT_TILE == 0
    out_t = jax.ShapeDtypeStruct((bsz, seq // ATT_TILE, C_WIDTH, ATT_TILE), _BF16)
    spec_t = pl.BlockSpec((None, ROW_TILE // ATT_TILE, C_WIDTH, ATT_TILE),
                          lambda b, j: (b, j, 0, 0))
    return pl.pallas_call(
        _odd_proj_kernel,
        grid=grid,
        in_specs=[
            _row_spec(D_MODEL),
            _const_spec((1, D_MODEL)),
            _const_spec((D_MODEL, ODD_IN)),
            _const_spec((CONV_WIDTH, D_WIDTH)),
        ],
        out_specs=[_row_spec(C_WIDTH), _row_spec(C_WIDTH), spec_t, _row_spec(D_WIDTH)],
        out_shape=[out, out, out_t, out],
        scratch_shapes=[
            pltpu.VMEM((CONV_HALO, D_WIDTH), _F32),
            pltpu.VMEM((PROJ_SLOTS, ROW_TILE // PROJ_SUBTILES, ODD_IN), _F32),
        ],
        compiler_params=pltpu.CompilerParams(
            dimension_semantics=("arbitrary", "arbitrary"),
            vmem_limit_bytes=VMEM_LIMIT_BYTES),
        name="odd_proj",
    )(h, g0, w_in, conv_w)


def _t5_bucket_upper_bounds(max_dist):
    d = np.arange(max_dist, dtype=np.int32)
    max_exact = REL_BUCKETS // 2
    nf = np.maximum(d, 1).astype(np.float32)
    large = max_exact + (np.log(nf / np.float32(max_exact))
                         / np.float32(math.log(REL_MAX_DIST / max_exact))
                         * np.float32(REL_BUCKETS - max_exact)).astype(np.int32)
    large = np.minimum(large, REL_BUCKETS - 1)
    bucket = np.where(d < max_exact, d, large)
    assert np.all(np.diff(bucket) >= 0)
    return bucket, {int(b): int(d[bucket == b].max()) for b in np.unique(bucket)}


def _attn_kernel(bucket_hi, lambda_init,
                 tab_ref, q_ref, k_ref, vt_ref, lam_ref, sg_ref, o_ref,
                 qq_ref, bias_ref, *scratch):
    t = ATT_TILE
    blk = pl.program_id(1)
    last_bucket = REL_BUCKETS - 1

    @pl.when(jnp.logical_and(pl.program_id(0) == 0, blk == 0))
    def _():
        kpos = lax.broadcasted_iota(jnp.int32, (t, t), 0)
        qpos = lax.broadcasted_iota(jnp.int32, (t, t), 1)
        for delta in range(2):
            d = qpos - kpos + delta * t
            for hd in range(C_HEADS):
                far = tab_ref[last_bucket, hd]
                val = jnp.zeros((t, t), _F32)
                for b in sorted(bucket_hi, reverse=True):
                    if b == last_bucket:
                        continue
                    val = jnp.where(d <= bucket_hi[b], (tab_ref[b, hd] - far) * LOG2E, val)
                val = jnp.where(d >= 0, val, MASK_VALUE)
                bias_ref[delta, hd] = jnp.concatenate([val, val], axis=1)

    def query_tile(sub, carry):
        rows = pl.ds(pl.multiple_of(sub * t, t), t)
        _attn_query_tile(lambda_init, blk * ATT_TILES_PER_STEP + sub,
                         q_ref.at[rows], k_ref, vt_ref, lam_ref, sg_ref, o_ref.at[sub],
                         qq_ref, bias_ref, *scratch)
        return carry

    lax.fori_loop(0, ATT_TILES_PER_STEP, query_tile, 0)


def _attn_query_tile(lambda_init, i, q_ref, k_ref, vt_ref, lam_ref, sg_ref, o_ref,
                     qq_ref, bias_ref, m_ref, acc_ref, s_ref, p_ref, al_ref):
    t = ATT_TILE
    hd_dim = 2 * C_QK_DIM
    heads = range(C_HEADS)

    feat = lax.broadcasted_iota(jnp.int32, (hd_dim, t), 0)
    for hd in heads:
        qt = q_ref[:, hd * hd_dim:(hd + 1) * hd_dim].astype(_F32).T
        qq_ref[hd] = jnp.concatenate(
            [jnp.where(feat < C_QK_DIM, qt, 0.0), jnp.where(feat >= C_QK_DIM, qt, 0.0)],
            axis=1).astype(_BF16)
    m_ref[...] = jnp.full_like(m_ref, MASK_VALUE)
    acc_ref[...] = jnp.zeros_like(acc_ref)
    p_ref[C_HEADS - 1] = jnp.zeros((t, 2 * t), _BF16)
    al_ref[C_HEADS - 1] = jnp.ones((1, 2 * t), _F32)

    def scores(jk, hd):
        start = pl.multiple_of(jk * t, t)
        k_t = k_ref[pl.ds(start, t), hd * hd_dim:(hd + 1) * hd_dim]
        s_ref[hd] = jnp.dot(k_t, qq_ref[hd], preferred_element_type=_F32)

    def values(jk, hd):
        vt = vt_ref[jk, hd * C_V_DIM:(hd + 1) * C_V_DIM, :]
        vt_ones = jnp.concatenate([vt, jnp.ones((SUM_ROWS, t), _BF16)], axis=0)
        acc_ref[hd] = al_ref[hd] * acc_ref[hd] + jnp.dot(
            vt_ones, p_ref[hd], preferred_element_type=_F32)

    def step(jk, delta, jk_next):
        for hd in heads:
            ahead = hd + SCORE_LOOKAHEAD
            if ahead < C_HEADS:
                scores(jk, ahead)
            elif jk_next is not None:
                scores(jk_next, ahead - C_HEADS)
            s = s_ref[hd]
            if delta is not None:
                s = s + bias_ref[delta, hd]
            m_prev = m_ref[hd]
            m_next = jnp.maximum(m_prev, jnp.max(s, axis=0, keepdims=True))
            alpha = jnp.exp2(m_prev - m_next)
            p = jnp.exp2(s - m_next)
            m_ref[hd] = m_next
            if hd == 0:
                values(jnp.maximum(jk - 1, 0), C_HEADS - 1)
            else:
                values(jk, hd - 1)
            p_ref[hd] = p.astype(_BF16)
            al_ref[hd] = alpha

    for hd in range(SCORE_LOOKAHEAD):
        scores(0, hd)

    n_far = jnp.maximum(i - 1, 0)

    def far_run(first, count):
        for n in range(count):
            step(first + n, None, first + n + 1)

    def far_trip(r, carry):
        far_run(FAR_UNROLL * r, FAR_UNROLL)
        return carry

    trips = n_far // FAR_UNROLL
    lax.fori_loop(0, trips, far_trip, 0)
    done = FAR_UNROLL * trips
    run = FAR_UNROLL // 2
    while run >= 2:
        take = (n_far - done) >= run
        pl.when(take)(functools.partial(far_run, done, run))
        done = done + jnp.where(take, run, 0)
        run //= 2

    def finish(odd_far_tile, sub_diagonal):
        if odd_far_tile:
            step(n_far - 1, None, n_far)
        if sub_diagonal:
            step(i - 1, 1, i)
        step(i, 0, None)
        values(i, C_HEADS - 1)
        lp = lam_ref[...]
        lam = (jnp.exp(jnp.sum(lp[0:1] * lp[1:2], axis=-1, keepdims=True))
               - jnp.exp(jnp.sum(lp[2:3] * lp[3:4], axis=-1, keepdims=True)) + lambda_init)
        for hd in heads:
            inv_l = 1.0 / acc_ref[hd, C_V_DIM:C_V_DIM + 1, :]
            a = (acc_ref[hd, :C_V_DIM, :t] * inv_l[:, :t]
                 - acc_ref[hd, :C_V_DIM, t:] * (lam * inv_l[:, t:]))
            ms = jnp.mean(a * a, axis=0, keepdims=True)
            y = a * (lax.rsqrt(ms + EPS) * (1.0 - lambda_init)) * sg_ref[...]
            o_ref[hd * C_V_DIM:(hd + 1) * C_V_DIM, :] = y.astype(_BF16)

    odd_far = n_far % 2 == 1
    pl.when(i == 0)(lambda: finish(False, False))
    pl.when(jnp.logical_and(i >= 1, jnp.logical_not(odd_far)))(lambda: finish(False, True))
    pl.when(odd_far)(lambda: finish(True, True))


def _diff_attention(q, k, vt, rel_table, lam_params, subln_g, lambda_init):
    bsz, seq, _ = q.shape
    t = ATT_TILE
    bucket, bucket_hi = _t5_bucket_upper_bounds(seq)
    assert np.all(bucket[t + 1:] == REL_BUCKETS - 1)
    rows = t * ATT_TILES_PER_STEP
    grid = (bsz, seq // rows)
    kernel = functools.partial(_attn_kernel, bucket_hi, lambda_init)
    return pl.pallas_call(
        kernel,
        grid=grid,
        in_specs=[
            pl.BlockSpec(memory_space=pltpu.SMEM),
            pl.BlockSpec((None, rows, C_WIDTH), lambda b, i: (b, i, 0)),
            pl.BlockSpec((None, seq, C_WIDTH), lambda b, i: (b, 0, 0)),
            pl.BlockSpec((None, seq // t, C_WIDTH, t), lambda b, i: (b, 0, 0, 0)),
            pl.BlockSpec((4, C_QK_DIM), lambda b, i: (0, 0)),
            pl.BlockSpec((C_V_DIM, 1), lambda b, i: (0, 0)),
        ],
        out_specs=pl.BlockSpec((None, ATT_TILES_PER_STEP, C_WIDTH, t), lambda b, i: (b, i, 0, 0)),
        out_shape=jax.ShapeDtypeStruct((bsz, seq // t, C_WIDTH, t), _BF16),
        scratch_shapes=[
            pltpu.VMEM((C_HEADS, 2 * C_QK_DIM, 2 * t), _BF16),
            pltpu.VMEM((2, C_HEADS, t, 2 * t), _F32),
            pltpu.VMEM((C_HEADS, 1, 2 * t), _F32),
            pltpu.VMEM((C_HEADS, C_V_DIM + SUM_ROWS, 2 * t), _F32),
            pltpu.VMEM((C_HEADS, t, 2 * t), _F32),
            pltpu.VMEM((C_HEADS, t, 2 * t), _BF16),
            pltpu.VMEM((C_HEADS, 1, 2 * t), _F32),
        ],
        compiler_params=pltpu.CompilerParams(
            dimension_semantics=("arbitrary", "arbitrary"),
            vmem_limit_bytes=VMEM_LIMIT_BYTES),
        name="diff_attention",
    )(rel_table, q, k, vt, lam_params, subln_g)


def _out_mlp_kernel(n_cast, ya_tiled, h_ref, ya_ref, yb_ref, g_ref, w_out_ref, w1_ref, w2_ref,
                    *rest):
    cast_in, o_ref, cast_out = rest[:n_cast], rest[n_cast], rest[n_cast + 1:]
    _cast_slabs(cast_in + cast_out)
    half = D_MODEL // 2
    sub = MLP_ROW_TILE // MLP_SUBTILES
    rows = [slice(r * sub, (r + 1) * sub) for r in range(MLP_SUBTILES)]

    def out_proj_a(r):
        if not ya_tiled:
            return jnp.dot(ya_ref[rows[r], :], w_out_ref[:half, :], preferred_element_type=_F32)
        per = sub // ATT_TILE
        parts = [lax.dot_general(ya_ref[r * per + kt], w_out_ref[:half, :],
                                 (((0,), (0,)), ((), ())), preferred_element_type=_F32)
                 for kt in range(per)]
        return parts[0] if per == 1 else jnp.concatenate(parts, axis=0)

    def mixed_residual(r):
        y = out_proj_a(r) + jnp.dot(yb_ref[rows[r], :], w_out_ref[half:, :],
                                    preferred_element_type=_F32)
        h1 = h_ref[rows[r], :] + _rms(y, g_ref[1:2, :])
        return h1, _rms(h1, g_ref[2:3, :]).astype(_BF16)

    ahead = mixed_residual(0)
    for r, rs in enumerate(rows):
        h1, hn = ahead
        if r + 1 < MLP_SUBTILES:
            ahead = mixed_residual(r + 1)
        acc = jnp.zeros((sub, D_MODEL), _F32)
        for c in range(D_FF // FF_CHUNK):
            cs = slice(c * FF_CHUNK, (c + 1) * FF_CHUNK)
            a = jnp.dot(hn, w1_ref[:, cs], preferred_element_type=_F32)
            a = jnp.square(jnp.maximum(a, 0.0)).astype(_BF16)
            acc = acc + jnp.dot(a, w2_ref[cs, :], preferred_element_type=_F32)
        o_ref[rs, :] = h1 + _rms(acc, g_ref[3:4, :])


def _out_mlp(h, ya, yb, g, w_out, w1, w2, casts=()):
    bsz, seq, _ = h.shape
    grid = (bsz, seq // MLP_ROW_TILE)
    row_spec = functools.partial(_row_spec, tile=MLP_ROW_TILE)
    cast_in, cast_out, cast_shapes = _cast_plan(casts, grid)
    ya_tiled = ya.ndim == 4
    if ya_tiled:
        assert (MLP_ROW_TILE // MLP_SUBTILES) % ATT_TILE == 0
        ya_spec = pl.BlockSpec((None, MLP_ROW_TILE // ATT_TILE, D_MODEL // 2, ATT_TILE),
                               lambda b, j: (b, j, 0, 0))
    else:
        ya_spec = row_spec(D_MODEL // 2)
    return pl.pallas_call(
        functools.partial(_out_mlp_kernel, len(casts), ya_tiled),
        grid=grid,
        in_specs=[
            row_spec(D_MODEL),
            ya_spec,
            row_spec(D_MODEL // 2),
            _const_spec((4, D_MODEL)),
            _const_spec((D_MODEL, D_MODEL)),
            _const_spec((D_MODEL, D_FF)),
            _const_spec((D_FF, D_MODEL)),
        ] + cast_in,
        out_specs=[row_spec(D_MODEL)] + cast_out,
        out_shape=[jax.ShapeDtypeStruct(h.shape, h.dtype)] + cast_shapes,
        compiler_params=pltpu.CompilerParams(
            dimension_semantics=("arbitrary", "arbitrary"),
            vmem_limit_bytes=VMEM_LIMIT_BYTES),
        name="out_mlp",
    )(h, ya, yb, g, w_out, w1, w2, *[w for w, _ in casts])


def kernel(x, rel_bias_table, norm_g, even_w_in, even_ln_g, even_ln_b, even_spatial_w,
           even_spatial_b, even_pool_w, even_pool_scale, even_w_out, odd_w_in, odd_lambda,
           odd_subln_g, odd_conv_w, odd_w_out, ffn_w1, ffn_w2):
    depth = norm_g.shape[0]
    bf = lambda w: w.astype(_BF16)

    def layer_weights(layer):
        mixer = (even_w_in, even_w_out) if layer % 2 == 0 else (odd_w_in, odd_w_out)
        return [(mixer[0], layer // 2), (mixer[1], layer // 2),
                (ffn_w1, layer), (ffn_w2, layer)]

    first = layer_weights(0)
    w_in = bf(first[0][0][first[0][1]])
    w_out = w1 = w2 = None
    h = x
    for layer in range(depth):
        g = norm_g[layer]
        own_casts = [] if w_out is not None else layer_weights(layer)[1:]
        if layer % 2 == 0:
            e = layer // 2
            ya, yb, *cast = _even_mixer(
                h, g[0:1], w_in, even_ln_g[e][None], even_ln_b[e][None],
                even_spatial_w[e], even_spatial_b[e].T, bf(even_pool_w[e]),
                even_pool_scale[e][None], own_casts)
        else:
            o = layer // 2
            lambda_init = 0.8 - 0.6 * math.exp(-0.3 * layer)
            q, k, vt, yb = _odd_proj(h, g[0:1], w_in, odd_conv_w[o])
            ya = _diff_attention(q, k, vt, rel_bias_table, odd_lambda[o],
                                 odd_subln_g[o][:, None], lambda_init)
            cast = [bf(w[i]) for w, i in own_casts]
        if own_casts:
            w_out, w1, w2 = cast
        next_casts = layer_weights(layer + 1) if layer + 1 < depth else []
        h, *nxt = _out_mlp(h, ya, yb, g, w_out, w1, w2, next_casts)
        w_in, w_out, w1, w2 = nxt if nxt else (None,) * 4
    return h
```

```python
import functools
import math

import numpy as np
import jax
import jax.numpy as jnp
from jax import lax
from jax.experimental import pallas as pl
from jax.experimental.pallas import tpu as pltpu

D_MODEL = 1024
A_WIDTH = 512
A_GROUPS = 4
A_GROUP_DIM = 128
CHUNK = 128
B_WIDTH = 512
POOL_WINDOWS = (2, 4, 8, 16)
B_GROUP_DIM = 128
C_HEADS = 4
C_QK_DIM = 64
C_V_DIM = 128
C_WIDTH = 512
D_WIDTH = 512
CONV_WIDTH = 3
REL_BUCKETS = 32
REL_MAX_DIST = 128
D_FF = 4096
EPS = 1e-6
EVEN_IN = 2 * A_WIDTH + B_WIDTH
ODD_IN = 2 * C_WIDTH + C_WIDTH + 3 * D_WIDTH

ROW_TILE = 1024
PROJ_SUBTILES = 4
PROJ_SLOTS = 3
MLP_ROW_TILE = 1024
MLP_SUBTILES = 4
FF_CHUNK = 1024
ATT_TILE = 256
ATT_TILES_PER_STEP = 4
FAR_UNROLL = 8
SCORE_LOOKAHEAD = 1
KEY_CHUNK = 64
SUM_ROWS = 16
POOL_HALO = 16
CONV_HALO = 8
LOG2E = math.log2(math.e)
MASK_VALUE = -1e30
BF16_SUBLANES = 16
VMEM_LIMIT_BYTES = 52 * 1024 * 1024

_F32 = jnp.float32
_BF16 = jnp.bfloat16


def _rms(x, g):
    return x * lax.rsqrt(jnp.mean(x * x, axis=-1, keepdims=True) + EPS) * g


def _gelu_tanh(x):
    c = math.sqrt(2.0 / math.pi)
    hx = 0.5 * x
    return hx + hx * jnp.tanh(x * (c + (c * 0.044715) * (x * x)))


def _norm_project(x_ref, g_ref, w_ref, proj_ref):
    rows = x_ref.shape[0] // PROJ_SUBTILES
    slots = proj_ref.shape[0]
    ahead = slots - 1

    def project(r):
        hn = _rms(x_ref[r * rows:(r + 1) * rows, :], g_ref[...]).astype(_BF16)
        proj_ref[r % slots] = jnp.dot(hn, w_ref[...], preferred_element_type=_F32)

    for r in range(min(ahead, PROJ_SUBTILES)):
        project(r)
    for r in range(PROJ_SUBTILES):
        if r + ahead < PROJ_SUBTILES:
            project(r + ahead)
        yield r, proj_ref.at[r % slots]


def _const_spec(shape):
    nd = len(shape)
    return pl.BlockSpec(shape, lambda *_: (0,) * nd, pipeline_mode=pl.Buffered(1))


def _row_spec(width, col=0, tile=None):
    return pl.BlockSpec((None, tile or ROW_TILE, width), lambda b, j: (b, j, col))


def _cast_plan(weights, grid):
    steps = grid[0] * grid[1]
    in_specs, out_specs, out_shapes = [], [], []
    for w, layer in weights:
        _, rows, cols = w.shape
        slab = rows // steps
        assert slab * steps == rows and slab % BF16_SUBLANES == 0
        in_specs.append(pl.BlockSpec(
            (None, slab, cols), lambda b, j, layer=layer: (layer, b * grid[1] + j, 0)))
        out_specs.append(pl.BlockSpec((slab, cols), lambda b, j: (b * grid[1] + j, 0)))
        out_shapes.append(jax.ShapeDtypeStruct((rows, cols), _BF16))
    return in_specs, out_specs, out_shapes


def _cast_slabs(refs):
    n = len(refs) // 2
    for src, dst in zip(refs[:n], refs[n:]):
        dst[...] = src[...].astype(_BF16)


def _even_mixer_kernel(n_cast, x_ref, g_ref, w_in_ref, ln_g_ref, ln_b_ref, ws_ref, bs_ref,
                       pw_ref, ps_ref, *rest):
    cast_in, (ya_ref, yb_ref), rest = rest[:n_cast], rest[n_cast:n_cast + 2], rest[n_cast + 2:]
    cast_out, (carry_ref, proj_ref) = rest[:n_cast], rest[n_cast:]
    _cast_slabs(cast_in + cast_out)
    j = pl.program_id(1)
    rows = ROW_TILE // PROJ_SUBTILES

    @pl.when(j == 0)
    def _():
        carry_ref[...] = jnp.zeros_like(carry_ref)

    row = lax.broadcasted_iota(jnp.int32, (CHUNK, CHUNK), 0)
    col = lax.broadcasted_iota(jnp.int32, (CHUNK, CHUNK), 1)
    tril_w = [jnp.where(row >= col, ws_ref[g], 0.0).astype(_BF16) for g in range(A_GROUPS)]
    halo = carry_ref[...]
    for sub, proj in _norm_project(x_ref, g_ref, w_in_ref, proj_ref):
        base = sub * rows

        z = _gelu_tanh(proj[:, :2 * A_WIDTH])
        u = z[:, :A_WIDTH]
        v = z[:, A_WIDTH:]
        mu = jnp.mean(v, axis=-1, keepdims=True)
        vc = v - mu
        vn = vc * lax.rsqrt(jnp.mean(vc * vc, axis=-1, keepdims=True) + EPS)
        vn = (vn * ln_g_ref[...] + ln_b_ref[...]).astype(_BF16)
        for g in range(A_GROUPS):
            cs = slice(g * A_GROUP_DIM, (g + 1) * A_GROUP_DIM)
            bias = bs_ref[:, g:g + 1]
            for c in range(0, rows // CHUNK, 2):
                r0 = slice(c * CHUNK, (c + 1) * CHUNK)
                r1 = slice((c + 1) * CHUNK, (c + 2) * CHUNK)
                rhs = jnp.concatenate([vn[r0, cs], vn[r1, cs]], axis=1)
                mixed = jnp.dot(tril_w[g], rhs, preferred_element_type=_F32) + bias
                ya_ref[base + r0.start:base + r0.stop, cs] = (
                    u[r0, cs] * mixed[:, :A_GROUP_DIM]).astype(_BF16)
                ya_ref[base + r1.start:base + r1.stop, cs] = (
                    u[r1, cs] * mixed[:, A_GROUP_DIM:]).astype(_BF16)

        p = proj[:, 2 * A_WIDTH:]
        cur = jnp.concatenate([halo, p], axis=0)
        halo = p[rows - POOL_HALO:, :]
        cur_win = 1
        pos = j * ROW_TILE + base + lax.broadcasted_iota(jnp.int32, (rows, B_GROUP_DIM), 0)
        tokens_so_far = (pos + 1).astype(_F32)
        for g, win in enumerate(POOL_WINDOWS):
            cs = slice(g * B_GROUP_DIM, (g + 1) * B_GROUP_DIM)
            while cur_win < win:
                cur = cur + pltpu.roll(cur, cur_win, axis=0)
                cur_win *= 2
            assert cur_win == win and win <= POOL_HALO
            count = jnp.minimum(tokens_so_far, float(win))
            pooled = (cur[POOL_HALO:, :B_GROUP_DIM] / count - p[:, cs]).astype(_BF16)
            y = jnp.dot(pooled, pw_ref[g], preferred_element_type=_F32)
            yb_ref[base:base + rows, cs] = (y * ps_ref[:, cs]).astype(_BF16)
            cur = cur[:, B_GROUP_DIM:]
    carry_ref[...] = halo


def _even_mixer(h, g0, w_in, ln_g, ln_b, w_s, b_s_t, pool_w, pool_scale, casts):
    bsz, seq, _ = h.shape
    grid = (bsz, seq // ROW_TILE)
    out = jax.ShapeDtypeStruct((bsz, seq, A_WIDTH), _BF16)
    cast_in, cast_out, cast_shapes = _cast_plan(casts, grid)
    return pl.pallas_call(
        functools.partial(_even_mixer_kernel, len(casts)),
        grid=grid,
        in_specs=[
            _row_spec(D_MODEL),
            _const_spec((1, D_MODEL)),
            _const_spec((D_MODEL, EVEN_IN)),
            _const_spec((1, A_WIDTH)),
            _const_spec((1, A_WIDTH)),
            _const_spec((A_GROUPS, CHUNK, CHUNK)),
            _const_spec((CHUNK, A_GROUPS)),
            _const_spec((len(POOL_WINDOWS), B_GROUP_DIM, B_GROUP_DIM)),
            _const_spec((1, B_WIDTH)),
        ] + cast_in,
        out_specs=[_row_spec(A_WIDTH), _row_spec(B_WIDTH)] + cast_out,
        out_shape=[out, out] + cast_shapes,
        scratch_shapes=[
            pltpu.VMEM((POOL_HALO, B_WIDTH), _F32),
            pltpu.VMEM((PROJ_SLOTS, ROW_TILE // PROJ_SUBTILES, EVEN_IN), _F32),
        ],
        compiler_params=pltpu.CompilerParams(
            dimension_semantics=("arbitrary", "arbitrary"),
            vmem_limit_bytes=VMEM_LIMIT_BYTES),
        name="even_mixer",
    )(h, g0, w_in, ln_g, ln_b, w_s, b_s_t, pool_w, pool_scale, *[w for w, _ in casts])


def _odd_proj_kernel(x_ref, g_ref, w_in_ref, cw_ref, q_ref, k_ref, vt_ref, yd_ref,
                     carry_ref, proj_ref):
    j = pl.program_id(1)
    rows = ROW_TILE // PROJ_SUBTILES

    @pl.when(j == 0)
    def _():
        carry_ref[...] = jnp.zeros_like(carry_ref)

    halo = carry_ref[...]
    for sub, proj in _norm_project(x_ref, g_ref, w_in_ref, proj_ref):
        rs = slice(sub * rows, (sub + 1) * rows)
        q_ref[rs, :] = (proj[:, :C_WIDTH] * (C_QK_DIM ** -0.5 * LOG2E)).astype(_BF16)
        k_ref[rs, :] = proj[:, C_WIDTH:2 * C_WIDTH].astype(_BF16)
        for kt in range(rows // ATT_TILE):
            v_tile = proj[kt * ATT_TILE:(kt + 1) * ATT_TILE, 2 * C_WIDTH:3 * C_WIDTH]
            vt_ref[sub * (rows // ATT_TILE) + kt] = v_tile.T.astype(_BF16)
        o = 3 * C_WIDTH
        bg = proj[:, o:o + D_WIDTH]
        z = proj[:, o + D_WIDTH:o + 2 * D_WIDTH] * proj[:, o + 2 * D_WIDTH:]
        ext = jnp.concatenate([halo, z], axis=0)
        halo = z[rows - CONV_HALO:, :]
        y = cw_ref[CONV_WIDTH - 1:CONV_WIDTH, :] * z
        for t in range(CONV_WIDTH - 1):
            shift = CONV_WIDTH - 1 - t
            y = y + cw_ref[t:t + 1, :] * pltpu.roll(ext, shift, axis=0)[CONV_HALO:, :]
        yd_ref[rs, :] = (bg * y).astype(_BF16)
    carry_ref[...] = halo


def _odd_proj(h, g0, w_in, conv_w):
    bsz, seq, _ = h.shape
    grid = (bsz, seq // ROW_TILE)
    out = jax.ShapeDtypeStruct((bsz, seq, C_WIDTH), _BF16)
    assert (ROW_TILE // PROJ_SUBTILES) % ATT_TILE == 0
    out_t = jax.ShapeDtypeStruct((bsz, seq // ATT_TILE, C_WIDTH, ATT_TILE), _BF16)
    spec_t = pl.BlockSpec((None, ROW_TILE // ATT_TILE, C_WIDTH, ATT_TILE),
                          lambda b, j: (b, j, 0, 0))
    return pl.pallas_call(
        _odd_proj_kernel,
        grid=grid,
        in_specs=[
            _row_spec(D_MODEL),
            _const_spec((1, D_MODEL)),
            _const_spec((D_MODEL, ODD_IN)),
            _const_spec((CONV_WIDTH, D_WIDTH)),
        ],
        out_specs=[_row_spec(C_WIDTH), _row_spec(C_WIDTH), spec_t, _row_spec(D_WIDTH)],
        out_shape=[out, out, out_t, out],
        scratch_shapes=[
            pltpu.VMEM((CONV_HALO, D_WIDTH), _F32),
            pltpu.VMEM((PROJ_SLOTS, ROW_TILE // PROJ_SUBTILES, ODD_IN), _F32),
        ],
        compiler_params=pltpu.CompilerParams(
            dimension_semantics=("arbitrary", "arbitrary"),
            vmem_limit_bytes=VMEM_LIMIT_BYTES),
        name="odd_proj",
    )(h, g0, w_in, conv_w)


def _t5_bucket_upper_bounds(max_dist):
    d = np.arange(max_dist, dtype=np.int32)
    max_exact = REL_BUCKETS // 2
    nf = np.maximum(d, 1).astype(np.float32)
    large = max_exact + (np.log(nf / np.float32(max_exact))
                         / np.float32(math.log(REL_MAX_DIST / max_exact))
                         * np.float32(REL_BUCKETS - max_exact)).astype(np.int32)
    large = np.minimum(large, REL_BUCKETS - 1)
    bucket = np.where(d < max_exact, d, large)
    assert np.all(np.diff(bucket) >= 0)
    return bucket, {int(b): int(d[bucket == b].max()) for b in np.unique(bucket)}


def _attn_kernel(bucket_hi, lambda_init,
                 tab_ref, q_ref, k_ref, vt_ref, lam_ref, sg_ref, o_ref,
                 qq_ref, bias_ref, *scratch):
    t = ATT_TILE
    blk = pl.program_id(1)
    last_bucket = REL_BUCKETS - 1

    @pl.when(jnp.logical_and(pl.program_id(0) == 0, blk == 0))
    def _():
        kpos = lax.broadcasted_iota(jnp.int32, (t, t), 0)
        qpos = lax.broadcasted_iota(jnp.int32, (t, t), 1)
        for delta in range(2):
            d = qpos - kpos + delta * t
            for hd in range(C_HEADS):
                far = tab_ref[last_bucket, hd]
                val = jnp.zeros((t, t), _F32)
                for b in sorted(bucket_hi, reverse=True):
                    if b == last_bucket:
                        continue
                    val = jnp.where(d <= bucket_hi[b], (tab_ref[b, hd] - far) * LOG2E, val)
                val = jnp.where(d >= 0, val, MASK_VALUE)
                bias_ref[delta, hd] = jnp.concatenate([val, val], axis=1)

    def query_tile(sub, carry):
        rows = pl.ds(pl.multiple_of(sub * t, t), t)
        _attn_query_tile(lambda_init, blk * ATT_TILES_PER_STEP + sub,
                         q_ref.at[rows], k_ref, vt_ref, lam_ref, sg_ref, o_ref.at[sub],
                         qq_ref, bias_ref, *scratch)
        return carry

    lax.fori_loop(0, ATT_TILES_PER_STEP, query_tile, 0)


def _attn_query_tile(lambda_init, i, q_ref, k_ref, vt_ref, lam_ref, sg_ref, o_ref,
                     qq_ref, bias_ref, m_ref, acc_ref, s_ref, p_ref, al_ref):
    t = ATT_TILE
    hd_dim = 2 * C_QK_DIM
    heads = range(C_HEADS)

    feat = lax.broadcasted_iota(jnp.int32, (hd_dim, t), 0)
    for hd in heads:
        qt = q_ref[:, hd * hd_dim:(hd + 1) * hd_dim].astype(_F32).T
        qq_ref[hd] = jnp.concatenate(
            [jnp.where(feat < C_QK_DIM, qt, 0.0), jnp.where(feat >= C_QK_DIM, qt, 0.0)],
            axis=1).astype(_BF16)
    m_ref[...] = jnp.full_like(m_ref, MASK_VALUE)
    acc_ref[...] = jnp.zeros_like(acc_ref)
    p_ref[C_HEADS - 1] = jnp.zeros((t, 2 * t), _BF16)
    al_ref[C_HEADS - 1] = jnp.ones((1, 2 * t), _F32)

    def scores(jk, hd):
        start = pl.multiple_of(jk * t, t)
        k_t = k_ref[pl.ds(start, t), hd * hd_dim:(hd + 1) * hd_dim]
        s_ref[hd] = jnp.dot(k_t, qq_ref[hd], preferred_element_type=_F32)

    def values(jk, hd):
        vt = vt_ref[jk, hd * C_V_DIM:(hd + 1) * C_V_DIM, :]
        vt_ones = jnp.concatenate([vt, jnp.ones((SUM_ROWS, t), _BF16)], axis=0)
        acc_ref[hd] = al_ref[hd] * acc_ref[hd] + jnp.dot(
            vt_ones, p_ref[hd], preferred_element_type=_F32)

    def step(jk, delta, jk_next):
        for hd in heads:
            ahead = hd + SCORE_LOOKAHEAD
            if ahead < C_HEADS:
                scores(jk, ahead)
            elif jk_next is not None:
                scores(jk_next, ahead - C_HEADS)
            chunks = [slice(c, c + KEY_CHUNK) for c in range(0, t, KEY_CHUNK)]

            def chunk(rows):
                s = s_ref[hd, rows, :]
                return s if delta is None else s + bias_ref[delta, hd, rows, :]

            m_prev = m_ref[hd]
            m_next = m_prev
            for rows in chunks:
                m_next = jnp.maximum(m_next, jnp.max(chunk(rows), axis=0, keepdims=True))
            alpha = jnp.exp2(m_prev - m_next)
            m_ref[hd] = m_next
            if hd == 0:
                values(jnp.maximum(jk - 1, 0), C_HEADS - 1)
            else:
                values(jk, hd - 1)
            for rows in chunks:
                p_ref[hd, rows, :] = jnp.exp2(chunk(rows) - m_next).astype(_BF16)
            al_ref[hd] = alpha

    for hd in range(SCORE_LOOKAHEAD):
        scores(0, hd)

    n_far = jnp.maximum(i - 1, 0)

    def far_run(first, count):
        for n in range(count):
            step(first + n, None, first + n + 1)

    def far_trip(r, carry):
        far_run(FAR_UNROLL * r, FAR_UNROLL)
        return carry

    trips = n_far // FAR_UNROLL
    lax.fori_loop(0, trips, far_trip, 0)
    done = FAR_UNROLL * trips
    run = FAR_UNROLL // 2
    while run >= 2:
        take = (n_far - done) >= run
        pl.when(take)(functools.partial(far_run, done, run))
        done = done + jnp.where(take, run, 0)
        run //= 2

    def finish(odd_far_tile, sub_diagonal):
        if odd_far_tile:
            step(n_far - 1, None, n_far)
        if sub_diagonal:
            step(i - 1, 1, i)
        step(i, 0, None)
        values(i, C_HEADS - 1)
        lp = lam_ref[...]
        lam = (jnp.exp(jnp.sum(lp[0:1] * lp[1:2], axis=-1, keepdims=True))
               - jnp.exp(jnp.sum(lp[2:3] * lp[3:4], axis=-1, keepdims=True)) + lambda_init)
        for hd in heads:
            inv_l = 1.0 / acc_ref[hd, C_V_DIM:C_V_DIM + 1, :]
            a = (acc_ref[hd, :C_V_DIM, :t] * inv_l[:, :t]
                 - acc_ref[hd, :C_V_DIM, t:] * (lam * inv_l[:, t:]))
            ms = jnp.mean(a * a, axis=0, keepdims=True)
            y = a * (lax.rsqrt(ms + EPS) * (1.0 - lambda_init)) * sg_ref[...]
            o_ref[hd * C_V_DIM:(hd + 1) * C_V_DIM, :] = y.astype(_BF16)

    odd_far = n_far % 2 == 1
    pl.when(i == 0)(lambda: finish(False, False))
    pl.when(jnp.logical_and(i >= 1, jnp.logical_not(odd_far)))(lambda: finish(False, True))
    pl.when(odd_far)(lambda: finish(True, True))


def _diff_attention(q, k, vt, rel_table, lam_params, subln_g, lambda_init):
    bsz, seq, _ = q.shape
    t = ATT_TILE
    bucket, bucket_hi = _t5_bucket_upper_bounds(seq)
    assert np.all(bucket[t + 1:] == REL_BUCKETS - 1)
    rows = t * ATT_TILES_PER_STEP
    grid = (bsz, seq // rows)
    kernel = functools.partial(_attn_kernel, bucket_hi, lambda_init)
    return pl.pallas_call(
        kernel,
        grid=grid,
        in_specs=[
            pl.BlockSpec(memory_space=pltpu.SMEM),
            pl.BlockSpec((None, rows, C_WIDTH), lambda b, i: (b, i, 0)),
            pl.BlockSpec((None, seq, C_WIDTH), lambda b, i: (b, 0, 0)),
            pl.BlockSpec((None, seq // t, C_WIDTH, t), lambda b, i: (b, 0, 0, 0)),
            pl.BlockSpec((4, C_QK_DIM), lambda b, i: (0, 0)),
            pl.BlockSpec((C_V_DIM, 1), lambda b, i: (0, 0)),
        ],
        out_specs=pl.BlockSpec((None, ATT_TILES_PER_STEP, C_WIDTH, t), lambda b, i: (b, i, 0, 0)),
        out_shape=jax.ShapeDtypeStruct((bsz, seq // t, C_WIDTH, t), _BF16),
        scratch_shapes=[
            pltpu.VMEM((C_HEADS, 2 * C_QK_DIM, 2 * t), _BF16),
            pltpu.VMEM((2, C_HEADS, t, 2 * t), _F32),
            pltpu.VMEM((C_HEADS, 1, 2 * t), _F32),
            pltpu.VMEM((C_HEADS, C_V_DIM + SUM_ROWS, 2 * t), _F32),
            pltpu.VMEM((C_HEADS, t, 2 * t), _F32),
            pltpu.VMEM((C_HEADS, t, 2 * t), _BF16),
            pltpu.VMEM((C_HEADS, 1, 2 * t), _F32),
        ],
        compiler_params=pltpu.CompilerParams(
            dimension_semantics=("arbitrary", "arbitrary"),
            vmem_limit_bytes=VMEM_LIMIT_BYTES),
        name="diff_attention",
    )(rel_table, q, k, vt, lam_params, subln_g)


def _out_mlp_kernel(n_cast, ya_tiled, h_ref, ya_ref, yb_ref, g_ref, w_out_ref, w1_ref, w2_ref,
                    *rest):
    cast_in, o_ref, cast_out = rest[:n_cast], rest[n_cast], rest[n_cast + 1:]
    _cast_slabs(cast_in + cast_out)
    half = D_MODEL // 2
    sub = MLP_ROW_TILE // MLP_SUBTILES
    rows = [slice(r * sub, (r + 1) * sub) for r in range(MLP_SUBTILES)]

    def out_proj_a(r):
        if not ya_tiled:
            return jnp.dot(ya_ref[rows[r], :], w_out_ref[:half, :], preferred_element_type=_F32)
        per = sub // ATT_TILE
        parts = [lax.dot_general(ya_ref[r * per + kt], w_out_ref[:half, :],
                                 (((0,), (0,)), ((), ())), preferred_element_type=_F32)
                 for kt in range(per)]
        return parts[0] if per == 1 else jnp.concatenate(parts, axis=0)

    ys = [out_proj_a(r)
          + jnp.dot(yb_ref[rs, :], w_out_ref[half:, :], preferred_element_type=_F32)
          for r, rs in enumerate(rows)]
    h1s = [h_ref[rs, :] + _rms(y, g_ref[1:2, :]) for rs, y in zip(rows, ys)]
    hns = [_rms(h1, g_ref[2:3, :]).astype(_BF16) for h1 in h1s]
    for rs, h1, hn in zip(rows, h1s, hns):
        acc = jnp.zeros((sub, D_MODEL), _F32)
        for c in range(D_FF // FF_CHUNK):
            cs = slice(c * FF_CHUNK, (c + 1) * FF_CHUNK)
            a = jnp.dot(hn, w1_ref[:, cs], preferred_element_type=_F32)
            a = jnp.square(jnp.maximum(a, 0.0)).astype(_BF16)
            acc = acc + jnp.dot(a, w2_ref[cs, :], preferred_element_type=_F32)
        o_ref[rs, :] = h1 + _rms(acc, g_ref[3:4, :])


def _out_mlp(h, ya, yb, g, w_out, w1, w2, casts=()):
    bsz, seq, _ = h.shape
    grid = (bsz, seq // MLP_ROW_TILE)
    row_spec = functools.partial(_row_spec, tile=MLP_ROW_TILE)
    cast_in, cast_out, cast_shapes = _cast_plan(casts, grid)
    ya_tiled = ya.ndim == 4
    if ya_tiled:
        assert (MLP_ROW_TILE // MLP_SUBTILES) % ATT_TILE == 0
        ya_spec = pl.BlockSpec((None, MLP_ROW_TILE // ATT_TILE, D_MODEL // 2, ATT_TILE),
                               lambda b, j: (b, j, 0, 0))
    else:
        ya_spec = row_spec(D_MODEL // 2)
    return pl.pallas_call(
        functools.partial(_out_mlp_kernel, len(casts), ya_tiled),
        grid=grid,
        in_specs=[
            row_spec(D_MODEL),
            ya_spec,
            row_spec(D_MODEL // 2),
            _const_spec((4, D_MODEL)),
            _const_spec((D_MODEL, D_MODEL)),
            _const_spec((D_MODEL, D_FF)),
            _const_spec((D_FF, D_MODEL)),
        ] + cast_in,
        out_specs=[row_spec(D_MODEL)] + cast_out,
        out_shape=[jax.ShapeDtypeStruct(h.shape, h.dtype)] + cast_shapes,
        compiler_params=pltpu.CompilerParams(
            dimension_semantics=("arbitrary", "arbitrary"),
            vmem_limit_bytes=VMEM_LIMIT_BYTES),
        name="out_mlp",
    )(h, ya, yb, g, w_out, w1, w2, *[w for w, _ in casts])


def kernel(x, rel_bias_table, norm_g, even_w_in, even_ln_g, even_ln_b, even_spatial_w,
           even_spatial_b, even_pool_w, even_pool_scale, even_w_out, odd_w_in, odd_lambda,
           odd_subln_g, odd_conv_w, odd_w_out, ffn_w1, ffn_w2):
    depth = norm_g.shape[0]
    bf = lambda w: w.astype(_BF16)

    def layer_weights(layer):
        mixer = (even_w_in, even_w_out) if layer % 2 == 0 else (odd_w_in, odd_w_out)
        return [(mixer[0], layer // 2), (mixer[1], layer // 2),
                (ffn_w1, layer), (ffn_w2, layer)]

    first = layer_weights(0)
    w_in = bf(first[0][0][first[0][1]])
    w_out = w1 = w2 = None
    h = x
    for layer in range(depth):
        g = norm_g[layer]
        own_casts = [] if w_out is not None else layer_weights(layer)[1:]
        if layer % 2 == 0:
            e = layer // 2
            ya, yb, *cast = _even_mixer(
                h, g[0:1], w_in, even_ln_g[e][None], even_ln_b[e][None],
                even_spatial_w[e], even_spatial_b[e].T, bf(even_pool_w[e]),
                even_pool_scale[e][None], own_casts)
        else:
            o = layer // 2
            lambda_init = 0.8 - 0.6 * math.exp(-0.3 * layer)
            q, k, vt, yb = _odd_proj(h, g[0:1], w_in, odd_conv_w[o])
            ya = _diff_attention(q, k, vt, rel_bias_table, odd_lambda[o],
                                 odd_subln_g[o][:, None], lambda_init)
            cast = [bf(w[i]) for w, i in own_casts]
        if own_casts:
            w_out, w1, w2 = cast
        next_casts = layer_weights(layer + 1) if layer + 1 < depth else []
        h, *nxt = _out_mlp(h, ya, yb, g, w_out, w1, w2, next_casts)
        w_in, w_out, w1, w2 = nxt if nxt else (None,) * 4
    return h
```
